```python
import math
import jax, jax.numpy as jnp
from jax import lax
import numpy as np

D_MODEL = 1024
BATCH = 2
SEQ = 16384
DEPTH = 2

HEAD_DIM = 64
A_HEADS = 6
A_WIDTH = A_HEADS * HEAD_DIM
DILATED_PATTERNS = ((128, 1), (512, 4), (2048, 16))
B_HEADS = 6
B_NOPE = 64
B_ROPE = 32
B_VDIM = 64
Q_RANK = 192
KV_RANK = 128
B_WIDTH = B_HEADS * B_VDIM
ROPE_THETA = 10000.0
QBLOCK = 128
C_GROUPS = 4
C_GROUP_DIM = 64
C_WIDTH = C_GROUPS * C_GROUP_DIM
CHUNK = 128
MIX_WIDTH = A_WIDTH + B_WIDTH + C_WIDTH
D_FF = 4 * D_MODEL
ALPHA = (2 * DEPTH) ** 0.25
BETA = (8 * DEPTH) ** -0.25
EPS = 1e-5
NEG_INF = -1e30
OFF_CQ = 3 * A_WIDTH
OFF_CKV = OFF_CQ + Q_RANK
OFF_KPE = OFF_CKV + KV_RANK
OFF_U = OFF_KPE + B_ROPE
OFF_V = OFF_U + C_WIDTH
P_IN = OFF_V + C_WIDTH

kernel_name = 'hybrid_dilated_mla_sgu_deepnorm_encoder'


def _layer_norm(x, g, b):
    xf = x.astype(jnp.float32)
    mu = xf.mean(-1, keepdims=True)
    var = jnp.square(xf - mu).mean(-1, keepdims=True)
    return ((xf - mu) * lax.rsqrt(var + EPS)).astype(x.dtype) * g + b


def _rms(x):
    xf = x.astype(jnp.float32)
    return (xf * lax.rsqrt(jnp.mean(xf * xf, -1, keepdims=True) + EPS)).astype(x.dtype)


def _rope(t, cos, sin):
    t1, t2 = jnp.split(t, 2, axis=-1)
    return jnp.concatenate([t1 * cos - t2 * sin, t1 * sin + t2 * cos], axis=-1)


def _dilated_band_attention(q, k, v, window, dil, slopes):
    B, S, H, Dh = q.shape
    band = window // (2 * dil)
    L = S // dil
    nb = -(-L // band)
    Lp = nb * band

    def to_sub(t):
        return t.reshape(B, L, dil, H, Dh).transpose(0, 2, 3, 1, 4)

    qs = jnp.pad(to_sub(q), ((0, 0),) * 3 + ((0, Lp - L), (0, 0))).reshape(B, dil, H, nb, band, Dh)

    def windows(t):
        tp = jnp.pad(to_sub(t), ((0, 0),) * 3 + ((band, Lp - L + band), (0, 0)))
        tp = tp.reshape(B, dil, H, nb + 2, band, Dh)
        return jnp.concatenate([tp[:, :, :, 0:nb], tp[:, :, :, 1:nb + 1], tp[:, :, :, 2:nb + 2]], axis=4)

    kw, vw = windows(k), windows(v)
    s = jnp.einsum('brhnqc,brhnkc->brhnqk', qs, kw).astype(jnp.float32) * (Dh ** -0.5)
    a = jnp.arange(band)[:, None]
    j = jnp.arange(3 * band)[None, :]
    off = j - band - a
    kpos = (jnp.arange(nb)[:, None, None] - 1) * band + j[None]
    valid = (jnp.abs(off) <= band)[None] & (kpos >= 0) & (kpos < L)
    bias = -(slopes[:, None, None, None] * (dil * jnp.abs(off)).astype(jnp.float32))
    s = jnp.where(valid, s + bias, NEG_INF)
    m = s.max(-1, keepdims=True)
    p = jnp.exp(s - m)
    den = p.sum(-1, keepdims=True)
    o = jnp.einsum('brhnqk,brhnkc->brhnqc', (p / den).astype(v.dtype), vw)
    lse = (m + jnp.log(den))[..., 0]
    o = o.reshape(B, dil, H, Lp, Dh)[:, :, :, :L].transpose(0, 3, 1, 2, 4).reshape(B, S, H, Dh)
    lse = lse.reshape(B, dil, H, Lp)[..., :L].transpose(0, 3, 1, 2).reshape(B, S, H)
    return o, lse


def _dilated_attention(q, k, v, slopes):
    res = [_dilated_band_attention(q, k, v, w, d, slopes) for (w, d) in DILATED_PATTERNS]
    outs = jnp.stack([r[0] for r in res], 0)
    wts = jax.nn.softmax(jnp.stack([r[1] for r in res], 0), axis=0)
    return jnp.einsum('pbsh,pbshc->bshc', wts.astype(q.dtype), outs)


def _mla_attention(q, k, v):
    B, S, H, Dq = q.shape
    nq = S // QBLOCK
    qb = q.reshape(B, nq, QBLOCK, H, Dq).transpose(1, 0, 2, 3, 4)
    scale = Dq ** -0.5

    def block(qi):
        s = jnp.einsum('bqhc,bkhc->bhqk', qi, k).astype(jnp.float32) * scale
        p = jax.nn.softmax(s, axis=-1).astype(v.dtype)
        return jnp.einsum('bhqk,bkhc->bqhc', p, v)

    o = lax.map(block, qb)
    return o.transpose(1, 0, 2, 3, 4).reshape(B, S, H, v.shape[-1])


def _spatial_gating(u, v, g, b, w_s, b_s):
    v = _layer_norm(v, g, b)
    B, S, _ = v.shape
    vc = v.reshape(B, S // CHUNK, CHUNK, C_GROUPS, C_GROUP_DIM)
    mixed = jnp.einsum('gts,bnsgc->bntgc', w_s, vc) + b_s.T[None, None, :, :, None]
    return u * mixed.reshape(B, S, C_WIDTH)


def _layer(x, w_in, q_norm, w_q_up, kv_norm, w_kv_up, sgu_ln_g, sgu_ln_b, sgu_w, sgu_b, mix_norm,
           w_out, ln1_g, ln1_b, w_ff1, b_ff1, w_ff2, b_ff2, ln2_g, ln2_b, cos, sin, slopes):
    B, S, _ = x.shape
    h = x @ w_in
    qa = h[..., 0:A_WIDTH].reshape(B, S, A_HEADS, HEAD_DIM)
    ka = h[..., A_WIDTH:2 * A_WIDTH].reshape(B, S, A_HEADS, HEAD_DIM)
    va = h[..., 2 * A_WIDTH:3 * A_WIDTH].reshape(B, S, A_HEADS, HEAD_DIM)
    y_a = _dilated_attention(qa, ka, va, slopes).reshape(B, S, A_WIDTH)
    c_q = _rms(h[..., OFF_CQ:OFF_CKV]) * q_norm
    qb = (c_q @ w_q_up).reshape(B, S, B_HEADS, B_NOPE + B_ROPE)
    q_pe = _rope(qb[..., B_NOPE:], cos[:, None, :], sin[:, None, :])
    c_kv = _rms(h[..., OFF_CKV:OFF_KPE]) * kv_norm
    kv = (c_kv @ w_kv_up).reshape(B, S, B_HEADS, B_NOPE + B_VDIM)
    k_pe = _rope(h[..., OFF_KPE:OFF_U], cos, sin)
    q_full = jnp.concatenate([qb[..., :B_NOPE], q_pe], axis=-1)
    k_full = jnp.concatenate([kv[..., :B_NOPE],
                              jnp.broadcast_to(k_pe[:, :, None, :], (B, S, B_HEADS, B_ROPE))], axis=-1)
    y_b = _mla_attention(q_full, k_full, kv[..., B_NOPE:]).reshape(B, S, B_WIDTH)
    z = jax.nn.gelu(h[..., OFF_U:P_IN])
    y_c = _spatial_gating(z[..., :C_WIDTH], z[..., C_WIDTH:], sgu_ln_g, sgu_ln_b, sgu_w, sgu_b)
    y = jnp.concatenate([_rms(y_a), _rms(y_b), _rms(y_c)], axis=-1) * mix_norm
    x = _layer_norm(ALPHA * x + y @ w_out, ln1_g, ln1_b)
    f = jnp.square(jax.nn.relu(x @ w_ff1 + b_ff1)) @ w_ff2 + b_ff2
    return _layer_norm(ALPHA * x + f, ln2_g, ln2_b)


def setup_inputs(seed: int = 0) -> dict:
    key = jax.random.key(seed)
    ks = jax.random.split(key, 24)
    f32 = jnp.float32
    nrm = lambda k, shape, scale: jax.random.normal(k, shape, f32) * scale
    gain = lambda k, shape: 1.0 + 0.02 * jax.random.normal(k, shape, f32)
    return {
        'x': jax.random.normal(ks[0], (BATCH, SEQ, D_MODEL), f32),
        'w_in': nrm(ks[1], (DEPTH, D_MODEL, P_IN), D_MODEL ** -0.5),
        'q_norm': gain(ks[2], (DEPTH, Q_RANK)),
        'w_q_up': nrm(ks[3], (DEPTH, Q_RANK, B_HEADS * (B_NOPE + B_ROPE)), Q_RANK ** -0.5),
        'kv_norm': gain(ks[4], (DEPTH, KV_RANK)),
        'w_kv_up': nrm(ks[5], (DEPTH, KV_RANK, B_HEADS * (B_NOPE + B_VDIM)), KV_RANK ** -0.5),
        'sgu_ln_g': gain(ks[6], (DEPTH, C_WIDTH)),
        'sgu_ln_b': nrm(ks[7], (DEPTH, C_WIDTH), 0.02),
        'sgu_w': nrm(ks[8], (DEPTH, C_GROUPS, CHUNK, CHUNK), CHUNK ** -0.5),
        'sgu_b': gain(ks[9], (DEPTH, C_GROUPS, CHUNK)),
        'mix_norm': gain(ks[10], (DEPTH, MIX_WIDTH)),
        'w_out': nrm(ks[11], (DEPTH, MIX_WIDTH, D_MODEL), BETA * MIX_WIDTH ** -0.5),
        'ln1_g': gain(ks[12], (DEPTH, D_MODEL)),
        'ln1_b': nrm(ks[13], (DEPTH, D_MODEL), 0.02),
        'w_ff1': nrm(ks[14], (DEPTH, D_MODEL, D_FF), BETA * D_MODEL ** -0.5),
        'b_ff1': nrm(ks[15], (DEPTH, D_FF), 0.02),
        'w_ff2': nrm(ks[16], (DEPTH, D_FF, D_MODEL), BETA * D_FF ** -0.5),
        'b_ff2': nrm(ks[17], (DEPTH, D_MODEL), 0.02),
        'ln2_g': gain(ks[18], (DEPTH, D_MODEL)),
        'ln2_b': nrm(ks[19], (DEPTH, D_MODEL), 0.02),
    }


def reference(x, w_in, q_norm, w_q_up, kv_norm, w_kv_up, sgu_ln_g, sgu_ln_b, sgu_w, sgu_b, mix_norm,
              w_out, ln1_g, ln1_b, w_ff1, b_ff1, w_ff2, b_ff2, ln2_g, ln2_b):
    S = x.shape[1]
    inv_freq = ROPE_THETA ** (-jnp.arange(0, B_ROPE, 2, dtype=jnp.float32) / B_ROPE)
    ang = jnp.arange(S, dtype=jnp.float32)[:, None] * inv_freq[None, :]
    cos = jnp.cos(ang).astype(x.dtype)
    sin = jnp.sin(ang).astype(x.dtype)
    slopes = jnp.asarray(2.0 ** (-8.0 * np.arange(1, A_HEADS + 1) / A_HEADS), dtype=jnp.float32)
    for l in range(DEPTH):
        x = _layer(x, w_in[l], q_norm[l], w_q_up[l], kv_norm[l], w_kv_up[l], sgu_ln_g[l], sgu_ln_b[l],
                   sgu_w[l], sgu_b[l], mix_norm[l], w_out[l], ln1_g[l], ln1_b[l], w_ff1[l], b_ff1[l],
                   w_ff2[l], b_ff2[l], ln2_g[l], ln2_b[l], cos, sin, slopes)
    return x
```

```python
import functools
import math

import numpy as np
import jax
import jax.numpy as jnp
from jax import lax
from jax.experimental import pallas as pl
from jax.experimental.pallas import tpu as pltpu

BF16 = jnp.bfloat16
F32 = jnp.float32

D_MODEL = 1024
DEPTH = 2
HEAD_DIM = 64
A_HEADS = 6
A_WIDTH = A_HEADS * HEAD_DIM
DILATED_PATTERNS = ((128, 1), (512, 4), (2048, 16))
B_HEADS = 6
B_NOPE = 64
B_ROPE = 32
B_VDIM = 64
Q_RANK = 192
KV_RANK = 128
B_WIDTH = B_HEADS * B_VDIM
ROPE_THETA = 10000.0
C_GROUPS = 4
C_GROUP_DIM = 64
C_WIDTH = C_GROUPS * C_GROUP_DIM
CHUNK = 128
D_FF = 4 * D_MODEL
ALPHA = (2 * DEPTH) ** 0.25
EPS = 1e-5
NEG = -1e30
LOG2E = math.log2(math.e)

LANES = 128
BF16_SUBLANES = 16
VT_ROWS = B_VDIM + BF16_SUBLANES
HEAD_PAD = LANES

P_QA, P_KA, P_VA = 0, A_WIDTH, 2 * A_WIDTH
P_CKV = 3 * A_WIDTH
P_CQ = P_CKV + KV_RANK
P_KPE = P_CQ + 256
P_U = P_KPE + LANES
P_V = P_U + C_WIDTH
P_PACKED = P_V + C_WIDTH

HALO = max((w // (2 * d)) * d for w, d in DILATED_PATTERNS)
TQ_A = 256
WIN_A = TQ_A + 2 * HALO
BIAS_ROWS = WIN_A + 2 * HALO

VMEM_LIMIT = 48 * 1024 * 1024


def _rms_scale(x):
    return x * lax.rsqrt(jnp.mean(x * x, axis=-1, keepdims=True) + EPS)


def _layer_norm(x, g, b):
    mu = jnp.mean(x, axis=-1, keepdims=True)
    xc = x - mu
    var = jnp.mean(xc * xc, axis=-1, keepdims=True)
    return xc * lax.rsqrt(var + EPS) * g + b


def _dot(a, b):
    return jnp.dot(a, b, preferred_element_type=F32)


def _inproj_kernel(x_ref, w_ref, qn_ref, wq_ref, kvn_ref, wk_ref, wv_ref, cq_ref, sq_ref, ck_ref, sk_ref,
                   lng_ref, lnb_ref, ws_ref, bs_ref, mnc_ref,
                   qaT_ref, ka_ref, vaT_ref, qbT_ref, kb_ref, vbT_ref, yc_ref):
    tm = x_ref.shape[0]
    h = _dot(x_ref[...].astype(BF16), w_ref[...])

    ones_tile = (lax.broadcasted_iota(jnp.int32, (BF16_SUBLANES, tm), 0) == 0).astype(BF16)
    zeros_half = jnp.zeros((HEAD_DIM, tm), BF16)

    def put_values_t(vT, out_ref, width):
        for hd in range(A_HEADS):
            out_ref[hd * VT_ROWS:hd * VT_ROWS + width, :] = vT[hd * width:(hd + 1) * width].astype(BF16)
            out_ref[hd * VT_ROWS + width:(hd + 1) * VT_ROWS, :] = ones_tile

    qaT = (h[:, P_QA:P_QA + A_WIDTH] * (HEAD_DIM ** -0.5 * LOG2E)).T.astype(BF16)
    for hd in range(A_HEADS):
        q_h = qaT[hd * HEAD_DIM:(hd + 1) * HEAD_DIM]
        lo, hi = (q_h, zeros_half) if hd % 2 == 0 else (zeros_half, q_h)
        qaT_ref[hd * HEAD_PAD:hd * HEAD_PAD + HEAD_DIM, :] = lo
        qaT_ref[hd * HEAD_PAD + HEAD_DIM:(hd + 1) * HEAD_PAD, :] = hi
    ka_ref[...] = h[:, P_KA:P_KA + A_WIDTH].astype(BF16)
    put_values_t(h[:, P_VA:P_VA + A_WIDTH].T, vaT_ref, HEAD_DIM)

    c_kv = (_rms_scale(h[:, P_CKV:P_CKV + KV_RANK]) * kvn_ref[...]).astype(BF16)
    kmat = _dot(c_kv, wk_ref[...])
    vmat = _dot(c_kv, wv_ref[...])
    g_k = h[:, P_KPE:P_KPE + LANES]
    kpe = g_k * ck_ref[...] + pltpu.roll(g_k, 96, 1) * sk_ref[...]
    c_q = (_rms_scale(h[:, P_CQ:P_CQ + Q_RANK]) * qn_ref[...]).astype(BF16)
    qmat = _dot(c_q, wq_ref[...])
    cq_t, sq_t = cq_ref[...], sq_ref[...]
    for hd in range(B_HEADS):
        sl = slice(hd * HEAD_PAD, (hd + 1) * HEAD_PAD)
        kb_ref[:, sl] = (kmat[:, sl] + kpe).astype(BF16)
        g_q = qmat[:, sl]
        q_h = g_q * cq_t + pltpu.roll(g_q, 96, 1) * sq_t
        qbT_ref[sl, :] = q_h.T.astype(BF16)
    put_values_t(vmat.T, vbT_ref, B_VDIM)

    z = jax.nn.gelu(h[:, P_U:P_U + 2 * C_WIDTH])
    u = z[:, :C_WIDTH]
    v = _layer_norm(z[:, C_WIDTH:], lng_ref[...], lnb_ref[...])
    group = lax.broadcasted_iota(jnp.int32, (CHUNK, C_WIDTH), 1) // C_GROUP_DIM
    for c in range(tm // CHUNK):
        rows = slice(c * CHUNK, (c + 1) * CHUNK)
        vc = v[rows]
        stack = jnp.concatenate([jnp.where(group == g, vc, 0.0) for g in range(C_GROUPS)], axis=0).astype(BF16)
        mixed = _dot(ws_ref[...], stack) + bs_ref[...]
        yc = u[rows] * mixed
        yc_ref[rows, :] = (_rms_scale(yc) * mnc_ref[...]).astype(BF16)


def _inproj(x2, pw, tabs, s_len, tm):
    t_len = x2.shape[0]
    nt = t_len // tm
    ns = s_len // tm
    row = lambda i: (i, 0)
    col = lambda i: (0, i)
    fixed = lambda i: (0, 0)
    pos = lambda i: (i % ns, 0)
    full = lambda a: pl.BlockSpec(a.shape, fixed)
    tab = pl.BlockSpec((tm, LANES), pos)
    in_specs = [pl.BlockSpec((tm, D_MODEL), row), full(pw['w_in']), full(pw['q_norm']), full(pw['w_q']),
                full(pw['kv_norm']), full(pw['w_k']), full(pw['w_v']), tab, tab, tab, tab,
                full(pw['sgu_g']), full(pw['sgu_b']), full(pw['ws_cat']), full(pw['bs_tab']), full(pw['mn_c'])]
    out_shape = [jax.ShapeDtypeStruct((A_HEADS * HEAD_PAD, t_len), BF16),
                 jax.ShapeDtypeStruct((t_len, A_WIDTH), BF16),
                 jax.ShapeDtypeStruct((A_HEADS * VT_ROWS, t_len), BF16),
                 jax.ShapeDtypeStruct((B_HEADS * HEAD_PAD, t_len), BF16),
                 jax.ShapeDtypeStruct((t_len, B_HEADS * HEAD_PAD), BF16),
                 jax.ShapeDtypeStruct((B_HEADS * VT_ROWS, t_len), BF16),
                 jax.ShapeDtypeStruct((t_len, C_WIDTH), BF16)]
    out_specs = [pl.BlockSpec((A_HEADS * HEAD_PAD, tm), col),
                 pl.BlockSpec((tm, A_WIDTH), row),
                 pl.BlockSpec((A_HEADS * VT_ROWS, tm), col),
                 pl.BlockSpec((B_HEADS * HEAD_PAD, tm), col),
                 pl.BlockSpec((tm, B_HEADS * HEAD_PAD), row),
                 pl.BlockSpec((B_HEADS * VT_ROWS, tm), col),
                 pl.BlockSpec((tm, C_WIDTH), row)]
    return pl.pallas_call(
        _inproj_kernel, grid=(nt,), in_specs=in_specs, out_specs=out_specs, out_shape=out_shape,
        compiler_params=pltpu.CompilerParams(dimension_semantics=("arbitrary",), vmem_limit_bytes=VMEM_LIMIT),
        name="inproj",
    )(x2, pw['w_in'], pw['q_norm'], pw['w_q'], pw['kv_norm'], pw['w_k'], pw['w_v'],
      tabs['cq'], tabs['sq'], tabs['ck'], tabs['sk'],
      pw['sgu_g'], pw['sgu_b'], pw['ws_cat'], pw['bs_tab'], pw['mn_c'])


def _dilated_kernel(qT_ref, k_ref, vT_ref, bias_ref, o_ref, *, s_len):
    qi = pl.program_id(2)
    t0 = qi * TQ_A
    w0 = jnp.clip(t0 - HALO, 0, s_len - WIN_A)
    b0 = w0 - t0 + 2 * HALO
    w0 = pl.multiple_of(w0, TQ_A)
    b0 = pl.multiple_of(b0, TQ_A)
    k = k_ref[pl.ds(w0, WIN_A), :]
    s = _dot(k, qT_ref[...]) + bias_ref[0, pl.ds(b0, WIN_A), :]
    m = jnp.max(s, axis=0, keepdims=True)
    p = jnp.exp2(s - m).astype(BF16)
    acc = _dot(vT_ref[:, pl.ds(w0, WIN_A)], p)
    o_ref[...] = acc[:HEAD_DIM] / acc[HEAD_DIM:HEAD_DIM + 1]


def _dilated(qaT, ka, vaT, bias, batch, s_len):
    nq = s_len // TQ_A
    t_len = batch * s_len
    return pl.pallas_call(
        functools.partial(_dilated_kernel, s_len=s_len),
        grid=(batch, A_HEADS, nq),
        in_specs=[pl.BlockSpec((HEAD_PAD, TQ_A), lambda b, h, q: (h, b * nq + q)),
                  pl.BlockSpec((s_len, LANES), lambda b, h, q: (b, h // 2)),
                  pl.BlockSpec((VT_ROWS, s_len), lambda b, h, q: (h, b)),
                  pl.BlockSpec((1, BIAS_ROWS, TQ_A), lambda b, h, q: (h, 0, 0))],
        out_specs=pl.BlockSpec((HEAD_DIM, TQ_A), lambda b, h, q: (h, b * nq + q)),
        out_shape=jax.ShapeDtypeStruct((A_WIDTH, t_len), F32),
        compiler_params=pltpu.CompilerParams(dimension_semantics=("arbitrary",) * 3, vmem_limit_bytes=VMEM_LIMIT),
        name="dilated",
    )(qaT, ka, vaT, bias)


def _mla_kernel(qT_ref, k_ref, vT_ref, o_ref, acc_ref, *, tk, nk):
    qT = qT_ref[...]
    tq = qT.shape[1]
    acc_ref[...] = jnp.zeros_like(acc_ref)

    def body(j, m_prev):
        off = pl.multiple_of(j * tk, tk)
        s = _dot(k_ref[pl.ds(off, tk), :], qT)
        m_new = jnp.maximum(m_prev, jnp.max(s, axis=0, keepdims=True))
        alpha = jnp.exp2(m_prev - m_new)
        p = jnp.exp2(s - m_new).astype(BF16)
        acc_ref[...] = acc_ref[...] * alpha + _dot(vT_ref[:, pl.ds(off, tk)], p)
        return m_new

    lax.fori_loop(0, nk, body, jnp.full((1, tq), NEG, F32))
    acc = acc_ref[...]
    o_ref[...] = acc[:B_VDIM] / acc[B_VDIM:B_VDIM + 1]


def _mla(qbT, kb, vbT, batch, s_len, tq, tk):
    nq = s_len // tq
    nk = s_len // tk
    t_len = batch * s_len
    return pl.pallas_call(
        functools.partial(_mla_kernel, tk=tk, nk=nk),
        grid=(batch, B_HEADS, nq),
        in_specs=[pl.BlockSpec((HEAD_PAD, tq), lambda b, h, q: (h, b * nq + q)),
                  pl.BlockSpec((s_len, HEAD_PAD), lambda b, h, q: (b, h)),
                  pl.BlockSpec((VT_ROWS, s_len), lambda b, h, q: (h, b))],
        out_specs=pl.BlockSpec((B_VDIM, tq), lambda b, h, q: (h, b * nq + q)),
        out_shape=jax.ShapeDtypeStruct((B_WIDTH, t_len), F32),
        scratch_shapes=[pltpu.VMEM((VT_ROWS, tq), F32)],
        compiler_params=pltpu.CompilerParams(dimension_semantics=("arbitrary",) * 3, vmem_limit_bytes=VMEM_LIMIT),
        name="mla",
    )(qbT, kb, vbT)


def _merge_kernel(oaT_ref, obT_ref, yc_ref, x_ref, wa_ref, wb_ref, wc_ref, mna_ref, mnb_ref, g_ref, b_ref, o_ref):
    ya = (_rms_scale(oaT_ref[...].T) * mna_ref[...]).astype(BF16)
    yb = (_rms_scale(obT_ref[...].T) * mnb_ref[...]).astype(BF16)
    y = _dot(ya, wa_ref[...]) + _dot(yb, wb_ref[...]) + _dot(yc_ref[...], wc_ref[...])
    o_ref[...] = _layer_norm(ALPHA * x_ref[...] + y, g_ref[...], b_ref[...])


def _merge(oaT, obT, yc, x2, pw, tm):
    t_len = x2.shape[0]
    row = lambda i: (i, 0)
    col = lambda i: (0, i)
    full = lambda a: pl.BlockSpec(a.shape, lambda i: (0, 0))
    return pl.pallas_call(
        _merge_kernel, grid=(t_len // tm,),
        in_specs=[pl.BlockSpec((A_WIDTH, tm), col), pl.BlockSpec((B_WIDTH, tm), col),
                  pl.BlockSpec((tm, C_WIDTH), row), pl.BlockSpec((tm, D_MODEL), row),
                  full(pw['wo_a']), full(pw['wo_b']), full(pw['wo_c']), full(pw['mn_a']), full(pw['mn_b']),
                  full(pw['ln1_g']), full(pw['ln1_b'])],
        out_specs=pl.BlockSpec((tm, D_MODEL), row),
        out_shape=jax.ShapeDtypeStruct((t_len, D_MODEL), F32),
        compiler_params=pltpu.CompilerParams(dimension_semantics=("arbitrary",), vmem_limit_bytes=VMEM_LIMIT),
        name="merge",
    )(oaT, obT, yc, x2, pw['wo_a'], pw['wo_b'], pw['wo_c'], pw['mn_a'], pw['mn_b'], pw['ln1_g'], pw['ln1_b'])


def _ffn_kernel(x_ref, w1_ref, b1_ref, w2_ref, b2_ref, g_ref, b_ref, o_ref, acc_ref):
    j = pl.program_id(1)
    hid = _dot(x_ref[...].astype(BF16), w1_ref[...]) + b1_ref[...]
    hid = jnp.square(jnp.maximum(hid, 0.0)).astype(BF16)
    part = _dot(hid, w2_ref[...])

    @pl.when(j == 0)
    def _():
        acc_ref[...] = part

    @pl.when(j > 0)
    def _():
        acc_ref[...] += part

    @pl.when(j == pl.num_programs(1) - 1)
    def _():
        o_ref[...] = _layer_norm(ALPHA * x_ref[...] + acc_ref[...] + b2_ref[...], g_ref[...], b_ref[...])


def _ffn(x2, pw, tm, tf):
    t_len = x2.shape[0]
    fixed = lambda i, j: (0, 0)
    return pl.pallas_call(
        _ffn_kernel, grid=(t_len // tm, D_FF // tf),
        in_specs=[pl.BlockSpec((tm, D_MODEL), lambda i, j: (i, 0)),
                  pl.BlockSpec((D_MODEL, tf), lambda i, j: (0, j)),
                  pl.BlockSpec((1, tf), lambda i, j: (0, j)),
                  pl.BlockSpec((tf, D_MODEL), lambda i, j: (j, 0)),
                  pl.BlockSpec((1, D_MODEL), fixed), pl.BlockSpec((1, D_MODEL), fixed),
                  pl.BlockSpec((1, D_MODEL), fixed)],
        out_specs=pl.BlockSpec((tm, D_MODEL), lambda i, j: (i, 0)),
        out_shape=jax.ShapeDtypeStruct((t_len, D_MODEL), F32),
        scratch_shapes=[pltpu.VMEM((tm, D_MODEL), F32)],
        compiler_params=pltpu.CompilerParams(dimension_semantics=("arbitrary", "arbitrary"),
                                             vmem_limit_bytes=VMEM_LIMIT),
        name="ffn",
    )(x2, pw['w_ff1'], pw['b_ff1'], pw['w_ff2'], pw['b_ff2'], pw['ln2_g'], pw['ln2_b'])


def _swap_halves(w):
    half = w.shape[-1] // 2
    return jnp.concatenate([w[..., half:], w[..., :half]], axis=-1)


def _pack_layer(l, w_in, q_norm, w_q_up, kv_norm, w_kv_up, sgu_ln_g, sgu_ln_b, sgu_w, sgu_b, mix_norm,
                w_out, ln1_g, ln1_b, w_ff1, b_ff1, w_ff2, b_ff2, ln2_g, ln2_b):
    w = w_in[l]
    zc = lambda n: jnp.zeros((D_MODEL, n), F32)
    off_cq = 3 * A_WIDTH
    off_ckv = off_cq + Q_RANK
    off_kpe = off_ckv + KV_RANK
    off_u = off_kpe + B_ROPE
    kpe = w[:, off_kpe:off_u]
    w_packed = jnp.concatenate([w[:, :off_cq], w[:, off_ckv:off_kpe], w[:, off_cq:off_ckv], zc(64),
                                zc(64), kpe, _swap_halves(kpe), w[:, off_u:]], axis=1)
    wq = w_q_up[l].reshape(Q_RANK, B_HEADS, B_NOPE + B_ROPE)
    wq = jnp.concatenate([wq, _swap_halves(wq[..., B_NOPE:])], axis=-1).reshape(Q_RANK, B_HEADS * HEAD_PAD)
    wkv = w_kv_up[l].reshape(KV_RANK, B_HEADS, B_NOPE + B_VDIM)
    wk = jnp.concatenate([wkv[..., :B_NOPE], jnp.zeros((KV_RANK, B_HEADS, HEAD_PAD - B_NOPE), F32)], axis=-1)
    row = lambda a: a.reshape(1, -1)
    mn = mix_norm[l]
    return {
        'w_in': w_packed.astype(BF16),
        'q_norm': row(q_norm[l]), 'w_q': wq.astype(BF16),
        'kv_norm': row(kv_norm[l]), 'w_k': wk.reshape(KV_RANK, B_HEADS * HEAD_PAD).astype(BF16),
        'w_v': wkv[..., B_NOPE:].reshape(KV_RANK, B_WIDTH).astype(BF16),
        'sgu_g': row(sgu_ln_g[l]), 'sgu_b': row(sgu_ln_b[l]),
        'ws_cat': jnp.concatenate([sgu_w[l][g] for g in range(C_GROUPS)], axis=1).astype(BF16),
        'bs_tab': jnp.repeat(sgu_b[l].T, C_GROUP_DIM, axis=1),
        'mn_a': row(mn[:A_WIDTH]), 'mn_b': row(mn[A_WIDTH:A_WIDTH + B_WIDTH]), 'mn_c': row(mn[A_WIDTH + B_WIDTH:]),
        'wo_a': w_out[l][:A_WIDTH].astype(BF16), 'wo_b': w_out[l][A_WIDTH:A_WIDTH + B_WIDTH].astype(BF16),
        'wo_c': w_out[l][A_WIDTH + B_WIDTH:].astype(BF16),
        'ln1_g': row(ln1_g[l]), 'ln1_b': row(ln1_b[l]),
        'w_ff1': w_ff1[l].astype(BF16), 'b_ff1': row(b_ff1[l]), 'w_ff2': w_ff2[l].astype(BF16),
        'b_ff2': row(b_ff2[l]), 'ln2_g': row(ln2_g[l]), 'ln2_b': row(ln2_b[l]),
    }


def _rope_tables(s_len):
    inv_freq = ROPE_THETA ** (-jnp.arange(0, B_ROPE, 2, dtype=F32) / B_ROPE)
    ang = jnp.arange(s_len, dtype=F32)[:, None] * inv_freq[None, :]
    cos, sin = jnp.cos(ang), jnp.sin(ang)
    ones = jnp.ones((s_len, B_NOPE), F32)
    zeros = lambda n: jnp.zeros((s_len, n), F32)
    pad = HEAD_PAD - B_NOPE - B_ROPE
    c_tab = jnp.concatenate([cos, cos, zeros(pad)], axis=1)
    s_tab = jnp.concatenate([-sin, sin, zeros(pad)], axis=1)
    q_scale = (B_NOPE + B_ROPE) ** -0.5 * LOG2E
    return {'cq': q_scale * jnp.concatenate([ones, c_tab], axis=1),
            'sq': q_scale * jnp.concatenate([zeros(B_NOPE), s_tab], axis=1),
            'ck': jnp.concatenate([zeros(B_NOPE), c_tab], axis=1),
            'sk': jnp.concatenate([zeros(B_NOPE), s_tab], axis=1)}


def _dilated_bias():
    e = jnp.arange(BIAS_ROWS)[:, None]
    ql = jnp.arange(TQ_A)[None, :]
    dist = jnp.abs(e - 2 * HALO - ql)
    cnt = sum(((dist % d == 0) & (dist <= (w // (2 * d)) * d)).astype(F32) for w, d in DILATED_PATTERNS)
    base = jnp.where(cnt > 0, jnp.log2(jnp.maximum(cnt, 1.0)), NEG)
    slopes = jnp.asarray(2.0 ** (-8.0 * np.arange(1, A_HEADS + 1) / A_HEADS), dtype=F32)
    return base[None] - (slopes * LOG2E)[:, None, None] * dist.astype(F32)[None]


def _tile(n, pref):
    t = min(pref, n)
    assert n % t == 0
    return t


def kernel(x, w_in, q_norm, w_q_up, kv_norm, w_kv_up, sgu_ln_g, sgu_ln_b, sgu_w, sgu_b, mix_norm, w_out,
           ln1_g, ln1_b, w_ff1, b_ff1, w_ff2, b_ff2, ln2_g, ln2_b):
    batch, s_len, _ = x.shape
    assert s_len >= WIN_A and s_len % TQ_A == 0
    t_len = batch * s_len
    tabs = _rope_tables(s_len)
    bias = _dilated_bias()
    x2 = x.reshape(t_len, D_MODEL)
    for l in range(DEPTH):
        pw = _pack_layer(l, w_in, q_norm, w_q_up, kv_norm, w_kv_up, sgu_ln_g, sgu_ln_b, sgu_w, sgu_b, mix_norm,
                         w_out, ln1_g, ln1_b, w_ff1, b_ff1, w_ff2, b_ff2, ln2_g, ln2_b)
        qaT, ka, vaT, qbT, kb, vbT, yc = _inproj(x2, pw, tabs, s_len, _tile(s_len, 512))
        oaT = _dilated(qaT, ka, vaT, bias, batch, s_len)
        obT = _mla(qbT, kb, vbT, batch, s_len, _tile(s_len, 512), _tile(s_len, 512))
        x1 = _merge(oaT, obT, yc, x2, pw, _tile(t_len, 512))
        x2 = _ffn(x1, pw, _tile(t_len, 1024), 1024)
    return x2.reshape(batch, s_len, D_MODEL)
```

```python
import functools
import math

import numpy as np
import jax
import jax.numpy as jnp
from jax import lax
from jax.experimental import pallas as pl
from jax.experimental.pallas import tpu as pltpu

BF16 = jnp.bfloat16
F32 = jnp.float32

D_MODEL = 1024
DEPTH = 2
HEAD_DIM = 64
A_HEADS = 6
A_WIDTH = A_HEADS * HEAD_DIM
DILATED_PATTERNS = ((128, 1), (512, 4), (2048, 16))
B_HEADS = 6
B_NOPE = 64
B_ROPE = 32
B_VDIM = 64
Q_RANK = 192
KV_RANK = 128
B_WIDTH = B_HEADS * B_VDIM
ROPE_THETA = 10000.0
C_GROUPS = 4
C_GROUP_DIM = 64
C_WIDTH = C_GROUPS * C_GROUP_DIM
CHUNK = 128
D_FF = 4 * D_MODEL
ALPHA = (2 * DEPTH) ** 0.25
EPS = 1e-5
NEG = -1e30
LOG2E = math.log2(math.e)

LANES = 128
BF16_SUBLANES = 16
VT_ROWS = B_VDIM + BF16_SUBLANES
HEAD_PAD = LANES

P_QA, P_KA, P_VA = 0, A_WIDTH, 2 * A_WIDTH
P_CKV = 3 * A_WIDTH
P_CQ = P_CKV + KV_RANK
P_KPE = P_CQ + 256
P_U = P_KPE + LANES
P_V = P_U + C_WIDTH
P_PACKED = P_V + C_WIDTH

HALO = max((w // (2 * d)) * d for w, d in DILATED_PATTERNS)
TQ_A = 256
WIN_A = TQ_A + 2 * HALO
BIAS_ROWS = WIN_A + 2 * HALO

VMEM_LIMIT = 48 * 1024 * 1024


def _rms_scale(x):
    return x * lax.rsqrt(jnp.mean(x * x, axis=-1, keepdims=True) + EPS)


def _layer_norm(x, g, b):
    mu = jnp.mean(x, axis=-1, keepdims=True)
    xc = x - mu
    var = jnp.mean(xc * xc, axis=-1, keepdims=True)
    return xc * lax.rsqrt(var + EPS) * g + b


def _dot(a, b):
    return jnp.dot(a, b, preferred_element_type=F32)


def _inproj_kernel(x_ref, w_ref, qn_ref, wq_ref, kvn_ref, wk_ref, wv_ref, cq_ref, sq_ref, ck_ref, sk_ref,
                   lng_ref, lnb_ref, ws_ref, bs_ref, mnc_ref,
                   qaT_ref, ka_ref, vaT_ref, qbT_ref, kb_ref, vbT_ref, yc_ref):
    tm = x_ref.shape[0]
    h = _dot(x_ref[...].astype(BF16), w_ref[...])

    ones_tile = (lax.broadcasted_iota(jnp.int32, (BF16_SUBLANES, tm), 0) == 0).astype(BF16)
    zeros_half = jnp.zeros((HEAD_DIM, tm), BF16)

    def put_values_t(vT, out_ref, width):
        for hd in range(A_HEADS):
            out_ref[hd * VT_ROWS:hd * VT_ROWS + width, :] = vT[hd * width:(hd + 1) * width].astype(BF16)
            out_ref[hd * VT_ROWS + width:(hd + 1) * VT_ROWS, :] = ones_tile

    qaT = (h[:, P_QA:P_QA + A_WIDTH] * (HEAD_DIM ** -0.5 * LOG2E)).T.astype(BF16)
    for hd in range(A_HEADS):
        q_h = qaT[hd * HEAD_DIM:(hd + 1) * HEAD_DIM]
        lo, hi = (q_h, zeros_half) if hd % 2 == 0 else (zeros_half, q_h)
        qaT_ref[hd * HEAD_PAD:hd * HEAD_PAD + HEAD_DIM, :] = lo
        qaT_ref[hd * HEAD_PAD + HEAD_DIM:(hd + 1) * HEAD_PAD, :] = hi
    ka_ref[...] = h[:, P_KA:P_KA + A_WIDTH].astype(BF16)
    put_values_t(h[:, P_VA:P_VA + A_WIDTH].T, vaT_ref, HEAD_DIM)

    c_kv = (_rms_scale(h[:, P_CKV:P_CKV + KV_RANK]) * kvn_ref[...]).astype(BF16)
    kmat = _dot(c_kv, wk_ref[...])
    vmat = _dot(c_kv, wv_ref[...])
    g_k = h[:, P_KPE:P_KPE + LANES]
    kpe = g_k * ck_ref[...] + pltpu.roll(g_k, 96, 1) * sk_ref[...]
    c_q = (_rms_scale(h[:, P_CQ:P_CQ + Q_RANK]) * qn_ref[...]).astype(BF16)
    qmat = _dot(c_q, wq_ref[...])
    cq_t, sq_t = cq_ref[...], sq_ref[...]
    for hd in range(B_HEADS):
        sl = slice(hd * HEAD_PAD, (hd + 1) * HEAD_PAD)
        kb_ref[:, sl] = (kmat[:, sl] + kpe).astype(BF16)
        g_q = qmat[:, sl]
        q_h = g_q * cq_t + pltpu.roll(g_q, 96, 1) * sq_t
        qbT_ref[sl, :] = q_h.T.astype(BF16)
    put_values_t(vmat.T, vbT_ref, B_VDIM)

    z = jax.nn.gelu(h[:, P_U:P_U + 2 * C_WIDTH])
    u = z[:, :C_WIDTH]
    v = _layer_norm(z[:, C_WIDTH:], lng_ref[...], lnb_ref[...])
    group = lax.broadcasted_iota(jnp.int32, (CHUNK, C_WIDTH), 1) // C_GROUP_DIM
    for c in range(tm // CHUNK):
        rows = slice(c * CHUNK, (c + 1) * CHUNK)
        vc = v[rows]
        stack = jnp.concatenate([jnp.where(group == g, vc, 0.0) for g in range(C_GROUPS)], axis=0).astype(BF16)
        mixed = _dot(ws_ref[...], stack) + bs_ref[...]
        yc = u[rows] * mixed
        yc_ref[rows, :] = (_rms_scale(yc) * mnc_ref[...]).astype(BF16)


def _inproj(x2, pw, tabs, s_len, tm):
    t_len = x2.shape[0]
    nt = t_len // tm
    ns = s_len // tm
    row = lambda i: (i, 0)
    col = lambda i: (0, i)
    fixed = lambda i: (0, 0)
    pos = lambda i: (i % ns, 0)
    full = lambda a: pl.BlockSpec(a.shape, fixed)
    tab = pl.BlockSpec((tm, LANES), pos)
    in_specs = [pl.BlockSpec((tm, D_MODEL), row), full(pw['w_in']), full(pw['q_norm']), full(pw['w_q']),
                full(pw['kv_norm']), full(pw['w_k']), full(pw['w_v']), tab, tab, tab, tab,
                full(pw['sgu_g']), full(pw['sgu_b']), full(pw['ws_cat']), full(pw['bs_tab']), full(pw['mn_c'])]
    out_shape = [jax.ShapeDtypeStruct((A_HEADS * HEAD_PAD, t_len), BF16),
                 jax.ShapeDtypeStruct((t_len, A_WIDTH), BF16),
                 jax.ShapeDtypeStruct((A_HEADS * VT_ROWS, t_len), BF16),
                 jax.ShapeDtypeStruct((B_HEADS * HEAD_PAD, t_len), BF16),
                 jax.ShapeDtypeStruct((t_len, B_HEADS * HEAD_PAD), BF16),
                 jax.ShapeDtypeStruct((B_HEADS * VT_ROWS, t_len), BF16),
                 jax.ShapeDtypeStruct((t_len, C_WIDTH), BF16)]
    out_specs = [pl.BlockSpec((A_HEADS * HEAD_PAD, tm), col),
                 pl.BlockSpec((tm, A_WIDTH), row),
                 pl.BlockSpec((A_HEADS * VT_ROWS, tm), col),
                 pl.BlockSpec((B_HEADS * HEAD_PAD, tm), col),
                 pl.BlockSpec((tm, B_HEADS * HEAD_PAD), row),
                 pl.BlockSpec((B_HEADS * VT_ROWS, tm), col),
                 pl.BlockSpec((tm, C_WIDTH), row)]
    return pl.pallas_call(
        _inproj_kernel, grid=(nt,), in_specs=in_specs, out_specs=out_specs, out_shape=out_shape,
        compiler_params=pltpu.CompilerParams(dimension_semantics=("arbitrary",), vmem_limit_bytes=VMEM_LIMIT),
        name="inproj",
    )(x2, pw['w_in'], pw['q_norm'], pw['w_q'], pw['kv_norm'], pw['w_k'], pw['w_v'],
      tabs['cq'], tabs['sq'], tabs['ck'], tabs['sk'],
      pw['sgu_g'], pw['sgu_b'], pw['ws_cat'], pw['bs_tab'], pw['mn_c'])


def _dilated_kernel(qT_ref, k_ref, vT_ref, bias_ref, o_ref, *, s_len):
    qi = pl.program_id(2)
    t0 = qi * TQ_A
    w0 = jnp.clip(t0 - HALO, 0, s_len - WIN_A)
    b0 = w0 - t0 + 2 * HALO
    w0 = pl.multiple_of(w0, TQ_A)
    b0 = pl.multiple_of(b0, TQ_A)
    k = k_ref[pl.ds(w0, WIN_A), :]
    s = _dot(k, qT_ref[...]) + bias_ref[0, pl.ds(b0, WIN_A), :]
    m = jnp.max(s, axis=0, keepdims=True)
    p = jnp.exp2(s - m).astype(BF16)
    acc = _dot(vT_ref[:, pl.ds(w0, WIN_A)], p)
    o_ref[...] = acc[:HEAD_DIM] / acc[HEAD_DIM:HEAD_DIM + 1]


def _dilated(qaT, ka, vaT, bias, batch, s_len):
    nq = s_len // TQ_A
    t_len = batch * s_len
    return pl.pallas_call(
        functools.partial(_dilated_kernel, s_len=s_len),
        grid=(batch, A_HEADS, nq),
        in_specs=[pl.BlockSpec((HEAD_PAD, TQ_A), lambda b, h, q: (h, b * nq + q)),
                  pl.BlockSpec((s_len, LANES), lambda b, h, q: (b, h // 2)),
                  pl.BlockSpec((VT_ROWS, s_len), lambda b, h, q: (h, b)),
                  pl.BlockSpec((1, BIAS_ROWS, TQ_A), lambda b, h, q: (h, 0, 0))],
        out_specs=pl.BlockSpec((HEAD_DIM, TQ_A), lambda b, h, q: (h, b * nq + q)),
        out_shape=jax.ShapeDtypeStruct((A_WIDTH, t_len), F32),
        compiler_params=pltpu.CompilerParams(dimension_semantics=("arbitrary",) * 3, vmem_limit_bytes=VMEM_LIMIT),
        name="dilated",
    )(qaT, ka, vaT, bias)


def _mla_kernel(qT_ref, k_ref, vT_ref, o_ref, acc_ref, s_ref, *, tk, nk):
    qT = qT_ref[...]
    tq = qT.shape[1]
    acc_ref[...] = jnp.zeros_like(acc_ref)

    def scores(j, slot):
        off = pl.multiple_of(j * tk, tk)
        s = _dot(k_ref[pl.ds(off, tk), :], qT)
        s_ref[slot] = s
        return jnp.max(s, axis=0, keepdims=True)

    def step(j, slot, m_prev, m_tile):
        m_tile_next = scores(jnp.minimum(j + 1, nk - 1), 1 - slot)
        off = pl.multiple_of(j * tk, tk)
        m_new = jnp.maximum(m_prev, m_tile)
        alpha = jnp.exp2(m_prev - m_new)
        p = jnp.exp2(s_ref[slot] - m_new).astype(BF16)
        acc_ref[...] = acc_ref[...] * alpha + _dot(vT_ref[:, pl.ds(off, tk)], p)
        return m_new, m_tile_next

    def body(i, carry):
        m, m_tile = carry
        m, m_tile = step(2 * i, 0, m, m_tile)
        return step(2 * i + 1, 1, m, m_tile)

    m_tile0 = scores(0, 0)
    lax.fori_loop(0, nk // 2, body, (jnp.full((1, tq), NEG, F32), m_tile0))
    acc = acc_ref[...]
    o_ref[...] = acc[:B_VDIM] / acc[B_VDIM:B_VDIM + 1]


def _mla(qbT, kb, vbT, batch, s_len, tq, tk):
    nq = s_len // tq
    nk = s_len // tk
    assert nk % 2 == 0
    t_len = batch * s_len
    return pl.pallas_call(
        functools.partial(_mla_kernel, tk=tk, nk=nk),
        grid=(batch, B_HEADS, nq),
        in_specs=[pl.BlockSpec((HEAD_PAD, tq), lambda b, h, q: (h, b * nq + q)),
                  pl.BlockSpec((s_len, HEAD_PAD), lambda b, h, q: (b, h)),
                  pl.BlockSpec((VT_ROWS, s_len), lambda b, h, q: (h, b))],
        out_specs=pl.BlockSpec((B_VDIM, tq), lambda b, h, q: (h, b * nq + q)),
        out_shape=jax.ShapeDtypeStruct((B_WIDTH, t_len), F32),
        scratch_shapes=[pltpu.VMEM((VT_ROWS, tq), F32), pltpu.VMEM((2, tk, tq), F32)],
        compiler_params=pltpu.CompilerParams(dimension_semantics=("arbitrary",) * 3, vmem_limit_bytes=VMEM_LIMIT),
        name="mla",
    )(qbT, kb, vbT)


def _merge_kernel(oaT_ref, obT_ref, yc_ref, x_ref, wa_ref, wb_ref, wc_ref, mna_ref, mnb_ref, g_ref, b_ref, o_ref):
    ya = (_rms_scale(oaT_ref[...].T) * mna_ref[...]).astype(BF16)
    yb = (_rms_scale(obT_ref[...].T) * mnb_ref[...]).astype(BF16)
    y = _dot(ya, wa_ref[...]) + _dot(yb, wb_ref[...]) + _dot(yc_ref[...], wc_ref[...])
    o_ref[...] = _layer_norm(ALPHA * x_ref[...] + y, g_ref[...], b_ref[...])


def _merge(oaT, obT, yc, x2, pw, tm):
    t_len = x2.shape[0]
    row = lambda i: (i, 0)
    col = lambda i: (0, i)
    full = lambda a: pl.BlockSpec(a.shape, lambda i: (0, 0))
    return pl.pallas_call(
        _merge_kernel, grid=(t_len // tm,),
        in_specs=[pl.BlockSpec((A_WIDTH, tm), col), pl.BlockSpec((B_WIDTH, tm), col),
                  pl.BlockSpec((tm, C_WIDTH), row), pl.BlockSpec((tm, D_MODEL), row),
                  full(pw['wo_a']), full(pw['wo_b']), full(pw['wo_c']), full(pw['mn_a']), full(pw['mn_b']),
                  full(pw['ln1_g']), full(pw['ln1_b'])],
        out_specs=pl.BlockSpec((tm, D_MODEL), row),
        out_shape=jax.ShapeDtypeStruct((t_len, D_MODEL), F32),
        compiler_params=pltpu.CompilerParams(dimension_semantics=("arbitrary",), vmem_limit_bytes=VMEM_LIMIT),
        name="merge",
    )(oaT, obT, yc, x2, pw['wo_a'], pw['wo_b'], pw['wo_c'], pw['mn_a'], pw['mn_b'], pw['ln1_g'], pw['ln1_b'])


def _ffn_kernel(x_ref, w1_ref, b1_ref, w2_ref, b2_ref, g_ref, b_ref, o_ref, acc_ref):
    j = pl.program_id(1)
    hid = _dot(x_ref[...].astype(BF16), w1_ref[...]) + b1_ref[...]
    hid = jnp.square(jnp.maximum(hid, 0.0)).astype(BF16)
    part = _dot(hid, w2_ref[...])

    @pl.when(j == 0)
    def _():
        acc_ref[...] = part

    @pl.when(j > 0)
    def _():
        acc_ref[...] += part

    @pl.when(j == pl.num_programs(1) - 1)
    def _():
        o_ref[...] = _layer_norm(ALPHA * x_ref[...] + acc_ref[...] + b2_ref[...], g_ref[...], b_ref[...])


def _ffn(x2, pw, tm, tf):
    t_len = x2.shape[0]
    fixed = lambda i, j: (0, 0)
    return pl.pallas_call(
        _ffn_kernel, grid=(t_len // tm, D_FF // tf),
        in_specs=[pl.BlockSpec((tm, D_MODEL), lambda i, j: (i, 0)),
                  pl.BlockSpec((D_MODEL, tf), lambda i, j: (0, j)),
                  pl.BlockSpec((1, tf), lambda i, j: (0, j)),
                  pl.BlockSpec((tf, D_MODEL), lambda i, j: (j, 0)),
                  pl.BlockSpec((1, D_MODEL), fixed), pl.BlockSpec((1, D_MODEL), fixed),
                  pl.BlockSpec((1, D_MODEL), fixed)],
        out_specs=pl.BlockSpec((tm, D_MODEL), lambda i, j: (i, 0)),
        out_shape=jax.ShapeDtypeStruct((t_len, D_MODEL), F32),
        scratch_shapes=[pltpu.VMEM((tm, D_MODEL), F32)],
        compiler_params=pltpu.CompilerParams(dimension_semantics=("arbitrary", "arbitrary"),
                                             vmem_limit_bytes=VMEM_LIMIT),
        name="ffn",
    )(x2, pw['w_ff1'], pw['b_ff1'], pw['w_ff2'], pw['b_ff2'], pw['ln2_g'], pw['ln2_b'])


def _swap_halves(w):
    half = w.shape[-1] // 2
    return jnp.concatenate([w[..., half:], w[..., :half]], axis=-1)


def _pack_layer(l, w_in, q_norm, w_q_up, kv_norm, w_kv_up, sgu_ln_g, sgu_ln_b, sgu_w, sgu_b, mix_norm,
                w_out, ln1_g, ln1_b, w_ff1, b_ff1, w_ff2, b_ff2, ln2_g, ln2_b):
    w = w_in[l]
    zc = lambda n: jnp.zeros((D_MODEL, n), F32)
    off_cq = 3 * A_WIDTH
    off_ckv = off_cq + Q_RANK
    off_kpe = off_ckv + KV_RANK
    off_u = off_kpe + B_ROPE
    kpe = w[:, off_kpe:off_u]
    w_packed = jnp.concatenate([w[:, :off_cq], w[:, off_ckv:off_kpe], w[:, off_cq:off_ckv], zc(64),
                                zc(64), kpe, _swap_halves(kpe), w[:, off_u:]], axis=1)
    wq = w_q_up[l].reshape(Q_RANK, B_HEADS, B_NOPE + B_ROPE)
    wq = jnp.concatenate([wq, _swap_halves(wq[..., B_NOPE:])], axis=-1).reshape(Q_RANK, B_HEADS * HEAD_PAD)
    wkv = w_kv_up[l].reshape(KV_RANK, B_HEADS, B_NOPE + B_VDIM)
    wk = jnp.concatenate([wkv[..., :B_NOPE], jnp.zeros((KV_RANK, B_HEADS, HEAD_PAD - B_NOPE), F32)], axis=-1)
    row = lambda a: a.reshape(1, -1)
    mn = mix_norm[l]
    return {
        'w_in': w_packed.astype(BF16),
        'q_norm': row(q_norm[l]), 'w_q': wq.astype(BF16),
        'kv_norm': row(kv_norm[l]), 'w_k': wk.reshape(KV_RANK, B_HEADS * HEAD_PAD).astype(BF16),
        'w_v': wkv[..., B_NOPE:].reshape(KV_RANK, B_WIDTH).astype(BF16),
        'sgu_g': row(sgu_ln_g[l]), 'sgu_b': row(sgu_ln_b[l]),
        'ws_cat': jnp.concatenate([sgu_w[l][g] for g in range(C_GROUPS)], axis=1).astype(BF16),
        'bs_tab': jnp.repeat(sgu_b[l].T, C_GROUP_DIM, axis=1),
        'mn_a': row(mn[:A_WIDTH]), 'mn_b': row(mn[A_WIDTH:A_WIDTH + B_WIDTH]), 'mn_c': row(mn[A_WIDTH + B_WIDTH:]),
        'wo_a': w_out[l][:A_WIDTH].astype(BF16), 'wo_b': w_out[l][A_WIDTH:A_WIDTH + B_WIDTH].astype(BF16),
        'wo_c': w_out[l][A_WIDTH + B_WIDTH:].astype(BF16),
        'ln1_g': row(ln1_g[l]), 'ln1_b': row(ln1_b[l]),
        'w_ff1': w_ff1[l].astype(BF16), 'b_ff1': row(b_ff1[l]), 'w_ff2': w_ff2[l].astype(BF16),
        'b_ff2': row(b_ff2[l]), 'ln2_g': row(ln2_g[l]), 'ln2_b': row(ln2_b[l]),
    }


def _rope_tables(s_len):
    inv_freq = ROPE_THETA ** (-jnp.arange(0, B_ROPE, 2, dtype=F32) / B_ROPE)
    ang = jnp.arange(s_len, dtype=F32)[:, None] * inv_freq[None, :]
    cos, sin = jnp.cos(ang), jnp.sin(ang)
    ones = jnp.ones((s_len, B_NOPE), F32)
    zeros = lambda n: jnp.zeros((s_len, n), F32)
    pad = HEAD_PAD - B_NOPE - B_ROPE
    c_tab = jnp.concatenate([cos, cos, zeros(pad)], axis=1)
    s_tab = jnp.concatenate([-sin, sin, zeros(pad)], axis=1)
    q_scale = (B_NOPE + B_ROPE) ** -0.5 * LOG2E
    return {'cq': q_scale * jnp.concatenate([ones, c_tab], axis=1),
            'sq': q_scale * jnp.concatenate([zeros(B_NOPE), s_tab], axis=1),
            'ck': jnp.concatenate([zeros(B_NOPE), c_tab], axis=1),
            'sk': jnp.concatenate([zeros(B_NOPE), s_tab], axis=1)}


def _dilated_bias():
    e = jnp.arange(BIAS_ROWS)[:, None]
    ql = jnp.arange(TQ_A)[None, :]
    dist = jnp.abs(e - 2 * HALO - ql)
    cnt = sum(((dist % d == 0) & (dist <= (w // (2 * d)) * d)).astype(F32) for w, d in DILATED_PATTERNS)
    base = jnp.where(cnt > 0, jnp.log2(jnp.maximum(cnt, 1.0)), NEG)
    slopes = jnp.asarray(2.0 ** (-8.0 * np.arange(1, A_HEADS + 1) / A_HEADS), dtype=F32)
    return base[None] - (slopes * LOG2E)[:, None, None] * dist.astype(F32)[None]


def _tile(n, pref):
    t = min(pref, n)
    assert n % t == 0
    return t


def kernel(x, w_in, q_norm, w_q_up, kv_norm, w_kv_up, sgu_ln_g, sgu_ln_b, sgu_w, sgu_b, mix_norm, w_out,
           ln1_g, ln1_b, w_ff1, b_ff1, w_ff2, b_ff2, ln2_g, ln2_b):
    batch, s_len, _ = x.shape
    assert s_len >= WIN_A and s_len % TQ_A == 0
    t_len = batch * s_len
    tabs = _rope_tables(s_len)
    bias = _dilated_bias()
    x2 = x.reshape(t_len, D_MODEL)
    for l in range(DEPTH):
        pw = _pack_layer(l, w_in, q_norm, w_q_up, kv_norm, w_kv_up, sgu_ln_g, sgu_ln_b, sgu_w, sgu_b, mix_norm,
                         w_out, ln1_g, ln1_b, w_ff1, b_ff1, w_ff2, b_ff2, ln2_g, ln2_b)
        qaT, ka, vaT, qbT, kb, vbT, yc = _inproj(x2, pw, tabs, s_len, _tile(s_len, 512))
        oaT = _dilated(qaT, ka, vaT, bias, batch, s_len)
        obT = _mla(qbT, kb, vbT, batch, s_len, _tile(s_len, 512), _tile(s_len, 512))
        x1 = _merge(oaT, obT, yc, x2, pw, _tile(t_len, 512))
        x2 = _ffn(x1, pw, _tile(t_len, 1024), 1024)
    return x2.reshape(batch, s_len, D_MODEL)
```

```python
import functools
import math

import numpy as np
import jax
import jax.numpy as jnp
from jax import lax
from jax.experimental import pallas as pl
from jax.experimental.pallas import tpu as pltpu

BF16 = jnp.bfloat16
F32 = jnp.float32

D_MODEL = 1024
DEPTH = 2
HEAD_DIM = 64
A_HEADS = 6
A_WIDTH = A_HEADS * HEAD_DIM
DILATED_PATTERNS = ((128, 1), (512, 4), (2048, 16))
B_HEADS = 6
B_NOPE = 64
B_ROPE = 32
B_VDIM = 64
Q_RANK = 192
KV_RANK = 128
B_WIDTH = B_HEADS * B_VDIM
ROPE_THETA = 10000.0
C_GROUPS = 4
C_GROUP_DIM = 64
C_WIDTH = C_GROUPS * C_GROUP_DIM
CHUNK = 128
D_FF = 4 * D_MODEL
ALPHA = (2 * DEPTH) ** 0.25
EPS = 1e-5
NEG = -1e30
LOG2E = math.log2(math.e)

LANES = 128
BF16_SUBLANES = 16
VT_ROWS = B_VDIM + BF16_SUBLANES
HEAD_PAD = LANES

P_QA, P_KA, P_VA = 0, A_WIDTH, 2 * A_WIDTH
P_CKV = 3 * A_WIDTH
P_CQ = P_CKV + KV_RANK
P_KPE = P_CQ + 256
P_U = P_KPE + LANES
P_V = P_U + C_WIDTH
P_PACKED = P_V + C_WIDTH

HALO = max((w // (2 * d)) * d for w, d in DILATED_PATTERNS)
TQ_A = 256
WIN_A = TQ_A + 2 * HALO
BIAS_ROWS = WIN_A + 2 * HALO

VMEM_LIMIT = 48 * 1024 * 1024
MLA_UNROLL = 4


def _rms_scale(x):
    return x * lax.rsqrt(jnp.mean(x * x, axis=-1, keepdims=True) + EPS)


def _layer_norm(x, g, b):
    mu = jnp.mean(x, axis=-1, keepdims=True)
    xc = x - mu
    var = jnp.mean(xc * xc, axis=-1, keepdims=True)
    return xc * lax.rsqrt(var + EPS) * g + b


def _dot(a, b):
    return jnp.dot(a, b, preferred_element_type=F32)


def _inproj_kernel(x_ref, w_ref, qn_ref, wq_ref, kvn_ref, wk_ref, wv_ref, cq_ref, sq_ref, ck_ref, sk_ref,
                   lng_ref, lnb_ref, ws_ref, bs_ref, mnc_ref,
                   qaT_ref, ka_ref, vaT_ref, qbT_ref, kb_ref, vbT_ref, yc_ref):
    tm = x_ref.shape[0]
    h = _dot(x_ref[...].astype(BF16), w_ref[...])

    ones_tile = (lax.broadcasted_iota(jnp.int32, (BF16_SUBLANES, tm), 0) == 0).astype(BF16)
    zeros_half = jnp.zeros((HEAD_DIM, tm), BF16)

    def put_values_t(vT, out_ref, width):
        for hd in range(A_HEADS):
            out_ref[hd * VT_ROWS:hd * VT_ROWS + width, :] = vT[hd * width:(hd + 1) * width].astype(BF16)
            out_ref[hd * VT_ROWS + width:(hd + 1) * VT_ROWS, :] = ones_tile

    qaT = (h[:, P_QA:P_QA + A_WIDTH] * (HEAD_DIM ** -0.5 * LOG2E)).T.astype(BF16)
    for hd in range(A_HEADS):
        q_h = qaT[hd * HEAD_DIM:(hd + 1) * HEAD_DIM]
        lo, hi = (q_h, zeros_half) if hd % 2 == 0 else (zeros_half, q_h)
        qaT_ref[hd * HEAD_PAD:hd * HEAD_PAD + HEAD_DIM, :] = lo
        qaT_ref[hd * HEAD_PAD + HEAD_DIM:(hd + 1) * HEAD_PAD, :] = hi
    ka_ref[...] = h[:, P_KA:P_KA + A_WIDTH].astype(BF16)
    put_values_t(h[:, P_VA:P_VA + A_WIDTH].T, vaT_ref, HEAD_DIM)

    c_kv = (_rms_scale(h[:, P_CKV:P_CKV + KV_RANK]) * kvn_ref[...]).astype(BF16)
    kmat = _dot(c_kv, wk_ref[...])
    vmat = _dot(c_kv, wv_ref[...])
    g_k = h[:, P_KPE:P_KPE + LANES]
    kpe = g_k * ck_ref[...] + pltpu.roll(g_k, 96, 1) * sk_ref[...]
    c_q = (_rms_scale(h[:, P_CQ:P_CQ + Q_RANK]) * qn_ref[...]).astype(BF16)
    qmat = _dot(c_q, wq_ref[...])
    cq_t, sq_t = cq_ref[...], sq_ref[...]
    for hd in range(B_HEADS):
        sl = slice(hd * HEAD_PAD, (hd + 1) * HEAD_PAD)
        kb_ref[:, sl] = (kmat[:, sl] + kpe).astype(BF16)
        g_q = qmat[:, sl]
        q_h = g_q * cq_t + pltpu.roll(g_q, 96, 1) * sq_t
        qbT_ref[sl, :] = q_h.T.astype(BF16)
    put_values_t(vmat.T, vbT_ref, B_VDIM)

    z = jax.nn.gelu(h[:, P_U:P_U + 2 * C_WIDTH])
    u = z[:, :C_WIDTH]
    v = _layer_norm(z[:, C_WIDTH:], lng_ref[...], lnb_ref[...])
    group = lax.broadcasted_iota(jnp.int32, (CHUNK, C_WIDTH), 1) // C_GROUP_DIM
    for c in range(tm // CHUNK):
        rows = slice(c * CHUNK, (c + 1) * CHUNK)
        vc = v[rows]
        stack = jnp.concatenate([jnp.where(group == g, vc, 0.0) for g in range(C_GROUPS)], axis=0).astype(BF16)
        mixed = _dot(ws_ref[...], stack) + bs_ref[...]
        yc = u[rows] * mixed
        yc_ref[rows, :] = (_rms_scale(yc) * mnc_ref[...]).astype(BF16)


def _inproj(x2, pw, tabs, s_len, tm):
    t_len = x2.shape[0]
    nt = t_len // tm
    ns = s_len // tm
    row = lambda i: (i, 0)
    col = lambda i: (0, i)
    fixed = lambda i: (0, 0)
    pos = lambda i: (i % ns, 0)
    full = lambda a: pl.BlockSpec(a.shape, fixed)
    tab = pl.BlockSpec((tm, LANES), pos)
    in_specs = [pl.BlockSpec((tm, D_MODEL), row), full(pw['w_in']), full(pw['q_norm']), full(pw['w_q']),
                full(pw['kv_norm']), full(pw['w_k']), full(pw['w_v']), tab, tab, tab, tab,
                full(pw['sgu_g']), full(pw['sgu_b']), full(pw['ws_cat']), full(pw['bs_tab']), full(pw['mn_c'])]
    out_shape = [jax.ShapeDtypeStruct((A_HEADS * HEAD_PAD, t_len), BF16),
                 jax.ShapeDtypeStruct((t_len, A_WIDTH), BF16),
                 jax.ShapeDtypeStruct((A_HEADS * VT_ROWS, t_len), BF16),
                 jax.ShapeDtypeStruct((B_HEADS * HEAD_PAD, t_len), BF16),
                 jax.ShapeDtypeStruct((t_len, B_HEADS * HEAD_PAD), BF16),
                 jax.ShapeDtypeStruct((B_HEADS * VT_ROWS, t_len), BF16),
                 jax.ShapeDtypeStruct((t_len, C_WIDTH), BF16)]
    out_specs = [pl.BlockSpec((A_HEADS * HEAD_PAD, tm), col),
                 pl.BlockSpec((tm, A_WIDTH), row),
                 pl.BlockSpec((A_HEADS * VT_ROWS, tm), col),
                 pl.BlockSpec((B_HEADS * HEAD_PAD, tm), col),
                 pl.BlockSpec((tm, B_HEADS * HEAD_PAD), row),
                 pl.BlockSpec((B_HEADS * VT_ROWS, tm), col),
                 pl.BlockSpec((tm, C_WIDTH), row)]
    return pl.pallas_call(
        _inproj_kernel, grid=(nt,), in_specs=in_specs, out_specs=out_specs, out_shape=out_shape,
        compiler_params=pltpu.CompilerParams(dimension_semantics=("arbitrary",), vmem_limit_bytes=VMEM_LIMIT),
        name="inproj",
    )(x2, pw['w_in'], pw['q_norm'], pw['w_q'], pw['kv_norm'], pw['w_k'], pw['w_v'],
      tabs['cq'], tabs['sq'], tabs['ck'], tabs['sk'],
      pw['sgu_g'], pw['sgu_b'], pw['ws_cat'], pw['bs_tab'], pw['mn_c'])


def _dilated_kernel(qT_ref, k_ref, vT_ref, bias_ref, o_ref, s_ref, *, s_len):
    nblk = qT_ref.shape[1] // TQ_A
    first = pl.program_id(2) * nblk

    def window(i):
        t0 = (first + i) * TQ_A
        w0 = jnp.clip(t0 - HALO, 0, s_len - WIN_A)
        b0 = w0 - t0 + 2 * HALO
        return pl.multiple_of(w0, TQ_A), pl.multiple_of(b0, TQ_A), pl.multiple_of(i * TQ_A, TQ_A)

    def scores(i, slot):
        w0, b0, q0 = window(i)
        qT = jnp.concatenate([qT_ref[hh * HEAD_PAD:(hh + 1) * HEAD_PAD, pl.ds(q0, TQ_A)] for hh in range(2)], axis=1)
        s = _dot(k_ref[pl.ds(w0, WIN_A), :], qT)
        m = []
        for hh in range(2):
            s_h = s[:, hh * TQ_A:(hh + 1) * TQ_A] + bias_ref[hh, pl.ds(b0, WIN_A), :]
            s_ref[slot, :, hh * TQ_A:(hh + 1) * TQ_A] = s_h
            m.append(jnp.max(s_h, axis=0, keepdims=True))
        return tuple(m)

    def finish(i, slot, m):
        w0, _, q0 = window(i)
        for hh in range(2):
            p = jnp.exp2(s_ref[slot, :, hh * TQ_A:(hh + 1) * TQ_A] - m[hh]).astype(BF16)
            acc = _dot(vT_ref[hh * VT_ROWS:(hh + 1) * VT_ROWS, pl.ds(w0, WIN_A)], p)
            o_ref[hh * HEAD_DIM:(hh + 1) * HEAD_DIM, pl.ds(q0, TQ_A)] = acc[:HEAD_DIM] / acc[HEAD_DIM:HEAD_DIM + 1]

    def body(i, m):
        m_next = scores(2 * i + 1, 1)
        finish(2 * i, 0, m)
        m = scores(jnp.minimum(2 * i + 2, nblk - 1), 0)
        finish(2 * i + 1, 1, m_next)
        return m

    lax.fori_loop(0, nblk // 2, body, scores(0, 0))


def _dilated(qaT, ka, vaT, bias, batch, s_len):
    chunk = _tile(s_len, 4096)
    assert (chunk // TQ_A) % 2 == 0
    nc = s_len // chunk
    t_len = batch * s_len
    once = pl.Buffered(1)
    return pl.pallas_call(
        functools.partial(_dilated_kernel, s_len=s_len),
        grid=(batch, A_HEADS // 2, nc),
        in_specs=[pl.BlockSpec((2 * HEAD_PAD, chunk), lambda b, h, c: (h, b * nc + c)),
                  pl.BlockSpec((s_len, LANES), lambda b, h, c: (b, h), pipeline_mode=once),
                  pl.BlockSpec((2 * VT_ROWS, s_len), lambda b, h, c: (h, b), pipeline_mode=once),
                  pl.BlockSpec((2, BIAS_ROWS, TQ_A), lambda b, h, c: (h, 0, 0), pipeline_mode=once)],
        out_specs=pl.BlockSpec((2 * HEAD_DIM, chunk), lambda b, h, c: (h, b * nc + c)),
        out_shape=jax.ShapeDtypeStruct((A_WIDTH, t_len), F32),
        scratch_shapes=[pltpu.VMEM((2, WIN_A, 2 * TQ_A + LANES), F32)],
        compiler_params=pltpu.CompilerParams(dimension_semantics=("arbitrary",) * 3, vmem_limit_bytes=VMEM_LIMIT),
        name="dilated",
    )(qaT, ka, vaT, bias)


def _mla_kernel(qT_ref, k_ref, vT_ref, o_ref, acc_ref, s_ref, *, tk, nk):
    qT = qT_ref[...]
    tq = qT.shape[1]
    acc_ref[...] = jnp.zeros_like(acc_ref)

    def scores(j, slot):
        off = pl.multiple_of(j * tk, tk)
        s = _dot(k_ref[pl.ds(off, tk), :], qT)
        s_ref[slot, :, :tq] = s
        return jnp.max(s, axis=0, keepdims=True)

    def step(j, slot, m_prev, m_tile):
        m_tile_next = scores(jnp.minimum(j + 1, nk - 1), 1 - slot)
        off = pl.multiple_of(j * tk, tk)
        m_new = jnp.maximum(m_prev, m_tile)
        alpha = jnp.exp2(m_prev - m_new)
        p = jnp.exp2(s_ref[slot, :, :tq] - m_new).astype(BF16)
        acc_ref[...] = acc_ref[...] * alpha + _dot(vT_ref[:, pl.ds(off, tk)], p)
        return m_new, m_tile_next

    def body(i, carry):
        for u in range(MLA_UNROLL):
            carry = step(MLA_UNROLL * i + u, u % 2, *carry)
        return carry

    m_tile0 = scores(0, 0)
    lax.fori_loop(0, nk // MLA_UNROLL, body, (jnp.full((1, tq), NEG, F32), m_tile0))
    acc = acc_ref[...]
    o_ref[...] = acc[:B_VDIM] / acc[B_VDIM:B_VDIM + 1]


def _mla(qbT, kb, vbT, batch, s_len, tq, tk):
    nq = s_len // tq
    nk = s_len // tk
    assert nk % MLA_UNROLL == 0
    t_len = batch * s_len
    return pl.pallas_call(
        functools.partial(_mla_kernel, tk=tk, nk=nk),
        grid=(batch, B_HEADS, nq),
        in_specs=[pl.BlockSpec((HEAD_PAD, tq), lambda b, h, q: (h, b * nq + q)),
                  pl.BlockSpec((s_len, HEAD_PAD), lambda b, h, q: (b, h)),
                  pl.BlockSpec((VT_ROWS, s_len), lambda b, h, q: (h, b))],
        out_specs=pl.BlockSpec((B_VDIM, tq), lambda b, h, q: (h, b * nq + q)),
        out_shape=jax.ShapeDtypeStruct((B_WIDTH, t_len), F32),
        scratch_shapes=[pltpu.VMEM((VT_ROWS, tq), F32), pltpu.VMEM((2, tk, tq + LANES), F32)],
        compiler_params=pltpu.CompilerParams(dimension_semantics=("arbitrary",) * 3, vmem_limit_bytes=VMEM_LIMIT),
        name="mla",
    )(qbT, kb, vbT)


def _merge_kernel(oaT_ref, obT_ref, yc_ref, x_ref, wa_ref, wb_ref, wc_ref, mna_ref, mnb_ref, g_ref, b_ref, o_ref):
    ya = (_rms_scale(oaT_ref[...].T) * mna_ref[...]).astype(BF16)
    yb = (_rms_scale(obT_ref[...].T) * mnb_ref[...]).astype(BF16)
    y = _dot(ya, wa_ref[...]) + _dot(yb, wb_ref[...]) + _dot(yc_ref[...], wc_ref[...])
    o_ref[...] = _layer_norm(ALPHA * x_ref[...] + y, g_ref[...], b_ref[...])


def _merge(oaT, obT, yc, x2, pw, tm):
    t_len = x2.shape[0]
    row = lambda i: (i, 0)
    col = lambda i: (0, i)
    full = lambda a: pl.BlockSpec(a.shape, lambda i: (0, 0))
    return pl.pallas_call(
        _merge_kernel, grid=(t_len // tm,),
        in_specs=[pl.BlockSpec((A_WIDTH, tm), col), pl.BlockSpec((B_WIDTH, tm), col),
                  pl.BlockSpec((tm, C_WIDTH), row), pl.BlockSpec((tm, D_MODEL), row),
                  full(pw['wo_a']), full(pw['wo_b']), full(pw['wo_c']), full(pw['mn_a']), full(pw['mn_b']),
                  full(pw['ln1_g']), full(pw['ln1_b'])],
        out_specs=pl.BlockSpec((tm, D_MODEL), row),
        out_shape=jax.ShapeDtypeStruct((t_len, D_MODEL), F32),
        compiler_params=pltpu.CompilerParams(dimension_semantics=("arbitrary",), vmem_limit_bytes=VMEM_LIMIT),
        name="merge",
    )(oaT, obT, yc, x2, pw['wo_a'], pw['wo_b'], pw['wo_c'], pw['mn_a'], pw['mn_b'], pw['ln1_g'], pw['ln1_b'])


def _ffn_kernel(x_ref, w1_ref, b1_ref, w2_ref, b2_ref, g_ref, b_ref, o_ref, acc_ref):
    j = pl.program_id(1)
    hid = _dot(x_ref[...].astype(BF16), w1_ref[...]) + b1_ref[...]
    hid = jnp.square(jnp.maximum(hid, 0.0)).astype(BF16)
    part = _dot(hid, w2_ref[...])

    @pl.when(j == 0)
    def _():
        acc_ref[...] = part

    @pl.when(j > 0)
    def _():
        acc_ref[...] += part

    @pl.when(j == pl.num_programs(1) - 1)
    def _():
        o_ref[...] = _layer_norm(ALPHA * x_ref[...] + acc_ref[...] + b2_ref[...], g_ref[...], b_ref[...])


def _ffn(x2, pw, tm, tf):
    t_len = x2.shape[0]
    fixed = lambda i, j: (0, 0)
    return pl.pallas_call(
        _ffn_kernel, grid=(t_len // tm, D_FF // tf),
        in_specs=[pl.BlockSpec((tm, D_MODEL), lambda i, j: (i, 0)),
                  pl.BlockSpec((D_MODEL, tf), lambda i, j: (0, j)),
                  pl.BlockSpec((1, tf), lambda i, j: (0, j)),
                  pl.BlockSpec((tf, D_MODEL), lambda i, j: (j, 0)),
                  pl.BlockSpec((1, D_MODEL), fixed), pl.BlockSpec((1, D_MODEL), fixed),
                  pl.BlockSpec((1, D_MODEL), fixed)],
        out_specs=pl.BlockSpec((tm, D_MODEL), lambda i, j: (i, 0)),
        out_shape=jax.ShapeDtypeStruct((t_len, D_MODEL), F32),
        scratch_shapes=[pltpu.VMEM((tm, D_MODEL), F32)],
        compiler_params=pltpu.CompilerParams(dimension_semantics=("arbitrary", "arbitrary"),
                                             vmem_limit_bytes=VMEM_LIMIT),
        name="ffn",
    )(x2, pw['w_ff1'], pw['b_ff1'], pw['w_ff2'], pw['b_ff2'], pw['ln2_g'], pw['ln2_b'])


def _swap_halves(w):
    half = w.shape[-1] // 2
    return jnp.concatenate([w[..., half:], w[..., :half]], axis=-1)


def _pack_layer(l, w_in, q_norm, w_q_up, kv_norm, w_kv_up, sgu_ln_g, sgu_ln_b, sgu_w, sgu_b, mix_norm,
                w_out, ln1_g, ln1_b, w_ff1, b_ff1, w_ff2, b_ff2, ln2_g, ln2_b):
    w = w_in[l]
    zc = lambda n: jnp.zeros((D_MODEL, n), F32)
    off_cq = 3 * A_WIDTH
    off_ckv = off_cq + Q_RANK
    off_kpe = off_ckv + KV_RANK
    off_u = off_kpe + B_ROPE
    kpe = w[:, off_kpe:off_u]
    w_packed = jnp.concatenate([w[:, :off_cq], w[:, off_ckv:off_kpe], w[:, off_cq:off_ckv], zc(64),
                                zc(64), kpe, _swap_halves(kpe), w[:, off_u:]], axis=1)
    wq = w_q_up[l].reshape(Q_RANK, B_HEADS, B_NOPE + B_ROPE)
    wq = jnp.concatenate([wq, _swap_halves(wq[..., B_NOPE:])], axis=-1).reshape(Q_RANK, B_HEADS * HEAD_PAD)
    wkv = w_kv_up[l].reshape(KV_RANK, B_HEADS, B_NOPE + B_VDIM)
    wk = jnp.concatenate([wkv[..., :B_NOPE], jnp.zeros((KV_RANK, B_HEADS, HEAD_PAD - B_NOPE), F32)], axis=-1)
    row = lambda a: a.reshape(1, -1)
    mn = mix_norm[l]
    return {
        'w_in': w_packed.astype(BF16),
        'q_norm': row(q_norm[l]), 'w_q': wq.astype(BF16),
        'kv_norm': row(kv_norm[l]), 'w_k': wk.reshape(KV_RANK, B_HEADS * HEAD_PAD).astype(BF16),
        'w_v': wkv[..., B_NOPE:].reshape(KV_RANK, B_WIDTH).astype(BF16),
        'sgu_g': row(sgu_ln_g[l]), 'sgu_b': row(sgu_ln_b[l]),
        'ws_cat': jnp.concatenate([sgu_w[l][g] for g in range(C_GROUPS)], axis=1).astype(BF16),
        'bs_tab': jnp.repeat(sgu_b[l].T, C_GROUP_DIM, axis=1),
        'mn_a': row(mn[:A_WIDTH]), 'mn_b': row(mn[A_WIDTH:A_WIDTH + B_WIDTH]), 'mn_c': row(mn[A_WIDTH + B_WIDTH:]),
        'wo_a': w_out[l][:A_WIDTH].astype(BF16), 'wo_b': w_out[l][A_WIDTH:A_WIDTH + B_WIDTH].astype(BF16),
        'wo_c': w_out[l][A_WIDTH + B_WIDTH:].astype(BF16),
        'ln1_g': row(ln1_g[l]), 'ln1_b': row(ln1_b[l]),
        'w_ff1': w_ff1[l].astype(BF16), 'b_ff1': row(b_ff1[l]), 'w_ff2': w_ff2[l].astype(BF16),
        'b_ff2': row(b_ff2[l]), 'ln2_g': row(ln2_g[l]), 'ln2_b': row(ln2_b[l]),
    }


def _rope_tables(s_len):
    inv_freq = ROPE_THETA ** (-jnp.arange(0, B_ROPE, 2, dtype=F32) / B_ROPE)
    ang = jnp.arange(s_len, dtype=F32)[:, None] * inv_freq[None, :]
    cos, sin = jnp.cos(ang), jnp.sin(ang)
    ones = jnp.ones((s_len, B_NOPE), F32)
    zeros = lambda n: jnp.zeros((s_len, n), F32)
    pad = HEAD_PAD - B_NOPE - B_ROPE
    c_tab = jnp.concatenate([cos, cos, zeros(pad)], axis=1)
    s_tab = jnp.concatenate([-sin, sin, zeros(pad)], axis=1)
    q_scale = (B_NOPE + B_ROPE) ** -0.5 * LOG2E
    return {'cq': q_scale * jnp.concatenate([ones, c_tab], axis=1),
            'sq': q_scale * jnp.concatenate([zeros(B_NOPE), s_tab], axis=1),
            'ck': jnp.concatenate([zeros(B_NOPE), c_tab], axis=1),
            'sk': jnp.concatenate([zeros(B_NOPE), s_tab], axis=1)}


def _dilated_bias():
    e = jnp.arange(BIAS_ROWS)[:, None]
    ql = jnp.arange(TQ_A)[None, :]
    dist = jnp.abs(e - 2 * HALO - ql)
    cnt = sum(((dist % d == 0) & (dist <= (w // (2 * d)) * d)).astype(F32) for w, d in DILATED_PATTERNS)
    base = jnp.where(cnt > 0, jnp.log2(jnp.maximum(cnt, 1.0)), NEG)
    slopes = jnp.asarray(2.0 ** (-8.0 * np.arange(1, A_HEADS + 1) / A_HEADS), dtype=F32)
    return base[None] - (slopes * LOG2E)[:, None, None] * dist.astype(F32)[None]


def _tile(n, pref):
    t = min(pref, n)
    assert n % t == 0
    return t


def kernel(x, w_in, q_norm, w_q_up, kv_norm, w_kv_up, sgu_ln_g, sgu_ln_b, sgu_w, sgu_b, mix_norm, w_out,
           ln1_g, ln1_b, w_ff1, b_ff1, w_ff2, b_ff2, ln2_g, ln2_b):
    batch, s_len, _ = x.shape
    assert s_len >= WIN_A and s_len % TQ_A == 0
    t_len = batch * s_len
    tabs = _rope_tables(s_len)
    bias = _dilated_bias()
    x2 = x.reshape(t_len, D_MODEL)
    for l in range(DEPTH):
        pw = _pack_layer(l, w_in, q_norm, w_q_up, kv_norm, w_kv_up, sgu_ln_g, sgu_ln_b, sgu_w, sgu_b, mix_norm,
                         w_out, ln1_g, ln1_b, w_ff1, b_ff1, w_ff2, b_ff2, ln2_g, ln2_b)
        qaT, ka, vaT, qbT, kb, vbT, yc = _inproj(x2, pw, tabs, s_len, _tile(s_len, 512))
        oaT = _dilated(qaT, ka, vaT, bias, batch, s_len)
        obT = _mla(qbT, kb, vbT, batch, s_len, _tile(s_len, 512), _tile(s_len, 512))
        x1 = _merge(oaT, obT, yc, x2, pw, _tile(t_len, 512))
        x2 = _ffn(x1, pw, _tile(t_len, 1024), 1024)
    return x2.reshape(batch, s_len, D_MODEL)
```

```python
import functools
import math

import numpy as np
import jax
import jax.numpy as jnp
from jax import lax
from jax.experimental import pallas as pl
from jax.experimental.pallas import tpu as pltpu

BF16 = jnp.bfloat16
F32 = jnp.float32

D_MODEL = 1024
DEPTH = 2
HEAD_DIM = 64
A_HEADS = 6
A_WIDTH = A_HEADS * HEAD_DIM
DILATED_PATTERNS = ((128, 1), (512, 4), (2048, 16))
B_HEADS = 6
B_NOPE = 64
B_ROPE = 32
B_VDIM = 64
Q_RANK = 192
KV_RANK = 128
B_WIDTH = B_HEADS * B_VDIM
ROPE_THETA = 10000.0
C_GROUPS = 4
C_GROUP_DIM = 64
C_WIDTH = C_GROUPS * C_GROUP_DIM
CHUNK = 128
D_FF = 4 * D_MODEL
ALPHA = (2 * DEPTH) ** 0.25
EPS = 1e-5
NEG = -1e30
LOG2E = math.log2(math.e)

LANES = 128
BF16_SUBLANES = 16
VT_ROWS = B_VDIM + BF16_SUBLANES
HEAD_PAD = LANES

P_QA, P_KA, P_VA = 0, A_WIDTH, 2 * A_WIDTH
P_CKV = 3 * A_WIDTH
P_CQ = P_CKV + KV_RANK
P_KPE = P_CQ + 256
P_U = P_KPE + LANES
P_V = P_U + C_WIDTH
P_PACKED = P_V + C_WIDTH

HALO = max((w // (2 * d)) * d for w, d in DILATED_PATTERNS)
TQ_A = 256
WIN_A = TQ_A + 2 * HALO
BIAS_ROWS = WIN_A + 2 * HALO

VMEM_LIMIT = 48 * 1024 * 1024
MLA_UNROLL = 8
MERGE_SUB = 256
INPROJ_SUB = 256
FFN_SUB = 256


def _rms_scale(x):
    return x * lax.rsqrt(jnp.mean(x * x, axis=-1, keepdims=True) + EPS)


def _layer_norm(x, g, b):
    mu = jnp.mean(x, axis=-1, keepdims=True)
    xc = x - mu
    var = jnp.mean(xc * xc, axis=-1, keepdims=True)
    return xc * lax.rsqrt(var + EPS) * g + b


def _dot(a, b):
    return jnp.dot(a, b, preferred_element_type=F32)


def _inproj_kernel(x_ref, w_ref, qn_ref, wq_ref, kvn_ref, wk_ref, wv_ref, cq_ref, sq_ref, ck_ref, sk_ref,
                   lng_ref, lnb_ref, ws_ref, bs_ref, mnc_ref,
                   qaT_ref, ka_ref, vaT_ref, qbT_ref, kb_ref, vbT_ref, yc_ref):
    sub = INPROJ_SUB
    ones_tile = (lax.broadcasted_iota(jnp.int32, (BF16_SUBLANES, sub), 0) == 0).astype(BF16)
    zeros_half = jnp.zeros((HEAD_DIM, sub), BF16)
    group = lax.broadcasted_iota(jnp.int32, (CHUNK, C_WIDTH), 1) // C_GROUP_DIM

    def put_values_t(vT, out_ref, width, rows):
        for hd in range(A_HEADS):
            out_ref[hd * VT_ROWS:hd * VT_ROWS + width, rows] = vT[hd * width:(hd + 1) * width].astype(BF16)
            out_ref[hd * VT_ROWS + width:(hd + 1) * VT_ROWS, rows] = ones_tile

    for r in range(0, x_ref.shape[0], sub):
        rows = slice(r, r + sub)
        h = _dot(x_ref[rows, :].astype(BF16), w_ref[...])

        qaT = (h[:, P_QA:P_QA + A_WIDTH] * (HEAD_DIM ** -0.5 * LOG2E)).T.astype(BF16)
        for hd in range(A_HEADS):
            q_h = qaT[hd * HEAD_DIM:(hd + 1) * HEAD_DIM]
            lo, hi = (q_h, zeros_half) if hd % 2 == 0 else (zeros_half, q_h)
            qaT_ref[hd * HEAD_PAD:hd * HEAD_PAD + HEAD_DIM, rows] = lo
            qaT_ref[hd * HEAD_PAD + HEAD_DIM:(hd + 1) * HEAD_PAD, rows] = hi
        ka_ref[rows, :] = h[:, P_KA:P_KA + A_WIDTH].astype(BF16)
        put_values_t(h[:, P_VA:P_VA + A_WIDTH].T, vaT_ref, HEAD_DIM, rows)

        c_kv = (_rms_scale(h[:, P_CKV:P_CKV + KV_RANK]) * kvn_ref[...]).astype(BF16)
        kmat = _dot(c_kv, wk_ref[...])
        vmat = _dot(c_kv, wv_ref[...])
        g_k = h[:, P_KPE:P_KPE + LANES]
        kpe = g_k * ck_ref[rows, :] + pltpu.roll(g_k, 96, 1) * sk_ref[rows, :]
        c_q = (_rms_scale(h[:, P_CQ:P_CQ + Q_RANK]) * qn_ref[...]).astype(BF16)
        qmat = _dot(c_q, wq_ref[...])
        cq_t, sq_t = cq_ref[rows, :], sq_ref[rows, :]
        for hd in range(B_HEADS):
            sl = slice(hd * HEAD_PAD, (hd + 1) * HEAD_PAD)
            kb_ref[rows, sl] = (kmat[:, sl] + kpe).astype(BF16)
            g_q = qmat[:, sl]
            q_h = g_q * cq_t + pltpu.roll(g_q, 96, 1) * sq_t
            qbT_ref[sl, rows] = q_h.T.astype(BF16)
        put_values_t(vmat.T, vbT_ref, B_VDIM, rows)

        z = jax.nn.gelu(h[:, P_U:P_U + 2 * C_WIDTH])
        u = z[:, :C_WIDTH]
        v = _layer_norm(z[:, C_WIDTH:], lng_ref[...], lnb_ref[...])
        for c in range(sub // CHUNK):
            crows = slice(c * CHUNK, (c + 1) * CHUNK)
            vc = v[crows]
            stack = jnp.concatenate([jnp.where(group == g, vc, 0.0) for g in range(C_GROUPS)], axis=0).astype(BF16)
            mixed = _dot(ws_ref[...], stack) + bs_ref[...]
            yc = u[crows] * mixed
            yc_ref[r + c * CHUNK:r + (c + 1) * CHUNK, :] = (_rms_scale(yc) * mnc_ref[...]).astype(BF16)


def _inproj(x2, pw, tabs, s_len, tm):
    t_len = x2.shape[0]
    nt = t_len // tm
    ns = s_len // tm
    row = lambda i: (i, 0)
    col = lambda i: (0, i)
    fixed = lambda i: (0, 0)
    pos = lambda i: (i % ns, 0)
    full = lambda a: pl.BlockSpec(a.shape, fixed)
    tab = pl.BlockSpec((tm, LANES), pos)
    in_specs = [pl.BlockSpec((tm, D_MODEL), row), full(pw['w_in']), full(pw['q_norm']), full(pw['w_q']),
                full(pw['kv_norm']), full(pw['w_k']), full(pw['w_v']), tab, tab, tab, tab,
                full(pw['sgu_g']), full(pw['sgu_b']), full(pw['ws_cat']), full(pw['bs_tab']), full(pw['mn_c'])]
    out_shape = [jax.ShapeDtypeStruct((A_HEADS * HEAD_PAD, t_len), BF16),
                 jax.ShapeDtypeStruct((t_len, A_WIDTH), BF16),
                 jax.ShapeDtypeStruct((A_HEADS * VT_ROWS, t_len), BF16),
                 jax.ShapeDtypeStruct((B_HEADS * HEAD_PAD, t_len), BF16),
                 jax.ShapeDtypeStruct((t_len, B_HEADS * HEAD_PAD), BF16),
                 jax.ShapeDtypeStruct((B_HEADS * VT_ROWS, t_len), BF16),
                 jax.ShapeDtypeStruct((t_len, C_WIDTH), BF16)]
    out_specs = [pl.BlockSpec((A_HEADS * HEAD_PAD, tm), col),
                 pl.BlockSpec((tm, A_WIDTH), row),
                 pl.BlockSpec((A_HEADS * VT_ROWS, tm), col),
                 pl.BlockSpec((B_HEADS * HEAD_PAD, tm), col),
                 pl.BlockSpec((tm, B_HEADS * HEAD_PAD), row),
                 pl.BlockSpec((B_HEADS * VT_ROWS, tm), col),
                 pl.BlockSpec((tm, C_WIDTH), row)]
    return pl.pallas_call(
        _inproj_kernel, grid=(nt,), in_specs=in_specs, out_specs=out_specs, out_shape=out_shape,
        compiler_params=pltpu.CompilerParams(dimension_semantics=("arbitrary",), vmem_limit_bytes=VMEM_LIMIT),
        name="inproj",
    )(x2, pw['w_in'], pw['q_norm'], pw['w_q'], pw['kv_norm'], pw['w_k'], pw['w_v'],
      tabs['cq'], tabs['sq'], tabs['ck'], tabs['sk'],
      pw['sgu_g'], pw['sgu_b'], pw['ws_cat'], pw['bs_tab'], pw['mn_c'])


def _dilated_kernel(qT_ref, k_ref, vT_ref, bias_ref, o_ref, s_ref, *, s_len):
    nblk = qT_ref.shape[1] // TQ_A
    first = pl.program_id(2) * nblk

    def window(i):
        t0 = (first + i) * TQ_A
        w0 = jnp.clip(t0 - HALO, 0, s_len - WIN_A)
        b0 = w0 - t0 + 2 * HALO
        return pl.multiple_of(w0, TQ_A), pl.multiple_of(b0, TQ_A), pl.multiple_of(i * TQ_A, TQ_A)

    def scores(i, slot):
        w0, b0, q0 = window(i)
        qT = jnp.concatenate([qT_ref[hh * HEAD_PAD:(hh + 1) * HEAD_PAD, pl.ds(q0, TQ_A)] for hh in range(2)], axis=1)
        s = _dot(k_ref[pl.ds(w0, WIN_A), :], qT)
        m = []
        for hh in range(2):
            s_h = s[:, hh * TQ_A:(hh + 1) * TQ_A] + bias_ref[hh, pl.ds(b0, WIN_A), :]
            s_ref[slot, :, hh * TQ_A:(hh + 1) * TQ_A] = s_h
            m.append(jnp.max(s_h, axis=0, keepdims=True))
        return tuple(m)

    def finish(i, slot, m):
        w0, _, q0 = window(i)
        for hh in range(2):
            p = jnp.exp2(s_ref[slot, :, hh * TQ_A:(hh + 1) * TQ_A] - m[hh]).astype(BF16)
            acc = _dot(vT_ref[hh * VT_ROWS:(hh + 1) * VT_ROWS, pl.ds(w0, WIN_A)], p)
            o_ref[hh * HEAD_DIM:(hh + 1) * HEAD_DIM, pl.ds(q0, TQ_A)] = acc[:HEAD_DIM] / acc[HEAD_DIM:HEAD_DIM + 1]

    def body(i, m):
        m_next = scores(2 * i + 1, 1)
        finish(2 * i, 0, m)
        m = scores(jnp.minimum(2 * i + 2, nblk - 1), 0)
        finish(2 * i + 1, 1, m_next)
        return m

    lax.fori_loop(0, nblk // 2, body, scores(0, 0))


def _dilated(qaT, ka, vaT, bias, batch, s_len):
    chunk = _tile(s_len, 4096)
    assert (chunk // TQ_A) % 2 == 0
    nc = s_len // chunk
    t_len = batch * s_len
    once = pl.Buffered(1)
    return pl.pallas_call(
        functools.partial(_dilated_kernel, s_len=s_len),
        grid=(batch, A_HEADS // 2, nc),
        in_specs=[pl.BlockSpec((2 * HEAD_PAD, chunk), lambda b, h, c: (h, b * nc + c)),
                  pl.BlockSpec((s_len, LANES), lambda b, h, c: (b, h), pipeline_mode=once),
                  pl.BlockSpec((2 * VT_ROWS, s_len), lambda b, h, c: (h, b), pipeline_mode=once),
                  pl.BlockSpec((2, BIAS_ROWS, TQ_A), lambda b, h, c: (h, 0, 0), pipeline_mode=once)],
        out_specs=pl.BlockSpec((2 * HEAD_DIM, chunk), lambda b, h, c: (h, b * nc + c)),
        out_shape=jax.ShapeDtypeStruct((A_WIDTH, t_len), F32),
        scratch_shapes=[pltpu.VMEM((2, WIN_A, 2 * TQ_A + LANES), F32)],
        compiler_params=pltpu.CompilerParams(dimension_semantics=("arbitrary",) * 3, vmem_limit_bytes=VMEM_LIMIT),
        name="dilated",
    )(qaT, ka, vaT, bias)


def _mla_kernel(qT_ref, k_ref, vT_ref, o_ref, acc_ref, s_ref, *, tk, nk):
    qT = qT_ref[...]
    tq = qT.shape[1]
    acc_ref[...] = jnp.zeros_like(acc_ref)

    def scores(j, slot):
        off = pl.multiple_of(j * tk, tk)
        s = _dot(k_ref[pl.ds(off, tk), :], qT)
        s_ref[slot, :, :tq] = s
        return jnp.max(s, axis=0, keepdims=True)

    def step(j, slot, m_prev, m_tile):
        m_tile_next = scores(jnp.minimum(j + 1, nk - 1), 1 - slot)
        off = pl.multiple_of(j * tk, tk)
        m_new = jnp.maximum(m_prev, m_tile)
        alpha = jnp.exp2(m_prev - m_new)
        p = jnp.exp2(s_ref[slot, :, :tq] - m_new).astype(BF16)
        acc_ref[...] = acc_ref[...] * alpha + _dot(vT_ref[:, pl.ds(off, tk)], p)
        return m_new, m_tile_next

    def body(i, carry):
        for u in range(MLA_UNROLL):
            carry = step(MLA_UNROLL * i + u, u % 2, *carry)
        return carry

    m_tile0 = scores(0, 0)
    lax.fori_loop(0, nk // MLA_UNROLL, body, (jnp.full((1, tq), NEG, F32), m_tile0))
    acc = acc_ref[...]
    o_ref[...] = acc[:B_VDIM] / acc[B_VDIM:B_VDIM + 1]


def _mla(qbT, kb, vbT, batch, s_len, tq, tk):
    nq = s_len // tq
    nk = s_len // tk
    assert nk % MLA_UNROLL == 0
    t_len = batch * s_len
    return pl.pallas_call(
        functools.partial(_mla_kernel, tk=tk, nk=nk),
        grid=(batch, B_HEADS, nq),
        in_specs=[pl.BlockSpec((HEAD_PAD, tq), lambda b, h, q: (h, b * nq + q)),
                  pl.BlockSpec((s_len, HEAD_PAD), lambda b, h, q: (b, h)),
                  pl.BlockSpec((VT_ROWS, s_len), lambda b, h, q: (h, b))],
        out_specs=pl.BlockSpec((B_VDIM, tq), lambda b, h, q: (h, b * nq + q)),
        out_shape=jax.ShapeDtypeStruct((B_WIDTH, t_len), F32),
        scratch_shapes=[pltpu.VMEM((VT_ROWS, tq), F32), pltpu.VMEM((2, tk, tq + LANES), F32)],
        compiler_params=pltpu.CompilerParams(dimension_semantics=("arbitrary",) * 3, vmem_limit_bytes=VMEM_LIMIT),
        name="mla",
    )(qbT, kb, vbT)


def _merge_kernel(oaT_ref, obT_ref, yc_ref, x_ref, wa_ref, wb_ref, wc_ref, mna_ref, mnb_ref, g_ref, b_ref, o_ref):
    for r in range(0, x_ref.shape[0], MERGE_SUB):
        rows = slice(r, r + MERGE_SUB)
        ya = (_rms_scale(oaT_ref[:, rows].T) * mna_ref[...]).astype(BF16)
        yb = (_rms_scale(obT_ref[:, rows].T) * mnb_ref[...]).astype(BF16)
        y = _dot(ya, wa_ref[...]) + _dot(yb, wb_ref[...]) + _dot(yc_ref[rows, :], wc_ref[...])
        o_ref[rows, :] = _layer_norm(ALPHA * x_ref[rows, :] + y, g_ref[...], b_ref[...])


def _merge(oaT, obT, yc, x2, pw, tm):
    t_len = x2.shape[0]
    row = lambda i: (i, 0)
    col = lambda i: (0, i)
    full = lambda a: pl.BlockSpec(a.shape, lambda i: (0, 0))
    return pl.pallas_call(
        _merge_kernel, grid=(t_len // tm,),
        in_specs=[pl.BlockSpec((A_WIDTH, tm), col), pl.BlockSpec((B_WIDTH, tm), col),
                  pl.BlockSpec((tm, C_WIDTH), row), pl.BlockSpec((tm, D_MODEL), row),
                  full(pw['wo_a']), full(pw['wo_b']), full(pw['wo_c']), full(pw['mn_a']), full(pw['mn_b']),
                  full(pw['ln1_g']), full(pw['ln1_b'])],
        out_specs=pl.BlockSpec((tm, D_MODEL), row),
        out_shape=jax.ShapeDtypeStruct((t_len, D_MODEL), F32),
        compiler_params=pltpu.CompilerParams(dimension_semantics=("arbitrary",), vmem_limit_bytes=VMEM_LIMIT),
        name="merge",
    )(oaT, obT, yc, x2, pw['wo_a'], pw['wo_b'], pw['wo_c'], pw['mn_a'], pw['mn_b'], pw['ln1_g'], pw['ln1_b'])


def _ffn_kernel(x_ref, w1_ref, b1_ref, w2_ref, b2_ref, g_ref, b_ref, o_ref):
    for r in range(0, x_ref.shape[0], FFN_SUB):
        rows = slice(r, r + FFN_SUB)
        x = x_ref[rows, :]
        hid = _dot(x.astype(BF16), w1_ref[...]) + b1_ref[...]
        hid = jnp.square(jnp.maximum(hid, 0.0)).astype(BF16)
        f = _dot(hid, w2_ref[...]) + b2_ref[...]
        o_ref[rows, :] = _layer_norm(ALPHA * x + f, g_ref[...], b_ref[...])


def _ffn(x2, pw, tm):
    t_len = x2.shape[0]
    fixed = lambda i: (0, 0)
    once = pl.Buffered(1)
    return pl.pallas_call(
        _ffn_kernel, grid=(t_len // tm,),
        in_specs=[pl.BlockSpec((tm, D_MODEL), lambda i: (i, 0)),
                  pl.BlockSpec((D_MODEL, D_FF), fixed, pipeline_mode=once),
                  pl.BlockSpec((1, D_FF), fixed),
                  pl.BlockSpec((D_FF, D_MODEL), fixed, pipeline_mode=once),
                  pl.BlockSpec((1, D_MODEL), fixed), pl.BlockSpec((1, D_MODEL), fixed),
                  pl.BlockSpec((1, D_MODEL), fixed)],
        out_specs=pl.BlockSpec((tm, D_MODEL), lambda i: (i, 0)),
        out_shape=jax.ShapeDtypeStruct((t_len, D_MODEL), F32),
        compiler_params=pltpu.CompilerParams(dimension_semantics=("arbitrary",), vmem_limit_bytes=VMEM_LIMIT),
        name="ffn",
    )(x2, pw['w_ff1'], pw['b_ff1'], pw['w_ff2'], pw['b_ff2'], pw['ln2_g'], pw['ln2_b'])


def _swap_halves(w):
    half = w.shape[-1] // 2
    return jnp.concatenate([w[..., half:], w[..., :half]], axis=-1)


def _pack_layer(l, w_in, q_norm, w_q_up, kv_norm, w_kv_up, sgu_ln_g, sgu_ln_b, sgu_w, sgu_b, mix_norm,
                w_out, ln1_g, ln1_b, w_ff1, b_ff1, w_ff2, b_ff2, ln2_g, ln2_b):
    w = w_in[l]
    zc = lambda n: jnp.zeros((D_MODEL, n), F32)
    off_cq = 3 * A_WIDTH
    off_ckv = off_cq + Q_RANK
    off_kpe = off_ckv + KV_RANK
    off_u = off_kpe + B_ROPE
    kpe = w[:, off_kpe:off_u]
    w_packed = jnp.concatenate([w[:, :off_cq], w[:, off_ckv:off_kpe], w[:, off_cq:off_ckv], zc(64),
                                zc(64), kpe, _swap_halves(kpe), w[:, off_u:]], axis=1)
    wq = w_q_up[l].reshape(Q_RANK, B_HEADS, B_NOPE + B_ROPE)
    wq = jnp.concatenate([wq, _swap_halves(wq[..., B_NOPE:])], axis=-1).reshape(Q_RANK, B_HEADS * HEAD_PAD)
    wkv = w_kv_up[l].reshape(KV_RANK, B_HEADS, B_NOPE + B_VDIM)
    wk = jnp.concatenate([wkv[..., :B_NOPE], jnp.zeros((KV_RANK, B_HEADS, HEAD_PAD - B_NOPE), F32)], axis=-1)
    row = lambda a: a.reshape(1, -1)
    mn = mix_norm[l]
    return {
        'w_in': w_packed.astype(BF16),
        'q_norm': row(q_norm[l]), 'w_q': wq.astype(BF16),
        'kv_norm': row(kv_norm[l]), 'w_k': wk.reshape(KV_RANK, B_HEADS * HEAD_PAD).astype(BF16),
        'w_v': wkv[..., B_NOPE:].reshape(KV_RANK, B_WIDTH).astype(BF16),
        'sgu_g': row(sgu_ln_g[l]), 'sgu_b': row(sgu_ln_b[l]),
        'ws_cat': jnp.concatenate([sgu_w[l][g] for g in range(C_GROUPS)], axis=1).astype(BF16),
        'bs_tab': jnp.repeat(sgu_b[l].T, C_GROUP_DIM, axis=1),
        'mn_a': row(mn[:A_WIDTH]), 'mn_b': row(mn[A_WIDTH:A_WIDTH + B_WIDTH]), 'mn_c': row(mn[A_WIDTH + B_WIDTH:]),
        'wo_a': w_out[l][:A_WIDTH].astype(BF16), 'wo_b': w_out[l][A_WIDTH:A_WIDTH + B_WIDTH].astype(BF16),
        'wo_c': w_out[l][A_WIDTH + B_WIDTH:].astype(BF16),
        'ln1_g': row(ln1_g[l]), 'ln1_b': row(ln1_b[l]),
        'w_ff1': w_ff1[l].astype(BF16), 'b_ff1': row(b_ff1[l]), 'w_ff2': w_ff2[l].astype(BF16),
        'b_ff2': row(b_ff2[l]), 'ln2_g': row(ln2_g[l]), 'ln2_b': row(ln2_b[l]),
    }


def _rope_tables(s_len):
    inv_freq = ROPE_THETA ** (-jnp.arange(0, B_ROPE, 2, dtype=F32) / B_ROPE)
    ang = jnp.arange(s_len, dtype=F32)[:, None] * inv_freq[None, :]
    cos, sin = jnp.cos(ang), jnp.sin(ang)
    ones = jnp.ones((s_len, B_NOPE), F32)
    zeros = lambda n: jnp.zeros((s_len, n), F32)
    pad = HEAD_PAD - B_NOPE - B_ROPE
    c_tab = jnp.concatenate([cos, cos, zeros(pad)], axis=1)
    s_tab = jnp.concatenate([-sin, sin, zeros(pad)], axis=1)
    q_scale = (B_NOPE + B_ROPE) ** -0.5 * LOG2E
    return {'cq': q_scale * jnp.concatenate([ones, c_tab], axis=1),
            'sq': q_scale * jnp.concatenate([zeros(B_NOPE), s_tab], axis=1),
            'ck': jnp.concatenate([zeros(B_NOPE), c_tab], axis=1),
            'sk': jnp.concatenate([zeros(B_NOPE), s_tab], axis=1)}


def _dilated_bias():
    e = jnp.arange(BIAS_ROWS)[:, None]
    ql = jnp.arange(TQ_A)[None, :]
    dist = jnp.abs(e - 2 * HALO - ql)
    cnt = sum(((dist % d == 0) & (dist <= (w // (2 * d)) * d)).astype(F32) for w, d in DILATED_PATTERNS)
    base = jnp.where(cnt > 0, jnp.log2(jnp.maximum(cnt, 1.0)), NEG)
    slopes = jnp.asarray(2.0 ** (-8.0 * np.arange(1, A_HEADS + 1) / A_HEADS), dtype=F32)
    return base[None] - (slopes * LOG2E)[:, None, None] * dist.astype(F32)[None]


def _tile(n, pref):
    t = min(pref, n)
    assert n % t == 0
    return t


def kernel(x, w_in, q_norm, w_q_up, kv_norm, w_kv_up, sgu_ln_g, sgu_ln_b, sgu_w, sgu_b, mix_norm, w_out,
           ln1_g, ln1_b, w_ff1, b_ff1, w_ff2, b_ff2, ln2_g, ln2_b):
    batch, s_len, _ = x.shape
    assert s_len >= WIN_A and s_len % TQ_A == 0
    t_len = batch * s_len
    tabs = _rope_tables(s_len)
    bias = _dilated_bias()
    x2 = x.reshape(t_len, D_MODEL)
    for l in range(DEPTH):
        pw = _pack_layer(l, w_in, q_norm, w_q_up, kv_norm, w_kv_up, sgu_ln_g, sgu_ln_b, sgu_w, sgu_b, mix_norm,
                         w_out, ln1_g, ln1_b, w_ff1, b_ff1, w_ff2, b_ff2, ln2_g, ln2_b)
        qaT, ka, vaT, qbT, kb, vbT, yc = _inproj(x2, pw, tabs, s_len, _tile(s_len, 512))
        oaT = _dilated(qaT, ka, vaT, bias, batch, s_len)
        obT = _mla(qbT, kb, vbT, batch, s_len, _tile(s_len, 512), _tile(s_len, 512))
        x1 = _merge(oaT, obT, yc, x2, pw, _tile(t_len, 1024))
        x2 = _ffn(x1, pw, _tile(t_len, 512))
    return x2.reshape(batch, s_len, D_MODEL)
```

```python
import functools
import math

import numpy as np
import jax
import jax.numpy as jnp
from jax import lax
from jax.experimental import pallas as pl
from jax.experimental.pallas import tpu as pltpu

BF16 = jnp.bfloat16
F32 = jnp.float32

D_MODEL = 1024
DEPTH = 2
HEAD_DIM = 64
A_HEADS = 6
A_WIDTH = A_HEADS * HEAD_DIM
DILATED_PATTERNS = ((128, 1), (512, 4), (2048, 16))
B_HEADS = 6
B_NOPE = 64
B_ROPE = 32
B_VDIM = 64
Q_RANK = 192
KV_RANK = 128
B_WIDTH = B_HEADS * B_VDIM
ROPE_THETA = 10000.0
C_GROUPS = 4
C_GROUP_DIM = 64
C_WIDTH = C_GROUPS * C_GROUP_DIM
CHUNK = 128
D_FF = 4 * D_MODEL
ALPHA = (2 * DEPTH) ** 0.25
EPS = 1e-5
NEG = -1e30
LOG2E = math.log2(math.e)

LANES = 128
BF16_SUBLANES = 16
VT_ROWS = B_VDIM + BF16_SUBLANES
HEAD_PAD = LANES

P_QA, P_KA, P_VA = 0, A_WIDTH, 2 * A_WIDTH
P_CKV = 3 * A_WIDTH
P_CQ = P_CKV + KV_RANK
P_KPE = P_CQ + 256
P_U = P_KPE + LANES
P_V = P_U + C_WIDTH
P_PACKED = P_V + C_WIDTH

HALO = max((w // (2 * d)) * d for w, d in DILATED_PATTERNS)
TQ_A = 256
WIN_A = TQ_A + 2 * HALO
BIAS_ROWS = WIN_A + 2 * HALO

VMEM_LIMIT = 48 * 1024 * 1024
MLA_UNROLL = 8
MERGE_SUB = 256
INPROJ_SUB = 256
FFN_SUB = 256


def _rms_scale(x):
    return x * lax.rsqrt(jnp.mean(x * x, axis=-1, keepdims=True) + EPS)


def _layer_norm(x, g, b):
    mu = jnp.mean(x, axis=-1, keepdims=True)
    xc = x - mu
    var = jnp.mean(xc * xc, axis=-1, keepdims=True)
    return xc * lax.rsqrt(var + EPS) * g + b


def _dot(a, b):
    return jnp.dot(a, b, preferred_element_type=F32)


def _inproj_kernel(x_ref, w_ref, qn_ref, wq_ref, kvn_ref, wk_ref, wv_ref, cq_ref, sq_ref, ck_ref, sk_ref,
                   lng_ref, lnb_ref, ws_ref, bs_ref, mnc_ref,
                   qaT_ref, ka_ref, vaT_ref, qbT_ref, kb_ref, vbT_ref, yc_ref):
    sub = INPROJ_SUB
    ones_tile = (lax.broadcasted_iota(jnp.int32, (BF16_SUBLANES, sub), 0) == 0).astype(BF16)
    zeros_half = jnp.zeros((HEAD_DIM, sub), BF16)
    group = lax.broadcasted_iota(jnp.int32, (CHUNK, C_WIDTH), 1) // C_GROUP_DIM

    def put_values_t(vT, out_ref, width, rows):
        for hd in range(A_HEADS):
            out_ref[hd * VT_ROWS:hd * VT_ROWS + width, rows] = vT[hd * width:(hd + 1) * width].astype(BF16)
            out_ref[hd * VT_ROWS + width:(hd + 1) * VT_ROWS, rows] = ones_tile

    for r in range(0, x_ref.shape[0], sub):
        rows = slice(r, r + sub)
        h = _dot(x_ref[rows, :].astype(BF16), w_ref[...])

        qaT = (h[:, P_QA:P_QA + A_WIDTH] * (HEAD_DIM ** -0.5 * LOG2E)).T.astype(BF16)
        for hd in range(A_HEADS):
            q_h = qaT[hd * HEAD_DIM:(hd + 1) * HEAD_DIM]
            lo, hi = (q_h, zeros_half) if hd % 2 == 0 else (zeros_half, q_h)
            qaT_ref[hd * HEAD_PAD:hd * HEAD_PAD + HEAD_DIM, rows] = lo
            qaT_ref[hd * HEAD_PAD + HEAD_DIM:(hd + 1) * HEAD_PAD, rows] = hi
        ka_ref[rows, :] = h[:, P_KA:P_KA + A_WIDTH].astype(BF16)
        put_values_t(h[:, P_VA:P_VA + A_WIDTH].T, vaT_ref, HEAD_DIM, rows)

        c_kv = (_rms_scale(h[:, P_CKV:P_CKV + KV_RANK]) * kvn_ref[...]).astype(BF16)
        kmat = _dot(c_kv, wk_ref[...])
        vmat = _dot(c_kv, wv_ref[...])
        g_k = h[:, P_KPE:P_KPE + LANES]
        kpe = g_k * ck_ref[rows, :] + pltpu.roll(g_k, 96, 1) * sk_ref[rows, :]
        c_q = (_rms_scale(h[:, P_CQ:P_CQ + Q_RANK]) * qn_ref[...]).astype(BF16)
        qmat = _dot(c_q, wq_ref[...])
        cq_t, sq_t = cq_ref[rows, :], sq_ref[rows, :]
        for hd in range(B_HEADS):
            sl = slice(hd * HEAD_PAD, (hd + 1) * HEAD_PAD)
            kb_ref[rows, sl] = (kmat[:, sl] + kpe).astype(BF16)
            g_q = qmat[:, sl]
            q_h = g_q * cq_t + pltpu.roll(g_q, 96, 1) * sq_t
            qbT_ref[sl, rows] = q_h.T.astype(BF16)
        put_values_t(vmat.T, vbT_ref, B_VDIM, rows)

        z = jax.nn.gelu(h[:, P_U:P_U + 2 * C_WIDTH])
        u = z[:, :C_WIDTH]
        v = _layer_norm(z[:, C_WIDTH:], lng_ref[...], lnb_ref[...])
        for c in range(sub // CHUNK):
            crows = slice(c * CHUNK, (c + 1) * CHUNK)
            vc = v[crows]
            stack = jnp.concatenate([jnp.where(group == g, vc, 0.0) for g in range(C_GROUPS)], axis=0).astype(BF16)
            mixed = _dot(ws_ref[...], stack) + bs_ref[...]
            yc = u[crows] * mixed
            yc_ref[r + c * CHUNK:r + (c + 1) * CHUNK, :] = (_rms_scale(yc) * mnc_ref[...]).astype(BF16)


def _inproj(x2, pw, tabs, s_len, tm):
    t_len = x2.shape[0]
    nt = t_len // tm
    ns = s_len // tm
    row = lambda i: (i, 0)
    col = lambda i: (0, i)
    fixed = lambda i: (0, 0)
    pos = lambda i: (i % ns, 0)
    full = lambda a: pl.BlockSpec(a.shape, fixed)
    tab = pl.BlockSpec((tm, LANES), pos)
    in_specs = [pl.BlockSpec((tm, D_MODEL), row), full(pw['w_in']), full(pw['q_norm']), full(pw['w_q']),
                full(pw['kv_norm']), full(pw['w_k']), full(pw['w_v']), tab, tab, tab, tab,
                full(pw['sgu_g']), full(pw['sgu_b']), full(pw['ws_cat']), full(pw['bs_tab']), full(pw['mn_c'])]
    out_shape = [jax.ShapeDtypeStruct((A_HEADS * HEAD_PAD, t_len), BF16),
                 jax.ShapeDtypeStruct((t_len, A_WIDTH), BF16),
                 jax.ShapeDtypeStruct((A_HEADS * VT_ROWS, t_len), BF16),
                 jax.ShapeDtypeStruct((B_HEADS * HEAD_PAD, t_len), BF16),
                 jax.ShapeDtypeStruct((t_len, B_HEADS * HEAD_PAD), BF16),
                 jax.ShapeDtypeStruct((B_HEADS * VT_ROWS, t_len), BF16),
                 jax.ShapeDtypeStruct((t_len, C_WIDTH), BF16)]
    out_specs = [pl.BlockSpec((A_HEADS * HEAD_PAD, tm), col),
                 pl.BlockSpec((tm, A_WIDTH), row),
                 pl.BlockSpec((A_HEADS * VT_ROWS, tm), col),
                 pl.BlockSpec((B_HEADS * HEAD_PAD, tm), col),
                 pl.BlockSpec((tm, B_HEADS * HEAD_PAD), row),
                 pl.BlockSpec((B_HEADS * VT_ROWS, tm), col),
                 pl.BlockSpec((tm, C_WIDTH), row)]
    return pl.pallas_call(
        _inproj_kernel, grid=(nt,), in_specs=in_specs, out_specs=out_specs, out_shape=out_shape,
        compiler_params=pltpu.CompilerParams(dimension_semantics=("arbitrary",), vmem_limit_bytes=VMEM_LIMIT),
        name="inproj",
    )(x2, pw['w_in'], pw['q_norm'], pw['w_q'], pw['kv_norm'], pw['w_k'], pw['w_v'],
      tabs['cq'], tabs['sq'], tabs['ck'], tabs['sk'],
      pw['sgu_g'], pw['sgu_b'], pw['ws_cat'], pw['bs_tab'], pw['mn_c'])


def _dilated_kernel(qT_ref, k_ref, vT_ref, bias_ref, o_ref, s_ref, *, s_len):
    nblk = qT_ref.shape[1] // TQ_A
    first = pl.program_id(2) * nblk

    def window(i):
        t0 = (first + i) * TQ_A
        w0 = jnp.clip(t0 - HALO, 0, s_len - WIN_A)
        b0 = w0 - t0 + 2 * HALO
        return pl.multiple_of(w0, TQ_A), pl.multiple_of(b0, TQ_A), pl.multiple_of(i * TQ_A, TQ_A)

    def scores(i, slot):
        w0, b0, q0 = window(i)
        qT = jnp.concatenate([qT_ref[hh * HEAD_PAD:(hh + 1) * HEAD_PAD, pl.ds(q0, TQ_A)] for hh in range(2)], axis=1)
        s = _dot(k_ref[pl.ds(w0, WIN_A), :], qT)
        m = []
        for hh in range(2):
            s_h = s[:, hh * TQ_A:(hh + 1) * TQ_A] + bias_ref[hh, pl.ds(b0, WIN_A), :]
            s_ref[slot, :, hh * TQ_A:(hh + 1) * TQ_A] = s_h
            m.append(jnp.max(s_h, axis=0, keepdims=True))
        return tuple(m)

    def finish(i, slot, m):
        w0, _, q0 = window(i)
        for hh in range(2):
            p = jnp.exp2(s_ref[slot, :, hh * TQ_A:(hh + 1) * TQ_A] - m[hh]).astype(BF16)
            acc = _dot(vT_ref[hh * VT_ROWS:(hh + 1) * VT_ROWS, pl.ds(w0, WIN_A)], p)
            o_ref[hh * HEAD_DIM:(hh + 1) * HEAD_DIM, pl.ds(q0, TQ_A)] = acc[:HEAD_DIM] / acc[HEAD_DIM:HEAD_DIM + 1]

    def body(i, m):
        m_next = scores(2 * i + 1, 1)
        finish(2 * i, 0, m)
        m = scores(jnp.minimum(2 * i + 2, nblk - 1), 0)
        finish(2 * i + 1, 1, m_next)
        return m

    lax.fori_loop(0, nblk // 2, body, scores(0, 0))


def _dilated(qaT, ka, vaT, bias, batch, s_len):
    chunk = _tile(s_len, 4096)
    assert (chunk // TQ_A) % 2 == 0
    nc = s_len // chunk
    t_len = batch * s_len
    once = pl.Buffered(1)
    return pl.pallas_call(
        functools.partial(_dilated_kernel, s_len=s_len),
        grid=(batch, A_HEADS // 2, nc),
        in_specs=[pl.BlockSpec((2 * HEAD_PAD, chunk), lambda b, h, c: (h, b * nc + c)),
                  pl.BlockSpec((s_len, LANES), lambda b, h, c: (b, h), pipeline_mode=once),
                  pl.BlockSpec((2 * VT_ROWS, s_len), lambda b, h, c: (h, b), pipeline_mode=once),
                  pl.BlockSpec((2, BIAS_ROWS, TQ_A), lambda b, h, c: (h, 0, 0), pipeline_mode=once)],
        out_specs=pl.BlockSpec((2 * HEAD_DIM, chunk), lambda b, h, c: (h, b * nc + c)),
        out_shape=jax.ShapeDtypeStruct((A_WIDTH, t_len), F32),
        scratch_shapes=[pltpu.VMEM((2, WIN_A, 2 * TQ_A + LANES), F32)],
        compiler_params=pltpu.CompilerParams(dimension_semantics=("arbitrary",) * 3, vmem_limit_bytes=VMEM_LIMIT),
        name="dilated",
    )(qaT, ka, vaT, bias)


def _mla_kernel(qT_ref, k_ref, vT_ref, o_ref, acc_ref, s_ref, *, tk, nk):
    qT = qT_ref[...]
    tq = qT.shape[1]
    acc_ref[...] = jnp.zeros_like(acc_ref)

    def scores(j, slot):
        off = pl.multiple_of(j * tk, tk)
        s = _dot(k_ref[pl.ds(off, tk), :], qT)
        s_ref[slot, :, :tq] = s
        return jnp.max(s, axis=0, keepdims=True)

    def step(j, slot, m_prev, m_tile):
        m_tile_next = scores(jnp.minimum(j + 1, nk - 1), 1 - slot)
        off = pl.multiple_of(j * tk, tk)
        m_new = jnp.maximum(m_prev, m_tile)
        alpha = jnp.exp2(m_prev - m_new)
        p = jnp.exp2((s_ref[slot, :, :tq] - m_new).astype(BF16))
        acc_ref[...] = acc_ref[...] * alpha + _dot(vT_ref[:, pl.ds(off, tk)], p)
        return m_new, m_tile_next

    def body(i, carry):
        for u in range(MLA_UNROLL):
            carry = step(MLA_UNROLL * i + u, u % 2, *carry)
        return carry

    m_tile0 = scores(0, 0)
    lax.fori_loop(0, nk // MLA_UNROLL, body, (jnp.full((1, tq), NEG, F32), m_tile0))
    acc = acc_ref[...]
    o_ref[...] = acc[:B_VDIM] / acc[B_VDIM:B_VDIM + 1]


def _mla(qbT, kb, vbT, batch, s_len, tq, tk):
    nq = s_len // tq
    nk = s_len // tk
    assert nk % MLA_UNROLL == 0
    t_len = batch * s_len
    return pl.pallas_call(
        functools.partial(_mla_kernel, tk=tk, nk=nk),
        grid=(batch, B_HEADS, nq),
        in_specs=[pl.BlockSpec((HEAD_PAD, tq), lambda b, h, q: (h, b * nq + q)),
                  pl.BlockSpec((s_len, HEAD_PAD), lambda b, h, q: (b, h)),
                  pl.BlockSpec((VT_ROWS, s_len), lambda b, h, q: (h, b))],
        out_specs=pl.BlockSpec((B_VDIM, tq), lambda b, h, q: (h, b * nq + q)),
        out_shape=jax.ShapeDtypeStruct((B_WIDTH, t_len), F32),
        scratch_shapes=[pltpu.VMEM((VT_ROWS, tq), F32), pltpu.VMEM((2, tk, tq + LANES), F32)],
        compiler_params=pltpu.CompilerParams(dimension_semantics=("arbitrary",) * 3, vmem_limit_bytes=VMEM_LIMIT),
        name="mla",
    )(qbT, kb, vbT)


def _merge_kernel(oaT_ref, obT_ref, yc_ref, x_ref, wa_ref, wb_ref, wc_ref, mna_ref, mnb_ref, g_ref, b_ref, o_ref):
    for r in range(0, x_ref.shape[0], MERGE_SUB):
        rows = slice(r, r + MERGE_SUB)
        ya = (_rms_scale(oaT_ref[:, rows].T) * mna_ref[...]).astype(BF16)
        yb = (_rms_scale(obT_ref[:, rows].T) * mnb_ref[...]).astype(BF16)
        y = _dot(ya, wa_ref[...]) + _dot(yb, wb_ref[...]) + _dot(yc_ref[rows, :], wc_ref[...])
        o_ref[rows, :] = _layer_norm(ALPHA * x_ref[rows, :] + y, g_ref[...], b_ref[...])


def _merge(oaT, obT, yc, x2, pw, tm):
    t_len = x2.shape[0]
    row = lambda i: (i, 0)
    col = lambda i: (0, i)
    full = lambda a: pl.BlockSpec(a.shape, lambda i: (0, 0))
    return pl.pallas_call(
        _merge_kernel, grid=(t_len // tm,),
        in_specs=[pl.BlockSpec((A_WIDTH, tm), col), pl.BlockSpec((B_WIDTH, tm), col),
                  pl.BlockSpec((tm, C_WIDTH), row), pl.BlockSpec((tm, D_MODEL), row),
                  full(pw['wo_a']), full(pw['wo_b']), full(pw['wo_c']), full(pw['mn_a']), full(pw['mn_b']),
                  full(pw['ln1_g']), full(pw['ln1_b'])],
        out_specs=pl.BlockSpec((tm, D_MODEL), row),
        out_shape=jax.ShapeDtypeStruct((t_len, D_MODEL), F32),
        compiler_params=pltpu.CompilerParams(dimension_semantics=("arbitrary",), vmem_limit_bytes=VMEM_LIMIT),
        name="merge",
    )(oaT, obT, yc, x2, pw['wo_a'], pw['wo_b'], pw['wo_c'], pw['mn_a'], pw['mn_b'], pw['ln1_g'], pw['ln1_b'])


def _ffn_kernel(x_ref, w1_ref, b1_ref, w2_ref, b2_ref, g_ref, b_ref, o_ref):
    for r in range(0, x_ref.shape[0], FFN_SUB):
        rows = slice(r, r + FFN_SUB)
        x = x_ref[rows, :]
        hid = _dot(x.astype(BF16), w1_ref[...]) + b1_ref[...]
        hid = jnp.square(jnp.maximum(hid, 0.0)).astype(BF16)
        f = _dot(hid, w2_ref[...]) + b2_ref[...]
        o_ref[rows, :] = _layer_norm(ALPHA * x + f, g_ref[...], b_ref[...])


def _ffn(x2, pw, tm):
    t_len = x2.shape[0]
    fixed = lambda i: (0, 0)
    once = pl.Buffered(1)
    return pl.pallas_call(
        _ffn_kernel, grid=(t_len // tm,),
        in_specs=[pl.BlockSpec((tm, D_MODEL), lambda i: (i, 0)),
                  pl.BlockSpec((D_MODEL, D_FF), fixed, pipeline_mode=once),
                  pl.BlockSpec((1, D_FF), fixed),
                  pl.BlockSpec((D_FF, D_MODEL), fixed, pipeline_mode=once),
                  pl.BlockSpec((1, D_MODEL), fixed), pl.BlockSpec((1, D_MODEL), fixed),
                  pl.BlockSpec((1, D_MODEL), fixed)],
        out_specs=pl.BlockSpec((tm, D_MODEL), lambda i: (i, 0)),
        out_shape=jax.ShapeDtypeStruct((t_len, D_MODEL), F32),
        compiler_params=pltpu.CompilerParams(dimension_semantics=("arbitrary",), vmem_limit_bytes=VMEM_LIMIT),
        name="ffn",
    )(x2, pw['w_ff1'], pw['b_ff1'], pw['w_ff2'], pw['b_ff2'], pw['ln2_g'], pw['ln2_b'])


def _swap_halves(w):
    half = w.shape[-1] // 2
    return jnp.concatenate([w[..., half:], w[..., :half]], axis=-1)


def _pack_layer(l, w_in, q_norm, w_q_up, kv_norm, w_kv_up, sgu_ln_g, sgu_ln_b, sgu_w, sgu_b, mix_norm,
                w_out, ln1_g, ln1_b, w_ff1, b_ff1, w_ff2, b_ff2, ln2_g, ln2_b):
    w = w_in[l]
    zc = lambda n: jnp.zeros((D_MODEL, n), F32)
    off_cq = 3 * A_WIDTH
    off_ckv = off_cq + Q_RANK
    off_kpe = off_ckv + KV_RANK
    off_u = off_kpe + B_ROPE
    kpe = w[:, off_kpe:off_u]
    w_packed = jnp.concatenate([w[:, :off_cq], w[:, off_ckv:off_kpe], w[:, off_cq:off_ckv], zc(64),
                                zc(64), kpe, _swap_halves(kpe), w[:, off_u:]], axis=1)
    wq = w_q_up[l].reshape(Q_RANK, B_HEADS, B_NOPE + B_ROPE)
    wq = jnp.concatenate([wq, _swap_halves(wq[..., B_NOPE:])], axis=-1).reshape(Q_RANK, B_HEADS * HEAD_PAD)
    wkv = w_kv_up[l].reshape(KV_RANK, B_HEADS, B_NOPE + B_VDIM)
    wk = jnp.concatenate([wkv[..., :B_NOPE], jnp.zeros((KV_RANK, B_HEADS, HEAD_PAD - B_NOPE), F32)], axis=-1)
    row = lambda a: a.reshape(1, -1)
    mn = mix_norm[l]
    return {
        'w_in': w_packed.astype(BF16),
        'q_norm': row(q_norm[l]), 'w_q': wq.astype(BF16),
        'kv_norm': row(kv_norm[l]), 'w_k': wk.reshape(KV_RANK, B_HEADS * HEAD_PAD).astype(BF16),
        'w_v': wkv[..., B_NOPE:].reshape(KV_RANK, B_WIDTH).astype(BF16),
        'sgu_g': row(sgu_ln_g[l]), 'sgu_b': row(sgu_ln_b[l]),
        'ws_cat': jnp.concatenate([sgu_w[l][g] for g in range(C_GROUPS)], axis=1).astype(BF16),
        'bs_tab': jnp.repeat(sgu_b[l].T, C_GROUP_DIM, axis=1),
        'mn_a': row(mn[:A_WIDTH]), 'mn_b': row(mn[A_WIDTH:A_WIDTH + B_WIDTH]), 'mn_c': row(mn[A_WIDTH + B_WIDTH:]),
        'wo_a': w_out[l][:A_WIDTH].astype(BF16), 'wo_b': w_out[l][A_WIDTH:A_WIDTH + B_WIDTH].astype(BF16),
        'wo_c': w_out[l][A_WIDTH + B_WIDTH:].astype(BF16),
        'ln1_g': row(ln1_g[l]), 'ln1_b': row(ln1_b[l]),
        'w_ff1': w_ff1[l].astype(BF16), 'b_ff1': row(b_ff1[l]), 'w_ff2': w_ff2[l].astype(BF16),
        'b_ff2': row(b_ff2[l]), 'ln2_g': row(ln2_g[l]), 'ln2_b': row(ln2_b[l]),
    }


def _rope_tables(s_len):
    inv_freq = ROPE_THETA ** (-jnp.arange(0, B_ROPE, 2, dtype=F32) / B_ROPE)
    ang = jnp.arange(s_len, dtype=F32)[:, None] * inv_freq[None, :]
    cos, sin = jnp.cos(ang), jnp.sin(ang)
    ones = jnp.ones((s_len, B_NOPE), F32)
    zeros = lambda n: jnp.zeros((s_len, n), F32)
    pad = HEAD_PAD - B_NOPE - B_ROPE
    c_tab = jnp.concatenate([cos, cos, zeros(pad)], axis=1)
    s_tab = jnp.concatenate([-sin, sin, zeros(pad)], axis=1)
    q_scale = (B_NOPE + B_ROPE) ** -0.5 * LOG2E
    return {'cq': q_scale * jnp.concatenate([ones, c_tab], axis=1),
            'sq': q_scale * jnp.concatenate([zeros(B_NOPE), s_tab], axis=1),
            'ck': jnp.concatenate([zeros(B_NOPE), c_tab], axis=1),
            'sk': jnp.concatenate([zeros(B_NOPE), s_tab], axis=1)}


def _dilated_bias():
    e = jnp.arange(BIAS_ROWS)[:, None]
    ql = jnp.arange(TQ_A)[None, :]
    dist = jnp.abs(e - 2 * HALO - ql)
    cnt = sum(((dist % d == 0) & (dist <= (w // (2 * d)) * d)).astype(F32) for w, d in DILATED_PATTERNS)
    base = jnp.where(cnt > 0, jnp.log2(jnp.maximum(cnt, 1.0)), NEG)
    slopes = jnp.asarray(2.0 ** (-8.0 * np.arange(1, A_HEADS + 1) / A_HEADS), dtype=F32)
    return base[None] - (slopes * LOG2E)[:, None, None] * dist.astype(F32)[None]


def _tile(n, pref):
    t = min(pref, n)
    assert n % t == 0
    return t


def kernel(x, w_in, q_norm, w_q_up, kv_norm, w_kv_up, sgu_ln_g, sgu_ln_b, sgu_w, sgu_b, mix_norm, w_out,
           ln1_g, ln1_b, w_ff1, b_ff1, w_ff2, b_ff2, ln2_g, ln2_b):
    batch, s_len, _ = x.shape
    assert s_len >= WIN_A and s_len % TQ_A == 0
    t_len = batch * s_len
    tabs = _rope_tables(s_len)
    bias = _dilated_bias()
    x2 = x.reshape(t_len, D_MODEL)
    for l in range(DEPTH):
        pw = _pack_layer(l, w_in, q_norm, w_q_up, kv_norm, w_kv_up, sgu_ln_g, sgu_ln_b, sgu_w, sgu_b, mix_norm,
                         w_out, ln1_g, ln1_b, w_ff1, b_ff1, w_ff2, b_ff2, ln2_g, ln2_b)
        qaT, ka, vaT, qbT, kb, vbT, yc = _inproj(x2, pw, tabs, s_len, _tile(s_len, 512))
        oaT = _dilated(qaT, ka, vaT, bias, batch, s_len)
        obT = _mla(qbT, kb, vbT, batch, s_len, _tile(s_len, 512), _tile(s_len, 512))
        x1 = _merge(oaT, obT, yc, x2, pw, _tile(t_len, 1024))
        x2 = _ffn(x1, pw, _tile(t_len, 512))
    return x2.reshape(batch, s_len, D_MODEL)
```

```python
import functools
import math

import numpy as np
import jax
import jax.numpy as jnp
from jax import lax
from jax.experimental import pallas as pl
from jax.experimental.pallas import tpu as pltpu

BF16 = jnp.bfloat16
F32 = jnp.float32

D_MODEL = 1024
DEPTH = 2
HEAD_DIM = 64
A_HEADS = 6
A_WIDTH = A_HEADS * HEAD_DIM
DILATED_PATTERNS = ((128, 1), (512, 4), (2048, 16))
B_HEADS = 6
B_NOPE = 64
B_ROPE = 32
B_VDIM = 64
Q_RANK = 192
KV_RANK = 128
B_WIDTH = B_HEADS * B_VDIM
ROPE_THETA = 10000.0
C_GROUPS = 4
C_GROUP_DIM = 64
C_WIDTH = C_GROUPS * C_GROUP_DIM
CHUNK = 128
D_FF = 4 * D_MODEL
ALPHA = (2 * DEPTH) ** 0.25
EPS = 1e-5
NEG = -1e30
LOG2E = math.log2(math.e)

LANES = 128
BF16_SUBLANES = 16
VT_ROWS = B_VDIM + BF16_SUBLANES
HEAD_PAD = LANES

P_QA, P_KA, P_VA = 0, A_WIDTH, 2 * A_WIDTH
P_CKV = 3 * A_WIDTH
P_CQ = P_CKV + KV_RANK
P_KPE = P_CQ + 256
P_U = P_KPE + LANES
P_V = P_U + C_WIDTH
P_PACKED = P_V + C_WIDTH

HALO = max((w // (2 * d)) * d for w, d in DILATED_PATTERNS)
TQ_A = 256
WIN_A = TQ_A + 2 * HALO
BIAS_ROWS = WIN_A + 2 * HALO

VMEM_LIMIT = 48 * 1024 * 1024
MLA_UNROLL = 16
DIL_UNROLL = 4
MERGE_SUB = 256
INPROJ_SUB = 256
FFN_SUB = 256


def _rms_scale(x):
    return x * lax.rsqrt(jnp.mean(x * x, axis=-1, keepdims=True) + EPS)


def _layer_norm(x, g, b):
    mu = jnp.mean(x, axis=-1, keepdims=True)
    xc = x - mu
    var = jnp.mean(xc * xc, axis=-1, keepdims=True)
    return xc * lax.rsqrt(var + EPS) * g + b


def _dot(a, b):
    return jnp.dot(a, b, preferred_element_type=F32)


def _inproj_kernel(x_ref, w_ref, qn_ref, wq_ref, kvn_ref, wk_ref, wv_ref, cq_ref, sq_ref, ck_ref, sk_ref,
                   lng_ref, lnb_ref, ws_ref, bs_ref, mnc_ref,
                   qaT_ref, ka_ref, vaT_ref, qbT_ref, kb_ref, vbT_ref, yc_ref):
    sub = INPROJ_SUB
    ones_tile = (lax.broadcasted_iota(jnp.int32, (BF16_SUBLANES, sub), 0) == 0).astype(BF16)
    zeros_half = jnp.zeros((HEAD_DIM, sub), BF16)
    group = lax.broadcasted_iota(jnp.int32, (CHUNK, C_WIDTH), 1) // C_GROUP_DIM

    def put_values_t(vT, out_ref, width, rows):
        for hd in range(A_HEADS):
            out_ref[hd * VT_ROWS:hd * VT_ROWS + width, rows] = vT[hd * width:(hd + 1) * width].astype(BF16)
            out_ref[hd * VT_ROWS + width:(hd + 1) * VT_ROWS, rows] = ones_tile

    for r in range(0, x_ref.shape[0], sub):
        rows = slice(r, r + sub)
        h = _dot(x_ref[rows, :].astype(BF16), w_ref[...])

        qaT = (h[:, P_QA:P_QA + A_WIDTH] * (HEAD_DIM ** -0.5 * LOG2E)).T.astype(BF16)
        for hd in range(A_HEADS):
            q_h = qaT[hd * HEAD_DIM:(hd + 1) * HEAD_DIM]
            lo, hi = (q_h, zeros_half) if hd % 2 == 0 else (zeros_half, q_h)
            qaT_ref[hd * HEAD_PAD:hd * HEAD_PAD + HEAD_DIM, rows] = lo
            qaT_ref[hd * HEAD_PAD + HEAD_DIM:(hd + 1) * HEAD_PAD, rows] = hi
        ka_ref[rows, :] = h[:, P_KA:P_KA + A_WIDTH].astype(BF16)
        put_values_t(h[:, P_VA:P_VA + A_WIDTH].T, vaT_ref, HEAD_DIM, rows)

        c_kv = (_rms_scale(h[:, P_CKV:P_CKV + KV_RANK]) * kvn_ref[...]).astype(BF16)
        kmat = _dot(c_kv, wk_ref[...])
        vmat = _dot(c_kv, wv_ref[...])
        g_k = h[:, P_KPE:P_KPE + LANES]
        kpe = g_k * ck_ref[rows, :] + pltpu.roll(g_k, 96, 1) * sk_ref[rows, :]
        c_q = (_rms_scale(h[:, P_CQ:P_CQ + Q_RANK]) * qn_ref[...]).astype(BF16)
        qmat = _dot(c_q, wq_ref[...])
        cq_t, sq_t = cq_ref[rows, :], sq_ref[rows, :]
        for hd in range(B_HEADS):
            sl = slice(hd * HEAD_PAD, (hd + 1) * HEAD_PAD)
            kb_ref[rows, sl] = (kmat[:, sl] + kpe).astype(BF16)
            g_q = qmat[:, sl]
            q_h = g_q * cq_t + pltpu.roll(g_q, 96, 1) * sq_t
            qbT_ref[sl, rows] = q_h.T.astype(BF16)
        put_values_t(vmat.T, vbT_ref, B_VDIM, rows)

        z = jax.nn.gelu(h[:, P_U:P_U + 2 * C_WIDTH])
        u = z[:, :C_WIDTH]
        v = _layer_norm(z[:, C_WIDTH:], lng_ref[...], lnb_ref[...])
        for c in range(sub // CHUNK):
            crows = slice(c * CHUNK, (c + 1) * CHUNK)
            vc = v[crows]
            stack = jnp.concatenate([jnp.where(group == g, vc, 0.0) for g in range(C_GROUPS)], axis=0).astype(BF16)
            mixed = _dot(ws_ref[...], stack) + bs_ref[...]
            yc = u[crows] * mixed
            yc_ref[r + c * CHUNK:r + (c + 1) * CHUNK, :] = (_rms_scale(yc) * mnc_ref[...]).astype(BF16)


def _inproj(x2, pw, tabs, s_len, tm):
    t_len = x2.shape[0]
    nt = t_len // tm
    ns = s_len // tm
    row = lambda i: (i, 0)
    col = lambda i: (0, i)
    fixed = lambda i: (0, 0)
    pos = lambda i: (i % ns, 0)
    full = lambda a: pl.BlockSpec(a.shape, fixed)
    tab = pl.BlockSpec((tm, LANES), pos)
    in_specs = [pl.BlockSpec((tm, D_MODEL), row), full(pw['w_in']), full(pw['q_norm']), full(pw['w_q']),
                full(pw['kv_norm']), full(pw['w_k']), full(pw['w_v']), tab, tab, tab, tab,
                full(pw['sgu_g']), full(pw['sgu_b']), full(pw['ws_cat']), full(pw['bs_tab']), full(pw['mn_c'])]
    out_shape = [jax.ShapeDtypeStruct((A_HEADS * HEAD_PAD, t_len), BF16),
                 jax.ShapeDtypeStruct((t_len, A_WIDTH), BF16),
                 jax.ShapeDtypeStruct((A_HEADS * VT_ROWS, t_len), BF16),
                 jax.ShapeDtypeStruct((B_HEADS * HEAD_PAD, t_len), BF16),
                 jax.ShapeDtypeStruct((t_len, B_HEADS * HEAD_PAD), BF16),
                 jax.ShapeDtypeStruct((B_HEADS * VT_ROWS, t_len), BF16),
                 jax.ShapeDtypeStruct((t_len, C_WIDTH), BF16)]
    out_specs = [pl.BlockSpec((A_HEADS * HEAD_PAD, tm), col),
                 pl.BlockSpec((tm, A_WIDTH), row),
                 pl.BlockSpec((A_HEADS * VT_ROWS, tm), col),
                 pl.BlockSpec((B_HEADS * HEAD_PAD, tm), col),
                 pl.BlockSpec((tm, B_HEADS * HEAD_PAD), row),
                 pl.BlockSpec((B_HEADS * VT_ROWS, tm), col),
                 pl.BlockSpec((tm, C_WIDTH), row)]
    return pl.pallas_call(
        _inproj_kernel, grid=(nt,), in_specs=in_specs, out_specs=out_specs, out_shape=out_shape,
        compiler_params=pltpu.CompilerParams(dimension_semantics=("arbitrary",), vmem_limit_bytes=VMEM_LIMIT),
        name="inproj",
    )(x2, pw['w_in'], pw['q_norm'], pw['w_q'], pw['kv_norm'], pw['w_k'], pw['w_v'],
      tabs['cq'], tabs['sq'], tabs['ck'], tabs['sk'],
      pw['sgu_g'], pw['sgu_b'], pw['ws_cat'], pw['bs_tab'], pw['mn_c'])


def _dilated_kernel(qT_ref, k_ref, vT_ref, bias_ref, o_ref, s_ref, *, s_len):
    nblk = qT_ref.shape[1] // TQ_A
    first = pl.program_id(2) * nblk

    def window(i):
        t0 = (first + i) * TQ_A
        w0 = jnp.clip(t0 - HALO, 0, s_len - WIN_A)
        b0 = w0 - t0 + 2 * HALO
        return pl.multiple_of(w0, TQ_A), pl.multiple_of(b0, TQ_A), pl.multiple_of(i * TQ_A, TQ_A)

    def scores(i, slot):
        w0, b0, q0 = window(i)
        qT = jnp.concatenate([qT_ref[hh * HEAD_PAD:(hh + 1) * HEAD_PAD, pl.ds(q0, TQ_A)] for hh in range(2)], axis=1)
        s = _dot(k_ref[pl.ds(w0, WIN_A), :], qT)
        m = []
        for hh in range(2):
            s_h = s[:, hh * TQ_A:(hh + 1) * TQ_A] + bias_ref[hh, pl.ds(b0, WIN_A), :]
            s_ref[slot, :, hh * TQ_A:(hh + 1) * TQ_A] = s_h
            m.append(jnp.max(s_h, axis=0, keepdims=True))
        return tuple(m)

    def finish(i, slot, m):
        w0, _, q0 = window(i)
        for hh in range(2):
            p = jnp.exp2(s_ref[slot, :, hh * TQ_A:(hh + 1) * TQ_A] - m[hh]).astype(BF16)
            acc = _dot(vT_ref[hh * VT_ROWS:(hh + 1) * VT_ROWS, pl.ds(w0, WIN_A)], p)
            o_ref[hh * HEAD_DIM:(hh + 1) * HEAD_DIM, pl.ds(q0, TQ_A)] = acc[:HEAD_DIM] / acc[HEAD_DIM:HEAD_DIM + 1]

    def body(i, m):
        for u in range(DIL_UNROLL):
            j = DIL_UNROLL * i + u
            m_next = scores(jnp.minimum(j + 1, nblk - 1), (u + 1) % 2)
            finish(j, u % 2, m)
            m = m_next
        return m

    lax.fori_loop(0, nblk // DIL_UNROLL, body, scores(0, 0))


def _dilated(qaT, ka, vaT, bias, batch, s_len):
    chunk = _tile(s_len, 4096)
    assert (chunk // TQ_A) % DIL_UNROLL == 0 and DIL_UNROLL % 2 == 0
    nc = s_len // chunk
    t_len = batch * s_len
    once = pl.Buffered(1)
    return pl.pallas_call(
        functools.partial(_dilated_kernel, s_len=s_len),
        grid=(batch, A_HEADS // 2, nc),
        in_specs=[pl.BlockSpec((2 * HEAD_PAD, chunk), lambda b, h, c: (h, b * nc + c)),
                  pl.BlockSpec((s_len, LANES), lambda b, h, c: (b, h), pipeline_mode=once),
                  pl.BlockSpec((2 * VT_ROWS, s_len), lambda b, h, c: (h, b), pipeline_mode=once),
                  pl.BlockSpec((2, BIAS_ROWS, TQ_A), lambda b, h, c: (h, 0, 0), pipeline_mode=once)],
        out_specs=pl.BlockSpec((2 * HEAD_DIM, chunk), lambda b, h, c: (h, b * nc + c)),
        out_shape=jax.ShapeDtypeStruct((A_WIDTH, t_len), F32),
        scratch_shapes=[pltpu.VMEM((2, WIN_A, 2 * TQ_A + LANES), F32)],
        compiler_params=pltpu.CompilerParams(dimension_semantics=("arbitrary",) * 3, vmem_limit_bytes=VMEM_LIMIT),
        name="dilated",
    )(qaT, ka, vaT, bias)


def _mla_kernel(qT_ref, k_ref, vT_ref, o_ref, acc_ref, s_ref, *, tk, nk, unroll):
    qT = qT_ref[...]
    tq = qT.shape[1]
    acc_ref[...] = jnp.zeros_like(acc_ref)

    def scores(j, slot):
        off = pl.multiple_of(j * tk, tk)
        s = _dot(k_ref[pl.ds(off, tk), :], qT)
        s_ref[slot, :, :tq] = s
        return jnp.max(s, axis=0, keepdims=True)

    def step(j, slot, m_prev, m_tile):
        m_tile_next = scores(jnp.minimum(j + 1, nk - 1), 1 - slot)
        off = pl.multiple_of(j * tk, tk)
        m_new = jnp.maximum(m_prev, m_tile)
        alpha = jnp.exp2(m_prev - m_new)
        p = jnp.exp2(s_ref[slot, :, :tq] - m_new).astype(BF16)
        acc_ref[...] = acc_ref[...] * alpha + _dot(vT_ref[:, pl.ds(off, tk)], p)
        return m_new, m_tile_next

    def body(i, carry):
        for u in range(unroll):
            carry = step(unroll * i + u, u % 2, *carry)
        return carry

    m_tile0 = scores(0, 0)
    lax.fori_loop(0, nk // unroll, body, (jnp.full((1, tq), NEG, F32), m_tile0))
    acc = acc_ref[...]
    o_ref[...] = acc[:B_VDIM] / acc[B_VDIM:B_VDIM + 1]


def _mla(qbT, kb, vbT, batch, s_len, tq, tk):
    nq = s_len // tq
    nk = s_len // tk
    unroll = min(MLA_UNROLL, nk)
    assert nk % unroll == 0 and unroll % 2 == 0
    t_len = batch * s_len
    return pl.pallas_call(
        functools.partial(_mla_kernel, tk=tk, nk=nk, unroll=unroll),
        grid=(batch, B_HEADS, nq),
        in_specs=[pl.BlockSpec((HEAD_PAD, tq), lambda b, h, q: (h, b * nq + q)),
                  pl.BlockSpec((s_len, HEAD_PAD), lambda b, h, q: (b, h)),
                  pl.BlockSpec((VT_ROWS, s_len), lambda b, h, q: (h, b))],
        out_specs=pl.BlockSpec((B_VDIM, tq), lambda b, h, q: (h, b * nq + q)),
        out_shape=jax.ShapeDtypeStruct((B_WIDTH, t_len), F32),
        scratch_shapes=[pltpu.VMEM((VT_ROWS, tq), F32), pltpu.VMEM((2, tk, tq + LANES), F32)],
        compiler_params=pltpu.CompilerParams(dimension_semantics=("arbitrary",) * 3, vmem_limit_bytes=VMEM_LIMIT),
        name="mla",
    )(qbT, kb, vbT)


def _merge_kernel(oaT_ref, obT_ref, yc_ref, x_ref, wa_ref, wb_ref, wc_ref, mna_ref, mnb_ref, g_ref, b_ref, o_ref):
    for r in range(0, x_ref.shape[0], MERGE_SUB):
        rows = slice(r, r + MERGE_SUB)
        ya = (_rms_scale(oaT_ref[:, rows].T) * mna_ref[...]).astype(BF16)
        yb = (_rms_scale(obT_ref[:, rows].T) * mnb_ref[...]).astype(BF16)
        y = _dot(ya, wa_ref[...]) + _dot(yb, wb_ref[...]) + _dot(yc_ref[rows, :], wc_ref[...])
        o_ref[rows, :] = _layer_norm(ALPHA * x_ref[rows, :] + y, g_ref[...], b_ref[...])


def _merge(oaT, obT, yc, x2, pw, tm):
    t_len = x2.shape[0]
    row = lambda i: (i, 0)
    col = lambda i: (0, i)
    full = lambda a: pl.BlockSpec(a.shape, lambda i: (0, 0))
    return pl.pallas_call(
        _merge_kernel, grid=(t_len // tm,),
        in_specs=[pl.BlockSpec((A_WIDTH, tm), col), pl.BlockSpec((B_WIDTH, tm), col),
                  pl.BlockSpec((tm, C_WIDTH), row), pl.BlockSpec((tm, D_MODEL), row),
                  full(pw['wo_a']), full(pw['wo_b']), full(pw['wo_c']), full(pw['mn_a']), full(pw['mn_b']),
                  full(pw['ln1_g']), full(pw['ln1_b'])],
        out_specs=pl.BlockSpec((tm, D_MODEL), row),
        out_shape=jax.ShapeDtypeStruct((t_len, D_MODEL), F32),
        compiler_params=pltpu.CompilerParams(dimension_semantics=("arbitrary",), vmem_limit_bytes=VMEM_LIMIT),
        name="merge",
    )(oaT, obT, yc, x2, pw['wo_a'], pw['wo_b'], pw['wo_c'], pw['mn_a'], pw['mn_b'], pw['ln1_g'], pw['ln1_b'])


def _ffn_kernel(x_ref, w1_ref, b1_ref, w2_ref, b2_ref, g_ref, b_ref, o_ref):
    for r in range(0, x_ref.shape[0], FFN_SUB):
        rows = slice(r, r + FFN_SUB)
        x = x_ref[rows, :]
        hid = _dot(x.astype(BF16), w1_ref[...]) + b1_ref[...]
        hid = jnp.square(jnp.maximum(hid, 0.0)).astype(BF16)
        f = _dot(hid, w2_ref[...]) + b2_ref[...]
        o_ref[rows, :] = _layer_norm(ALPHA * x + f, g_ref[...], b_ref[...])


def _ffn(x2, pw, tm):
    t_len = x2.shape[0]
    fixed = lambda i: (0, 0)
    once = pl.Buffered(1)
    return pl.pallas_call(
        _ffn_kernel, grid=(t_len // tm,),
        in_specs=[pl.BlockSpec((tm, D_MODEL), lambda i: (i, 0)),
                  pl.BlockSpec((D_MODEL, D_FF), fixed, pipeline_mode=once),
                  pl.BlockSpec((1, D_FF), fixed),
                  pl.BlockSpec((D_FF, D_MODEL), fixed, pipeline_mode=once),
                  pl.BlockSpec((1, D_MODEL), fixed), pl.BlockSpec((1, D_MODEL), fixed),
                  pl.BlockSpec((1, D_MODEL), fixed)],
        out_specs=pl.BlockSpec((tm, D_MODEL), lambda i: (i, 0)),
        out_shape=jax.ShapeDtypeStruct((t_len, D_MODEL), F32),
        compiler_params=pltpu.CompilerParams(dimension_semantics=("arbitrary",), vmem_limit_bytes=VMEM_LIMIT),
        name="ffn",
    )(x2, pw['w_ff1'], pw['b_ff1'], pw['w_ff2'], pw['b_ff2'], pw['ln2_g'], pw['ln2_b'])


def _swap_halves(w):
    half = w.shape[-1] // 2
    return jnp.concatenate([w[..., half:], w[..., :half]], axis=-1)


def _pack_layer(l, w_in, q_norm, w_q_up, kv_norm, w_kv_up, sgu_ln_g, sgu_ln_b, sgu_w, sgu_b, mix_norm,
                w_out, ln1_g, ln1_b, w_ff1, b_ff1, w_ff2, b_ff2, ln2_g, ln2_b):
    w = w_in[l]
    zc = lambda n: jnp.zeros((D_MODEL, n), F32)
    off_cq = 3 * A_WIDTH
    off_ckv = off_cq + Q_RANK
    off_kpe = off_ckv + KV_RANK
    off_u = off_kpe + B_ROPE
    kpe = w[:, off_kpe:off_u]
    w_packed = jnp.concatenate([w[:, :off_cq], w[:, off_ckv:off_kpe], w[:, off_cq:off_ckv], zc(64),
                                zc(64), kpe, _swap_halves(kpe), w[:, off_u:]], axis=1)
    wq = w_q_up[l].reshape(Q_RANK, B_HEADS, B_NOPE + B_ROPE)
    wq = jnp.concatenate([wq, _swap_halves(wq[..., B_NOPE:])], axis=-1).reshape(Q_RANK, B_HEADS * HEAD_PAD)
    wkv = w_kv_up[l].reshape(KV_RANK, B_HEADS, B_NOPE + B_VDIM)
    wk = jnp.concatenate([wkv[..., :B_NOPE], jnp.zeros((KV_RANK, B_HEADS, HEAD_PAD - B_NOPE), F32)], axis=-1)
    row = lambda a: a.reshape(1, -1)
    mn = mix_norm[l]
    return {
        'w_in': w_packed.astype(BF16),
        'q_norm': row(q_norm[l]), 'w_q': wq.astype(BF16),
        'kv_norm': row(kv_norm[l]), 'w_k': wk.reshape(KV_RANK, B_HEADS * HEAD_PAD).astype(BF16),
        'w_v': wkv[..., B_NOPE:].reshape(KV_RANK, B_WIDTH).astype(BF16),
        'sgu_g': row(sgu_ln_g[l]), 'sgu_b': row(sgu_ln_b[l]),
        'ws_cat': jnp.concatenate([sgu_w[l][g] for g in range(C_GROUPS)], axis=1).astype(BF16),
        'bs_tab': jnp.repeat(sgu_b[l].T, C_GROUP_DIM, axis=1),
        'mn_a': row(mn[:A_WIDTH]), 'mn_b': row(mn[A_WIDTH:A_WIDTH + B_WIDTH]), 'mn_c': row(mn[A_WIDTH + B_WIDTH:]),
        'wo_a': w_out[l][:A_WIDTH].astype(BF16), 'wo_b': w_out[l][A_WIDTH:A_WIDTH + B_WIDTH].astype(BF16),
        'wo_c': w_out[l][A_WIDTH + B_WIDTH:].astype(BF16),
        'ln1_g': row(ln1_g[l]), 'ln1_b': row(ln1_b[l]),
        'w_ff1': w_ff1[l].astype(BF16), 'b_ff1': row(b_ff1[l]), 'w_ff2': w_ff2[l].astype(BF16),
        'b_ff2': row(b_ff2[l]), 'ln2_g': row(ln2_g[l]), 'ln2_b': row(ln2_b[l]),
    }


def _rope_tables(s_len):
    inv_freq = ROPE_THETA ** (-jnp.arange(0, B_ROPE, 2, dtype=F32) / B_ROPE)
    ang = jnp.arange(s_len, dtype=F32)[:, None] * inv_freq[None, :]
    cos, sin = jnp.cos(ang), jnp.sin(ang)
    ones = jnp.ones((s_len, B_NOPE), F32)
    zeros = lambda n: jnp.zeros((s_len, n), F32)
    pad = HEAD_PAD - B_NOPE - B_ROPE
    c_tab = jnp.concatenate([cos, cos, zeros(pad)], axis=1)
    s_tab = jnp.concatenate([-sin, sin, zeros(pad)], axis=1)
    q_scale = (B_NOPE + B_ROPE) ** -0.5 * LOG2E
    return {'cq': q_scale * jnp.concatenate([ones, c_tab], axis=1),
            'sq': q_scale * jnp.concatenate([zeros(B_NOPE), s_tab], axis=1),
            'ck': jnp.concatenate([zeros(B_NOPE), c_tab], axis=1),
            'sk': jnp.concatenate([zeros(B_NOPE), s_tab], axis=1)}


def _dilated_bias():
    e = jnp.arange(BIAS_ROWS)[:, None]
    ql = jnp.arange(TQ_A)[None, :]
    dist = jnp.abs(e - 2 * HALO - ql)
    cnt = sum(((dist % d == 0) & (dist <= (w // (2 * d)) * d)).astype(F32) for w, d in DILATED_PATTERNS)
    base = jnp.where(cnt > 0, jnp.log2(jnp.maximum(cnt, 1.0)), NEG)
    slopes = jnp.asarray(2.0 ** (-8.0 * np.arange(1, A_HEADS + 1) / A_HEADS), dtype=F32)
    return base[None] - (slopes * LOG2E)[:, None, None] * dist.astype(F32)[None]


def _tile(n, pref):
    t = min(pref, n)
    assert n % t == 0
    return t


def kernel(x, w_in, q_norm, w_q_up, kv_norm, w_kv_up, sgu_ln_g, sgu_ln_b, sgu_w, sgu_b, mix_norm, w_out,
           ln1_g, ln1_b, w_ff1, b_ff1, w_ff2, b_ff2, ln2_g, ln2_b):
    batch, s_len, _ = x.shape
    assert s_len >= WIN_A and s_len % TQ_A == 0
    t_len = batch * s_len
    tabs = _rope_tables(s_len)
    bias = _dilated_bias()
    x2 = x.reshape(t_len, D_MODEL)
    for l in range(DEPTH):
        pw = _pack_layer(l, w_in, q_norm, w_q_up, kv_norm, w_kv_up, sgu_ln_g, sgu_ln_b, sgu_w, sgu_b, mix_norm,
                         w_out, ln1_g, ln1_b, w_ff1, b_ff1, w_ff2, b_ff2, ln2_g, ln2_b)
        qaT, ka, vaT, qbT, kb, vbT, yc = _inproj(x2, pw, tabs, s_len, _tile(s_len, 512))
        oaT = _dilated(qaT, ka, vaT, bias, batch, s_len)
        obT = _mla(qbT, kb, vbT, batch, s_len, _tile(s_len, 1024), _tile(s_len, 256))
        x1 = _merge(oaT, obT, yc, x2, pw, _tile(t_len, 1024))
        x2 = _ffn(x1, pw, _tile(t_len, 512))
    return x2.reshape(batch, s_len, D_MODEL)
```

```python
import functools
import math

import numpy as np
import jax
import jax.numpy as jnp
from jax import lax
from jax.experimental import pallas as pl
from jax.experimental.pallas import tpu as pltpu

BF16 = jnp.bfloat16
F32 = jnp.float32

D_MODEL = 1024
DEPTH = 2
HEAD_DIM = 64
A_HEADS = 6
A_WIDTH = A_HEADS * HEAD_DIM
DILATED_PATTERNS = ((128, 1), (512, 4), (2048, 16))
B_HEADS = 6
B_NOPE = 64
B_ROPE = 32
B_VDIM = 64
Q_RANK = 192
KV_RANK = 128
B_WIDTH = B_HEADS * B_VDIM
ROPE_THETA = 10000.0
C_GROUPS = 4
C_GROUP_DIM = 64
C_WIDTH = C_GROUPS * C_GROUP_DIM
CHUNK = 128
D_FF = 4 * D_MODEL
ALPHA = (2 * DEPTH) ** 0.25
EPS = 1e-5
NEG = -1e30
LOG2E = math.log2(math.e)

LANES = 128
BF16_SUBLANES = 16
VT_ROWS = B_VDIM + BF16_SUBLANES
HEAD_PAD = LANES

P_QA, P_KA, P_VA = 0, A_WIDTH, 2 * A_WIDTH
P_CKV = 3 * A_WIDTH
P_CQ = P_CKV + KV_RANK
P_KPE = P_CQ + 256
P_U = P_KPE + LANES
P_V = P_U + C_WIDTH
P_PACKED = P_V + C_WIDTH

HALO = max((w // (2 * d)) * d for w, d in DILATED_PATTERNS)
TQ_A = 256
WIN_A = TQ_A + 2 * HALO
BIAS_ROWS = WIN_A + 2 * HALO

VMEM_LIMIT = 48 * 1024 * 1024
MLA_UNROLL = 32
DIL_UNROLL = 4
MERGE_SUB = 256
INPROJ_SUB = 256
FFN_SUB = 256


def _rms_scale(x):
    return x * lax.rsqrt(jnp.mean(x * x, axis=-1, keepdims=True) + EPS)


def _layer_norm(x, g, b):
    mu = jnp.mean(x, axis=-1, keepdims=True)
    xc = x - mu
    var = jnp.mean(xc * xc, axis=-1, keepdims=True)
    return xc * lax.rsqrt(var + EPS) * g + b


def _dot(a, b):
    return jnp.dot(a, b, preferred_element_type=F32)


def _inproj_kernel(x_ref, w_ref, qn_ref, wq_ref, kvn_ref, wk_ref, wv_ref, cq_ref, sq_ref, ck_ref, sk_ref,
                   lng_ref, lnb_ref, ws_ref, bs_ref, mnc_ref,
                   qaT_ref, ka_ref, vaT_ref, qbT_ref, kb_ref, vbT_ref, yc_ref):
    sub = INPROJ_SUB
    ones_tile = (lax.broadcasted_iota(jnp.int32, (BF16_SUBLANES, sub), 0) == 0).astype(BF16)
    zeros_half = jnp.zeros((HEAD_DIM, sub), BF16)
    group = lax.broadcasted_iota(jnp.int32, (CHUNK, C_WIDTH), 1) // C_GROUP_DIM

    def put_values_t(vT, out_ref, width, rows):
        for hd in range(A_HEADS):
            out_ref[hd * VT_ROWS:hd * VT_ROWS + width, rows] = vT[hd * width:(hd + 1) * width].astype(BF16)
            out_ref[hd * VT_ROWS + width:(hd + 1) * VT_ROWS, rows] = ones_tile

    for r in range(0, x_ref.shape[0], sub):
        rows = slice(r, r + sub)
        h = _dot(x_ref[rows, :].astype(BF16), w_ref[...])

        qaT = (h[:, P_QA:P_QA + A_WIDTH] * (HEAD_DIM ** -0.5 * LOG2E)).T.astype(BF16)
        for hd in range(A_HEADS):
            q_h = qaT[hd * HEAD_DIM:(hd + 1) * HEAD_DIM]
            lo, hi = (q_h, zeros_half) if hd % 2 == 0 else (zeros_half, q_h)
            qaT_ref[hd * HEAD_PAD:hd * HEAD_PAD + HEAD_DIM, rows] = lo
            qaT_ref[hd * HEAD_PAD + HEAD_DIM:(hd + 1) * HEAD_PAD, rows] = hi
        ka_ref[rows, :] = h[:, P_KA:P_KA + A_WIDTH].astype(BF16)
        put_values_t(h[:, P_VA:P_VA + A_WIDTH].T, vaT_ref, HEAD_DIM, rows)

        c_kv = (_rms_scale(h[:, P_CKV:P_CKV + KV_RANK]) * kvn_ref[...]).astype(BF16)
        kmat = _dot(c_kv, wk_ref[...])
        vmat = _dot(c_kv, wv_ref[...])
        g_k = h[:, P_KPE:P_KPE + LANES]
        kpe = g_k * ck_ref[rows, :] + pltpu.roll(g_k, 96, 1) * sk_ref[rows, :]
        c_q = (_rms_scale(h[:, P_CQ:P_CQ + Q_RANK]) * qn_ref[...]).astype(BF16)
        qmat = _dot(c_q, wq_ref[...])
        cq_t, sq_t = cq_ref[rows, :], sq_ref[rows, :]
        for hd in range(B_HEADS):
            sl = slice(hd * HEAD_PAD, (hd + 1) * HEAD_PAD)
            kb_ref[rows, sl] = (kmat[:, sl] + kpe).astype(BF16)
            g_q = qmat[:, sl]
            q_h = g_q * cq_t + pltpu.roll(g_q, 96, 1) * sq_t
            qbT_ref[sl, rows] = q_h.T.astype(BF16)
        put_values_t(vmat.T, vbT_ref, B_VDIM, rows)

        z = jax.nn.gelu(h[:, P_U:P_U + 2 * C_WIDTH])
        u = z[:, :C_WIDTH]
        v = _layer_norm(z[:, C_WIDTH:], lng_ref[...], lnb_ref[...])
        for c in range(sub // CHUNK):
            crows = slice(c * CHUNK, (c + 1) * CHUNK)
            vc = v[crows]
            stack = jnp.concatenate([jnp.where(group == g, vc, 0.0) for g in range(C_GROUPS)], axis=0).astype(BF16)
            mixed = _dot(ws_ref[...], stack) + bs_ref[...]
            yc = u[crows] * mixed
            yc_ref[r + c * CHUNK:r + (c + 1) * CHUNK, :] = (_rms_scale(yc) * mnc_ref[...]).astype(BF16)


def _inproj(x2, pw, tabs, s_len, tm):
    t_len = x2.shape[0]
    nt = t_len // tm
    ns = s_len // tm
    row = lambda i: (i, 0)
    col = lambda i: (0, i)
    fixed = lambda i: (0, 0)
    pos = lambda i: (i % ns, 0)
    full = lambda a: pl.BlockSpec(a.shape, fixed)
    tab = pl.BlockSpec((tm, LANES), pos)
    in_specs = [pl.BlockSpec((tm, D_MODEL), row), full(pw['w_in']), full(pw['q_norm']), full(pw['w_q']),
                full(pw['kv_norm']), full(pw['w_k']), full(pw['w_v']), tab, tab, tab, tab,
                full(pw['sgu_g']), full(pw['sgu_b']), full(pw['ws_cat']), full(pw['bs_tab']), full(pw['mn_c'])]
    out_shape = [jax.ShapeDtypeStruct((A_HEADS * HEAD_PAD, t_len), BF16),
                 jax.ShapeDtypeStruct((t_len, A_WIDTH), BF16),
                 jax.ShapeDtypeStruct((A_HEADS * VT_ROWS, t_len), BF16),
                 jax.ShapeDtypeStruct((B_HEADS * HEAD_PAD, t_len), BF16),
                 jax.ShapeDtypeStruct((t_len, B_HEADS * HEAD_PAD), BF16),
                 jax.ShapeDtypeStruct((B_HEADS * VT_ROWS, t_len), BF16),
                 jax.ShapeDtypeStruct((t_len, C_WIDTH), BF16)]
    out_specs = [pl.BlockSpec((A_HEADS * HEAD_PAD, tm), col),
                 pl.BlockSpec((tm, A_WIDTH), row),
                 pl.BlockSpec((A_HEADS * VT_ROWS, tm), col),
                 pl.BlockSpec((B_HEADS * HEAD_PAD, tm), col),
                 pl.BlockSpec((tm, B_HEADS * HEAD_PAD), row),
                 pl.BlockSpec((B_HEADS * VT_ROWS, tm), col),
                 pl.BlockSpec((tm, C_WIDTH), row)]
    return pl.pallas_call(
        _inproj_kernel, grid=(nt,), in_specs=in_specs, out_specs=out_specs, out_shape=out_shape,
        compiler_params=pltpu.CompilerParams(dimension_semantics=("arbitrary",), vmem_limit_bytes=VMEM_LIMIT),
        name="inproj",
    )(x2, pw['w_in'], pw['q_norm'], pw['w_q'], pw['kv_norm'], pw['w_k'], pw['w_v'],
      tabs['cq'], tabs['sq'], tabs['ck'], tabs['sk'],
      pw['sgu_g'], pw['sgu_b'], pw['ws_cat'], pw['bs_tab'], pw['mn_c'])


def _dilated_kernel(qT_ref, k_ref, vT_ref, bias_ref, o_ref, s_ref, *, s_len):
    nblk = qT_ref.shape[1] // TQ_A
    first = pl.program_id(2) * nblk

    def window(i):
        t0 = (first + i) * TQ_A
        w0 = jnp.clip(t0 - HALO, 0, s_len - WIN_A)
        b0 = w0 - t0 + 2 * HALO
        return pl.multiple_of(w0, TQ_A), pl.multiple_of(b0, TQ_A), pl.multiple_of(i * TQ_A, TQ_A)

    def scores(i, slot):
        w0, b0, q0 = window(i)
        qT = jnp.concatenate([qT_ref[hh * HEAD_PAD:(hh + 1) * HEAD_PAD, pl.ds(q0, TQ_A)] for hh in range(2)], axis=1)
        s = _dot(k_ref[pl.ds(w0, WIN_A), :], qT)
        m = []
        for hh in range(2):
            s_h = s[:, hh * TQ_A:(hh + 1) * TQ_A] + bias_ref[hh, pl.ds(b0, WIN_A), :]
            s_ref[slot, :, hh * TQ_A:(hh + 1) * TQ_A] = s_h
            m.append(jnp.max(s_h, axis=0, keepdims=True))
        return tuple(m)

    def finish(i, slot, m):
        w0, _, q0 = window(i)
        for hh in range(2):
            p = jnp.exp2(s_ref[slot, :, hh * TQ_A:(hh + 1) * TQ_A] - m[hh]).astype(BF16)
            acc = _dot(vT_ref[hh * VT_ROWS:(hh + 1) * VT_ROWS, pl.ds(w0, WIN_A)], p)
            o_ref[hh * HEAD_DIM:(hh + 1) * HEAD_DIM, pl.ds(q0, TQ_A)] = acc[:HEAD_DIM] / acc[HEAD_DIM:HEAD_DIM + 1]

    def body(i, m):
        for u in range(DIL_UNROLL):
            j = DIL_UNROLL * i + u
            m_next = scores(jnp.minimum(j + 1, nblk - 1), (u + 1) % 2)
            finish(j, u % 2, m)
            m = m_next
        return m

    lax.fori_loop(0, nblk // DIL_UNROLL, body, scores(0, 0))


def _dilated(qaT, ka, vaT, bias, batch, s_len):
    chunk = _tile(s_len, 4096)
    assert (chunk // TQ_A) % DIL_UNROLL == 0 and DIL_UNROLL % 2 == 0
    nc = s_len // chunk
    t_len = batch * s_len
    once = pl.Buffered(1)
    return pl.pallas_call(
        functools.partial(_dilated_kernel, s_len=s_len),
        grid=(batch, A_HEADS // 2, nc),
        in_specs=[pl.BlockSpec((2 * HEAD_PAD, chunk), lambda b, h, c: (h, b * nc + c)),
                  pl.BlockSpec((s_len, LANES), lambda b, h, c: (b, h), pipeline_mode=once),
                  pl.BlockSpec((2 * VT_ROWS, s_len), lambda b, h, c: (h, b), pipeline_mode=once),
                  pl.BlockSpec((2, BIAS_ROWS, TQ_A), lambda b, h, c: (h, 0, 0), pipeline_mode=once)],
        out_specs=pl.BlockSpec((2 * HEAD_DIM, chunk), lambda b, h, c: (h, b * nc + c)),
        out_shape=jax.ShapeDtypeStruct((A_WIDTH, t_len), F32),
        scratch_shapes=[pltpu.VMEM((2, WIN_A, 2 * TQ_A + LANES), F32)],
        compiler_params=pltpu.CompilerParams(dimension_semantics=("arbitrary",) * 3, vmem_limit_bytes=VMEM_LIMIT),
        name="dilated",
    )(qaT, ka, vaT, bias)


def _mla_kernel(qT_ref, k_ref, vT_ref, o_ref, acc_ref, s_ref, *, tk, nk, unroll):
    qT = qT_ref[...]
    tq = qT.shape[1]
    acc_ref[...] = jnp.zeros_like(acc_ref)

    def scores(j, slot):
        off = pl.multiple_of(j * tk, tk)
        s = _dot(k_ref[pl.ds(off, tk), :], qT)
        s_ref[slot, :, :tq] = s
        return jnp.max(s, axis=0, keepdims=True)

    def step(j, slot, m_prev, m_tile):
        m_tile_next = scores(jnp.minimum(j + 1, nk - 1), 1 - slot)
        off = pl.multiple_of(j * tk, tk)
        m_new = jnp.maximum(m_prev, m_tile)
        alpha = jnp.exp2(m_prev - m_new)
        p = jnp.exp2(s_ref[slot, :, :tq] - m_new).astype(BF16)
        acc_ref[...] = acc_ref[...] * alpha + _dot(vT_ref[:, pl.ds(off, tk)], p)
        return m_new, m_tile_next

    def body(i, carry):
        for u in range(unroll):
            carry = step(unroll * i + u, u % 2, *carry)
        return carry

    m_tile0 = scores(0, 0)
    lax.fori_loop(0, nk // unroll, body, (jnp.full((1, tq), NEG, F32), m_tile0))
    acc = acc_ref[...]
    o_ref[...] = acc[:B_VDIM] / acc[B_VDIM:B_VDIM + 1]


def _mla(qbT, kb, vbT, batch, s_len, tq, tk):
    nq = s_len // tq
    nk = s_len // tk
    unroll = min(MLA_UNROLL, nk)
    assert nk % unroll == 0 and unroll % 2 == 0
    t_len = batch * s_len
    return pl.pallas_call(
        functools.partial(_mla_kernel, tk=tk, nk=nk, unroll=unroll),
        grid=(batch, B_HEADS, nq),
        in_specs=[pl.BlockSpec((HEAD_PAD, tq), lambda b, h, q: (h, b * nq + q)),
                  pl.BlockSpec((s_len, HEAD_PAD), lambda b, h, q: (b, h)),
                  pl.BlockSpec((VT_ROWS, s_len), lambda b, h, q: (h, b))],
        out_specs=pl.BlockSpec((B_VDIM, tq), lambda b, h, q: (h, b * nq + q)),
        out_shape=jax.ShapeDtypeStruct((B_WIDTH, t_len), F32),
        scratch_shapes=[pltpu.VMEM((VT_ROWS, tq), F32), pltpu.VMEM((2, tk, tq + LANES), F32)],
        compiler_params=pltpu.CompilerParams(dimension_semantics=("arbitrary",) * 3, vmem_limit_bytes=VMEM_LIMIT),
        name="mla",
    )(qbT, kb, vbT)


def _merge_kernel(oaT_ref, obT_ref, yc_ref, x_ref, wa_ref, wb_ref, wc_ref, mna_ref, mnb_ref, g_ref, b_ref, o_ref):
    for r in range(0, x_ref.shape[0], MERGE_SUB):
        rows = slice(r, r + MERGE_SUB)
        ya = (_rms_scale(oaT_ref[:, rows].T) * mna_ref[...]).astype(BF16)
        yb = (_rms_scale(obT_ref[:, rows].T) * mnb_ref[...]).astype(BF16)
        y = _dot(ya, wa_ref[...]) + _dot(yb, wb_ref[...]) + _dot(yc_ref[rows, :], wc_ref[...])
        o_ref[rows, :] = _layer_norm(ALPHA * x_ref[rows, :] + y, g_ref[...], b_ref[...])


def _merge(oaT, obT, yc, x2, pw, tm):
    t_len = x2.shape[0]
    row = lambda i: (i, 0)
    col = lambda i: (0, i)
    full = lambda a: pl.BlockSpec(a.shape, lambda i: (0, 0))
    return pl.pallas_call(
        _merge_kernel, grid=(t_len // tm,),
        in_specs=[pl.BlockSpec((A_WIDTH, tm), col), pl.BlockSpec((B_WIDTH, tm), col),
                  pl.BlockSpec((tm, C_WIDTH), row), pl.BlockSpec((tm, D_MODEL), row),
                  full(pw['wo_a']), full(pw['wo_b']), full(pw['wo_c']), full(pw['mn_a']), full(pw['mn_b']),
                  full(pw['ln1_g']), full(pw['ln1_b'])],
        out_specs=pl.BlockSpec((tm, D_MODEL), row),
        out_shape=jax.ShapeDtypeStruct((t_len, D_MODEL), F32),
        compiler_params=pltpu.CompilerParams(dimension_semantics=("arbitrary",), vmem_limit_bytes=VMEM_LIMIT),
        name="merge",
    )(oaT, obT, yc, x2, pw['wo_a'], pw['wo_b'], pw['wo_c'], pw['mn_a'], pw['mn_b'], pw['ln1_g'], pw['ln1_b'])


def _ffn_kernel(x_ref, w1_ref, b1_ref, w2_ref, b2_ref, g_ref, b_ref, o_ref):
    for r in range(0, x_ref.shape[0], FFN_SUB):
        rows = slice(r, r + FFN_SUB)
        x = x_ref[rows, :]
        hid = _dot(x.astype(BF16), w1_ref[...]) + b1_ref[...]
        hid = jnp.square(jnp.maximum(hid, 0.0)).astype(BF16)
        f = _dot(hid, w2_ref[...]) + b2_ref[...]
        o_ref[rows, :] = _layer_norm(ALPHA * x + f, g_ref[...], b_ref[...])


def _ffn(x2, pw, tm):
    t_len = x2.shape[0]
    fixed = lambda i: (0, 0)
    once = pl.Buffered(1)
    return pl.pallas_call(
        _ffn_kernel, grid=(t_len // tm,),
        in_specs=[pl.BlockSpec((tm, D_MODEL), lambda i: (i, 0)),
                  pl.BlockSpec((D_MODEL, D_FF), fixed, pipeline_mode=once),
                  pl.BlockSpec((1, D_FF), fixed),
                  pl.BlockSpec((D_FF, D_MODEL), fixed, pipeline_mode=once),
                  pl.BlockSpec((1, D_MODEL), fixed), pl.BlockSpec((1, D_MODEL), fixed),
                  pl.BlockSpec((1, D_MODEL), fixed)],
        out_specs=pl.BlockSpec((tm, D_MODEL), lambda i: (i, 0)),
        out_shape=jax.ShapeDtypeStruct((t_len, D_MODEL), F32),
        compiler_params=pltpu.CompilerParams(dimension_semantics=("arbitrary",), vmem_limit_bytes=VMEM_LIMIT),
        name="ffn",
    )(x2, pw['w_ff1'], pw['b_ff1'], pw['w_ff2'], pw['b_ff2'], pw['ln2_g'], pw['ln2_b'])


def _swap_halves(w):
    half = w.shape[-1] // 2
    return jnp.concatenate([w[..., half:], w[..., :half]], axis=-1)


def _pack_layer(l, w_in, q_norm, w_q_up, kv_norm, w_kv_up, sgu_ln_g, sgu_ln_b, sgu_w, sgu_b, mix_norm,
                w_out, ln1_g, ln1_b, w_ff1, b_ff1, w_ff2, b_ff2, ln2_g, ln2_b):
    w = w_in[l]
    zc = lambda n: jnp.zeros((D_MODEL, n), F32)
    off_cq = 3 * A_WIDTH
    off_ckv = off_cq + Q_RANK
    off_kpe = off_ckv + KV_RANK
    off_u = off_kpe + B_ROPE
    kpe = w[:, off_kpe:off_u]
    w_packed = jnp.concatenate([w[:, :off_cq], w[:, off_ckv:off_kpe], w[:, off_cq:off_ckv], zc(64),
                                zc(64), kpe, _swap_halves(kpe), w[:, off_u:]], axis=1)
    wq = w_q_up[l].reshape(Q_RANK, B_HEADS, B_NOPE + B_ROPE)
    wq = jnp.concatenate([wq, _swap_halves(wq[..., B_NOPE:])], axis=-1).reshape(Q_RANK, B_HEADS * HEAD_PAD)
    wkv = w_kv_up[l].reshape(KV_RANK, B_HEADS, B_NOPE + B_VDIM)
    wk = jnp.concatenate([wkv[..., :B_NOPE], jnp.zeros((KV_RANK, B_HEADS, HEAD_PAD - B_NOPE), F32)], axis=-1)
    row = lambda a: a.reshape(1, -1)
    mn = mix_norm[l]
    return {
        'w_in': w_packed.astype(BF16),
        'q_norm': row(q_norm[l]), 'w_q': wq.astype(BF16),
        'kv_norm': row(kv_norm[l]), 'w_k': wk.reshape(KV_RANK, B_HEADS * HEAD_PAD).astype(BF16),
        'w_v': wkv[..., B_NOPE:].reshape(KV_RANK, B_WIDTH).astype(BF16),
        'sgu_g': row(sgu_ln_g[l]), 'sgu_b': row(sgu_ln_b[l]),
        'ws_cat': jnp.concatenate([sgu_w[l][g] for g in range(C_GROUPS)], axis=1).astype(BF16),
        'bs_tab': jnp.repeat(sgu_b[l].T, C_GROUP_DIM, axis=1),
        'mn_a': row(mn[:A_WIDTH]), 'mn_b': row(mn[A_WIDTH:A_WIDTH + B_WIDTH]), 'mn_c': row(mn[A_WIDTH + B_WIDTH:]),
        'wo_a': w_out[l][:A_WIDTH].astype(BF16), 'wo_b': w_out[l][A_WIDTH:A_WIDTH + B_WIDTH].astype(BF16),
        'wo_c': w_out[l][A_WIDTH + B_WIDTH:].astype(BF16),
        'ln1_g': row(ln1_g[l]), 'ln1_b': row(ln1_b[l]),
        'w_ff1': w_ff1[l].astype(BF16), 'b_ff1': row(b_ff1[l]), 'w_ff2': w_ff2[l].astype(BF16),
        'b_ff2': row(b_ff2[l]), 'ln2_g': row(ln2_g[l]), 'ln2_b': row(ln2_b[l]),
    }


def _rope_tables(s_len):
    inv_freq = ROPE_THETA ** (-jnp.arange(0, B_ROPE, 2, dtype=F32) / B_ROPE)
    ang = jnp.arange(s_len, dtype=F32)[:, None] * inv_freq[None, :]
    cos, sin = jnp.cos(ang), jnp.sin(ang)
    ones = jnp.ones((s_len, B_NOPE), F32)
    zeros = lambda n: jnp.zeros((s_len, n), F32)
    pad = HEAD_PAD - B_NOPE - B_ROPE
    c_tab = jnp.concatenate([cos, cos, zeros(pad)], axis=1)
    s_tab = jnp.concatenate([-sin, sin, zeros(pad)], axis=1)
    q_scale = (B_NOPE + B_ROPE) ** -0.5 * LOG2E
    return {'cq': q_scale * jnp.concatenate([ones, c_tab], axis=1),
            'sq': q_scale * jnp.concatenate([zeros(B_NOPE), s_tab], axis=1),
            'ck': jnp.concatenate([zeros(B_NOPE), c_tab], axis=1),
            'sk': jnp.concatenate([zeros(B_NOPE), s_tab], axis=1)}


def _dilated_bias():
    e = jnp.arange(BIAS_ROWS)[:, None]
    ql = jnp.arange(TQ_A)[None, :]
    dist = jnp.abs(e - 2 * HALO - ql)
    cnt = sum(((dist % d == 0) & (dist <= (w // (2 * d)) * d)).astype(F32) for w, d in DILATED_PATTERNS)
    base = jnp.where(cnt > 0, jnp.log2(jnp.maximum(cnt, 1.0)), NEG)
    slopes = jnp.asarray(2.0 ** (-8.0 * np.arange(1, A_HEADS + 1) / A_HEADS), dtype=F32)
    return base[None] - (slopes * LOG2E)[:, None, None] * dist.astype(F32)[None]


def _tile(n, pref):
    t = min(pref, n)
    assert n % t == 0
    return t


def kernel(x, w_in, q_norm, w_q_up, kv_norm, w_kv_up, sgu_ln_g, sgu_ln_b, sgu_w, sgu_b, mix_norm, w_out,
           ln1_g, ln1_b, w_ff1, b_ff1, w_ff2, b_ff2, ln2_g, ln2_b):
    batch, s_len, _ = x.shape
    assert s_len >= WIN_A and s_len % TQ_A == 0
    t_len = batch * s_len
    tabs = _rope_tables(s_len)
    bias = _dilated_bias()
    x2 = x.reshape(t_len, D_MODEL)
    for l in range(DEPTH):
        pw = _pack_layer(l, w_in, q_norm, w_q_up, kv_norm, w_kv_up, sgu_ln_g, sgu_ln_b, sgu_w, sgu_b, mix_norm,
                         w_out, ln1_g, ln1_b, w_ff1, b_ff1, w_ff2, b_ff2, ln2_g, ln2_b)
        qaT, ka, vaT, qbT, kb, vbT, yc = _inproj(x2, pw, tabs, s_len, _tile(s_len, 1024))
        oaT = _dilated(qaT, ka, vaT, bias, batch, s_len)
        obT = _mla(qbT, kb, vbT, batch, s_len, _tile(s_len, 1024), _tile(s_len, 256))
        x1 = _merge(oaT, obT, yc, x2, pw, _tile(t_len, 1024))
        x2 = _ffn(x1, pw, _tile(t_len, 512))
    return x2.reshape(batch, s_len, D_MODEL)
```

```python
import functools
import math

import numpy as np
import jax
import jax.numpy as jnp
from jax import lax
from jax.experimental import pallas as pl
from jax.experimental.pallas import tpu as pltpu

BF16 = jnp.bfloat16
F32 = jnp.float32

D_MODEL = 1024
DEPTH = 2
HEAD_DIM = 64
A_HEADS = 6
A_WIDTH = A_HEADS * HEAD_DIM
DILATED_PATTERNS = ((128, 1), (512, 4), (2048, 16))
B_HEADS = 6
B_NOPE = 64
B_ROPE = 32
B_VDIM = 64
Q_RANK = 192
KV_RANK = 128
B_WIDTH = B_HEADS * B_VDIM
ROPE_THETA = 10000.0
C_GROUPS = 4
C_GROUP_DIM = 64
C_WIDTH = C_GROUPS * C_GROUP_DIM
CHUNK = 128
D_FF = 4 * D_MODEL
ALPHA = (2 * DEPTH) ** 0.25
EPS = 1e-5
NEG = -1e30
LOG2E = math.log2(math.e)

LANES = 128
BF16_SUBLANES = 16
VT_ROWS = B_VDIM + BF16_SUBLANES
HEAD_PAD = LANES

P_QA, P_KA, P_VA = 0, A_WIDTH, 2 * A_WIDTH
P_CKV = 3 * A_WIDTH
P_CQ = P_CKV + KV_RANK
P_KPE = P_CQ + 256
P_U = P_KPE + LANES
P_V = P_U + C_WIDTH
P_PACKED = P_V + C_WIDTH

HALO = max((w // (2 * d)) * d for w, d in DILATED_PATTERNS)
TQ_A = 256
WIN_A = TQ_A + 2 * HALO
BIAS_ROWS = WIN_A + 2 * HALO

VMEM_LIMIT = 48 * 1024 * 1024
MLA_UNROLL = 16
DIL_UNROLL = 4
POST_SUB = 256
INPROJ_SUB = 256

def _rms_scale(x):
    return x * lax.rsqrt(jnp.mean(x * x, axis=-1, keepdims=True) + EPS)


def _layer_norm(x, g, b):
    mu = jnp.mean(x, axis=-1, keepdims=True)
    xc = x - mu
    var = jnp.mean(xc * xc, axis=-1, keepdims=True)
    return xc * lax.rsqrt(var + EPS) * g + b


def _dot(a, b):
    return jnp.dot(a, b, preferred_element_type=F32)


def _inproj_kernel(x_ref, w_ref, qn_ref, wq_ref, kvn_ref, wk_ref, wv_ref, cq_ref, sq_ref, ck_ref, sk_ref,
                   lng_ref, lnb_ref, ws_ref, bs_ref, mnc_ref,
                   qaT_ref, ka_ref, vaT_ref, qbT_ref, kb_ref, vbT_ref, yc_ref):
    sub = INPROJ_SUB
    ones_tile = (lax.broadcasted_iota(jnp.int32, (BF16_SUBLANES, sub), 0) == 0).astype(BF16)
    zeros_half = jnp.zeros((HEAD_DIM, sub), BF16)
    group = lax.broadcasted_iota(jnp.int32, (CHUNK, C_WIDTH), 1) // C_GROUP_DIM

    def put_values_t(vT, out_ref, width, rows):
        for hd in range(A_HEADS):
            out_ref[hd * VT_ROWS:hd * VT_ROWS + width, rows] = vT[hd * width:(hd + 1) * width].astype(BF16)
            out_ref[hd * VT_ROWS + width:(hd + 1) * VT_ROWS, rows] = ones_tile

    for r in range(0, x_ref.shape[0], sub):
        rows = slice(r, r + sub)
        h = _dot(x_ref[rows, :].astype(BF16), w_ref[...])

        qaT = (h[:, P_QA:P_QA + A_WIDTH] * (HEAD_DIM ** -0.5 * LOG2E)).T.astype(BF16)
        for hd in range(A_HEADS):
            q_h = qaT[hd * HEAD_DIM:(hd + 1) * HEAD_DIM]
            lo, hi = (q_h, zeros_half) if hd % 2 == 0 else (zeros_half, q_h)
            qaT_ref[hd * HEAD_PAD:hd * HEAD_PAD + HEAD_DIM, rows] = lo
            qaT_ref[hd * HEAD_PAD + HEAD_DIM:(hd + 1) * HEAD_PAD, rows] = hi
        ka_ref[rows, :] = h[:, P_KA:P_KA + A_WIDTH].astype(BF16)
        put_values_t(h[:, P_VA:P_VA + A_WIDTH].T, vaT_ref, HEAD_DIM, rows)

        c_kv = (_rms_scale(h[:, P_CKV:P_CKV + KV_RANK]) * kvn_ref[...]).astype(BF16)
        kmat = _dot(c_kv, wk_ref[...])
        vmat = _dot(c_kv, wv_ref[...])
        g_k = h[:, P_KPE:P_KPE + LANES]
        kpe = g_k * ck_ref[rows, :] + pltpu.roll(g_k, 96, 1) * sk_ref[rows, :]
        c_q = (_rms_scale(h[:, P_CQ:P_CQ + Q_RANK]) * qn_ref[...]).astype(BF16)
        qmat = _dot(c_q, wq_ref[...])
        cq_t, sq_t = cq_ref[rows, :], sq_ref[rows, :]
        for hd in range(B_HEADS):
            sl = slice(hd * HEAD_PAD, (hd + 1) * HEAD_PAD)
            kb_ref[rows, sl] = (kmat[:, sl] + kpe).astype(BF16)
            g_q = qmat[:, sl]
            q_h = g_q * cq_t + pltpu.roll(g_q, 96, 1) * sq_t
            qbT_ref[sl, rows] = q_h.T.astype(BF16)
        put_values_t(vmat.T, vbT_ref, B_VDIM, rows)

        z = jax.nn.gelu(h[:, P_U:P_U + 2 * C_WIDTH])
        u = z[:, :C_WIDTH]
        v = _layer_norm(z[:, C_WIDTH:], lng_ref[...], lnb_ref[...])
        for c in range(sub // CHUNK):
            crows = slice(c * CHUNK, (c + 1) * CHUNK)
            vc = v[crows]
            stack = jnp.concatenate([jnp.where(group == g, vc, 0.0) for g in range(C_GROUPS)], axis=0).astype(BF16)
            mixed = _dot(ws_ref[...], stack) + bs_ref[...]
            yc = u[crows] * mixed
            yc_ref[r + c * CHUNK:r + (c + 1) * CHUNK, :] = (_rms_scale(yc) * mnc_ref[...]).astype(BF16)


def _inproj(x2, pw, tabs, s_len, tm):
    t_len = x2.shape[0]
    nt = t_len // tm
    ns = s_len // tm
    row = lambda i: (i, 0)
    col = lambda i: (0, i)
    fixed = lambda i: (0, 0)
    pos = lambda i: (i % ns, 0)
    full = lambda a: pl.BlockSpec(a.shape, fixed)
    tab = pl.BlockSpec((tm, LANES), pos)
    in_specs = [pl.BlockSpec((tm, D_MODEL), row), full(pw['w_in']), full(pw['q_norm']), full(pw['w_q']),
                full(pw['kv_norm']), full(pw['w_k']), full(pw['w_v']), tab, tab, tab, tab,
                full(pw['sgu_g']), full(pw['sgu_b']), full(pw['ws_cat']), full(pw['bs_tab']), full(pw['mn_c'])]
    out_shape = [jax.ShapeDtypeStruct((A_HEADS * HEAD_PAD, t_len), BF16),
                 jax.ShapeDtypeStruct((t_len, A_WIDTH), BF16),
                 jax.ShapeDtypeStruct((A_HEADS * VT_ROWS, t_len), BF16),
                 jax.ShapeDtypeStruct((B_HEADS * HEAD_PAD, t_len), BF16),
                 jax.ShapeDtypeStruct((t_len, B_HEADS * HEAD_PAD), BF16),
                 jax.ShapeDtypeStruct((B_HEADS * VT_ROWS, t_len), BF16),
                 jax.ShapeDtypeStruct((t_len, C_WIDTH), BF16)]
    out_specs = [pl.BlockSpec((A_HEADS * HEAD_PAD, tm), col),
                 pl.BlockSpec((tm, A_WIDTH), row),
                 pl.BlockSpec((A_HEADS * VT_ROWS, tm), col),
                 pl.BlockSpec((B_HEADS * HEAD_PAD, tm), col),
                 pl.BlockSpec((tm, B_HEADS * HEAD_PAD), row),
                 pl.BlockSpec((B_HEADS * VT_ROWS, tm), col),
                 pl.BlockSpec((tm, C_WIDTH), row)]
    return pl.pallas_call(
        _inproj_kernel, grid=(nt,), in_specs=in_specs, out_specs=out_specs, out_shape=out_shape,
        compiler_params=pltpu.CompilerParams(dimension_semantics=("arbitrary",), vmem_limit_bytes=VMEM_LIMIT),
        name="inproj",
    )(x2, pw['w_in'], pw['q_norm'], pw['w_q'], pw['kv_norm'], pw['w_k'], pw['w_v'],
      tabs['cq'], tabs['sq'], tabs['ck'], tabs['sk'],
      pw['sgu_g'], pw['sgu_b'], pw['ws_cat'], pw['bs_tab'], pw['mn_c'])


def _dilated_kernel(qT_ref, k_ref, vT_ref, bias_ref, o_ref, s_ref, *, s_len):
    nblk = qT_ref.shape[1] // TQ_A
    first = pl.program_id(2) * nblk

    def window(i):
        t0 = (first + i) * TQ_A
        w0 = jnp.clip(t0 - HALO, 0, s_len - WIN_A)
        b0 = w0 - t0 + 2 * HALO
        return pl.multiple_of(w0, TQ_A), pl.multiple_of(b0, TQ_A), pl.multiple_of(i * TQ_A, TQ_A)

    def scores(i, slot):
        w0, b0, q0 = window(i)
        qT = jnp.concatenate([qT_ref[hh * HEAD_PAD:(hh + 1) * HEAD_PAD, pl.ds(q0, TQ_A)] for hh in range(2)], axis=1)
        s = _dot(k_ref[pl.ds(w0, WIN_A), :], qT)
        m = []
        for hh in range(2):
            s_h = s[:, hh * TQ_A:(hh + 1) * TQ_A] + bias_ref[hh, pl.ds(b0, WIN_A), :]
            s_ref[slot, :, hh * TQ_A:(hh + 1) * TQ_A] = s_h
            m.append(jnp.max(s_h, axis=0, keepdims=True))
        return tuple(m)

    def finish(i, slot, m):
        w0, _, q0 = window(i)
        for hh in range(2):
            p = jnp.exp2(s_ref[slot, :, hh * TQ_A:(hh + 1) * TQ_A] - m[hh]).astype(BF16)
            acc = _dot(vT_ref[hh * VT_ROWS:(hh + 1) * VT_ROWS, pl.ds(w0, WIN_A)], p)
            o_ref[hh * HEAD_DIM:(hh + 1) * HEAD_DIM, pl.ds(q0, TQ_A)] = acc[:HEAD_DIM] / acc[HEAD_DIM:HEAD_DIM + 1]

    def body(i, m):
        for u in range(DIL_UNROLL):
            j = DIL_UNROLL * i + u
            m_next = scores(jnp.minimum(j + 1, nblk - 1), (u + 1) % 2)
            finish(j, u % 2, m)
            m = m_next
        return m

    lax.fori_loop(0, nblk // DIL_UNROLL, body, scores(0, 0))


def _dilated(qaT, ka, vaT, bias, batch, s_len):
    chunk = _tile(s_len, 4096)
    assert (chunk // TQ_A) % DIL_UNROLL == 0 and DIL_UNROLL % 2 == 0
    nc = s_len // chunk
    t_len = batch * s_len
    once = pl.Buffered(1)
    return pl.pallas_call(
        functools.partial(_dilated_kernel, s_len=s_len),
        grid=(batch, A_HEADS // 2, nc),
        in_specs=[pl.BlockSpec((2 * HEAD_PAD, chunk), lambda b, h, c: (h, b * nc + c)),
                  pl.BlockSpec((s_len, LANES), lambda b, h, c: (b, h), pipeline_mode=once),
                  pl.BlockSpec((2 * VT_ROWS, s_len), lambda b, h, c: (h, b), pipeline_mode=once),
                  pl.BlockSpec((2, BIAS_ROWS, TQ_A), lambda b, h, c: (h, 0, 0), pipeline_mode=once)],
        out_specs=pl.BlockSpec((2 * HEAD_DIM, chunk), lambda b, h, c: (h, b * nc + c)),
        out_shape=jax.ShapeDtypeStruct((A_WIDTH, t_len), F32),
        scratch_shapes=[pltpu.VMEM((2, WIN_A, 2 * TQ_A + LANES), F32)],
        compiler_params=pltpu.CompilerParams(dimension_semantics=("arbitrary",) * 3, vmem_limit_bytes=VMEM_LIMIT),
        name="dilated",
    )(qaT, ka, vaT, bias)


def _mla_kernel(qT_ref, k_ref, vT_ref, o_ref, acc_ref, s_ref, *, tk, nk, unroll):
    qT = qT_ref[...]
    tq = qT.shape[1]
    acc_ref[...] = jnp.zeros_like(acc_ref)

    def scores(j, slot):
        off = pl.multiple_of(j * tk, tk)
        s = _dot(k_ref[pl.ds(off, tk), :], qT)
        s_ref[slot, :, :tq] = s
        return jnp.max(s, axis=0, keepdims=True)

    def step(j, slot, m_prev, m_tile):
        m_tile_next = scores(jnp.minimum(j + 1, nk - 1), 1 - slot)
        off = pl.multiple_of(j * tk, tk)
        m_new = jnp.maximum(m_prev, m_tile)
        alpha = jnp.exp2(m_prev - m_new)
        p = jnp.exp2(s_ref[slot, :, :tq] - m_new).astype(BF16)
        acc_ref[...] = acc_ref[...] * alpha + _dot(vT_ref[:, pl.ds(off, tk)], p)
        return m_new, m_tile_next

    def body(i, carry):
        for u in range(unroll):
            carry = step(unroll * i + u, u % 2, *carry)
        return carry

    m_tile0 = scores(0, 0)
    lax.fori_loop(0, nk // unroll, body, (jnp.full((1, tq), NEG, F32), m_tile0))
    acc = acc_ref[...]
    o_ref[...] = acc[:B_VDIM] / acc[B_VDIM:B_VDIM + 1]


def _mla(qbT, kb, vbT, batch, s_len, tq, tk):
    nq = s_len // tq
    nk = s_len // tk
    unroll = min(MLA_UNROLL, nk)
    assert nk % unroll == 0 and unroll % 2 == 0
    t_len = batch * s_len
    return pl.pallas_call(
        functools.partial(_mla_kernel, tk=tk, nk=nk, unroll=unroll),
        grid=(batch, B_HEADS, nq),
        in_specs=[pl.BlockSpec((HEAD_PAD, tq), lambda b, h, q: (h, b * nq + q)),
                  pl.BlockSpec((s_len, HEAD_PAD), lambda b, h, q: (b, h)),
                  pl.BlockSpec((VT_ROWS, s_len), lambda b, h, q: (h, b))],
        out_specs=pl.BlockSpec((B_VDIM, tq), lambda b, h, q: (h, b * nq + q)),
        out_shape=jax.ShapeDtypeStruct((B_WIDTH, t_len), F32),
        scratch_shapes=[pltpu.VMEM((VT_ROWS, tq), F32), pltpu.VMEM((2, tk, tq + LANES), F32)],
        compiler_params=pltpu.CompilerParams(dimension_semantics=("arbitrary",) * 3, vmem_limit_bytes=VMEM_LIMIT),
        name="mla",
    )(qbT, kb, vbT)


def _post_kernel(oaT_ref, obT_ref, yc_ref, x_ref, wa_ref, wb_ref, wc_ref, mna_ref, mnb_ref, g1_ref, b1_ref,
                 w1_ref, bf1_ref, w2_ref, bf2_ref, g2_ref, b2_ref, o_ref):
    for r in range(0, x_ref.shape[0], POST_SUB):
        rows = slice(r, r + POST_SUB)
        ya = (_rms_scale(oaT_ref[:, rows].T) * mna_ref[...]).astype(BF16)
        yb = (_rms_scale(obT_ref[:, rows].T) * mnb_ref[...]).astype(BF16)
        y = _dot(ya, wa_ref[...]) + _dot(yb, wb_ref[...]) + _dot(yc_ref[rows, :], wc_ref[...])
        x1 = _layer_norm(ALPHA * x_ref[rows, :] + y, g1_ref[...], b1_ref[...])
        hid = _dot(x1.astype(BF16), w1_ref[...]) + bf1_ref[...]
        hid = jnp.square(jnp.maximum(hid, 0.0)).astype(BF16)
        f = _dot(hid, w2_ref[...]) + bf2_ref[...]
        o_ref[rows, :] = _layer_norm(ALPHA * x1 + f, g2_ref[...], b2_ref[...])


def _post(oaT, obT, yc, x2, pw, tm):
    t_len = x2.shape[0]
    row = lambda i: (i, 0)
    col = lambda i: (0, i)
    once = pl.Buffered(1)
    full = lambda a: pl.BlockSpec(a.shape, lambda i: (0, 0), pipeline_mode=once)
    names = ('wo_a', 'wo_b', 'wo_c', 'mn_a', 'mn_b', 'ln1_g', 'ln1_b', 'w_ff1', 'b_ff1', 'w_ff2', 'b_ff2',
             'ln2_g', 'ln2_b')
    return pl.pallas_call(
        _post_kernel, grid=(t_len // tm,),
        in_specs=[pl.BlockSpec((A_WIDTH, tm), col), pl.BlockSpec((B_WIDTH, tm), col),
                  pl.BlockSpec((tm, C_WIDTH), row), pl.BlockSpec((tm, D_MODEL), row)] + [full(pw[n]) for n in names],
        out_specs=pl.BlockSpec((tm, D_MODEL), row),
        out_shape=jax.ShapeDtypeStruct((t_len, D_MODEL), F32),
        compiler_params=pltpu.CompilerParams(dimension_semantics=("arbitrary",), vmem_limit_bytes=VMEM_LIMIT),
        name="post",
    )(oaT, obT, yc, x2, *[pw[n] for n in names])


def _swap_halves(w):
    half = w.shape[-1] // 2
    return jnp.concatenate([w[..., half:], w[..., :half]], axis=-1)


def _pack_layer(l, w_in, q_norm, w_q_up, kv_norm, w_kv_up, sgu_ln_g, sgu_ln_b, sgu_w, sgu_b, mix_norm,
                w_out, ln1_g, ln1_b, w_ff1, b_ff1, w_ff2, b_ff2, ln2_g, ln2_b):
    w = w_in[l]
    zc = lambda n: jnp.zeros((D_MODEL, n), F32)
    off_cq = 3 * A_WIDTH
    off_ckv = off_cq + Q_RANK
    off_kpe = off_ckv + KV_RANK
    off_u = off_kpe + B_ROPE
    kpe = w[:, off_kpe:off_u]
    w_packed = jnp.concatenate([w[:, :off_cq], w[:, off_ckv:off_kpe], w[:, off_cq:off_ckv], zc(64),
                                zc(64), kpe, _swap_halves(kpe), w[:, off_u:]], axis=1)
    wq = w_q_up[l].reshape(Q_RANK, B_HEADS, B_NOPE + B_ROPE)
    wq = jnp.concatenate([wq, _swap_halves(wq[..., B_NOPE:])], axis=-1).reshape(Q_RANK, B_HEADS * HEAD_PAD)
    wkv = w_kv_up[l].reshape(KV_RANK, B_HEADS, B_NOPE + B_VDIM)
    wk = jnp.concatenate([wkv[..., :B_NOPE], jnp.zeros((KV_RANK, B_HEADS, HEAD_PAD - B_NOPE), F32)], axis=-1)
    row = lambda a: a.reshape(1, -1)
    mn = mix_norm[l]
    return {
        'w_in': w_packed.astype(BF16),
        'q_norm': row(q_norm[l]), 'w_q': wq.astype(BF16),
        'kv_norm': row(kv_norm[l]), 'w_k': wk.reshape(KV_RANK, B_HEADS * HEAD_PAD).astype(BF16),
        'w_v': wkv[..., B_NOPE:].reshape(KV_RANK, B_WIDTH).astype(BF16),
        'sgu_g': row(sgu_ln_g[l]), 'sgu_b': row(sgu_ln_b[l]),
        'ws_cat': jnp.concatenate([sgu_w[l][g] for g in range(C_GROUPS)], axis=1).astype(BF16),
        'bs_tab': jnp.repeat(sgu_b[l].T, C_GROUP_DIM, axis=1),
        'mn_a': row(mn[:A_WIDTH]), 'mn_b': row(mn[A_WIDTH:A_WIDTH + B_WIDTH]), 'mn_c': row(mn[A_WIDTH + B_WIDTH:]),
        'wo_a': w_out[l][:A_WIDTH].astype(BF16), 'wo_b': w_out[l][A_WIDTH:A_WIDTH + B_WIDTH].astype(BF16),
        'wo_c': w_out[l][A_WIDTH + B_WIDTH:].astype(BF16),
        'ln1_g': row(ln1_g[l]), 'ln1_b': row(ln1_b[l]),
        'w_ff1': w_ff1[l].astype(BF16), 'b_ff1': row(b_ff1[l]), 'w_ff2': w_ff2[l].astype(BF16),
        'b_ff2': row(b_ff2[l]), 'ln2_g': row(ln2_g[l]), 'ln2_b': row(ln2_b[l]),
    }


def _rope_tables(s_len):
    inv_freq = ROPE_THETA ** (-jnp.arange(0, B_ROPE, 2, dtype=F32) / B_ROPE)
    ang = jnp.arange(s_len, dtype=F32)[:, None] * inv_freq[None, :]
    cos, sin = jnp.cos(ang), jnp.sin(ang)
    ones = jnp.ones((s_len, B_NOPE), F32)
    zeros = lambda n: jnp.zeros((s_len, n), F32)
    pad = HEAD_PAD - B_NOPE - B_ROPE
    c_tab = jnp.concatenate([cos, cos, zeros(pad)], axis=1)
    s_tab = jnp.concatenate([-sin, sin, zeros(pad)], axis=1)
    q_scale = (B_NOPE + B_ROPE) ** -0.5 * LOG2E
    return {'cq': q_scale * jnp.concatenate([ones, c_tab], axis=1),
            'sq': q_scale * jnp.concatenate([zeros(B_NOPE), s_tab], axis=1),
            'ck': jnp.concatenate([zeros(B_NOPE), c_tab], axis=1),
            'sk': jnp.concatenate([zeros(B_NOPE), s_tab], axis=1)}


def _dilated_bias():
    e = jnp.arange(BIAS_ROWS)[:, None]
    ql = jnp.arange(TQ_A)[None, :]
    dist = jnp.abs(e - 2 * HALO - ql)
    cnt = sum(((dist % d == 0) & (dist <= (w // (2 * d)) * d)).astype(F32) for w, d in DILATED_PATTERNS)
    base = jnp.where(cnt > 0, jnp.log2(jnp.maximum(cnt, 1.0)), NEG)
    slopes = jnp.asarray(2.0 ** (-8.0 * np.arange(1, A_HEADS + 1) / A_HEADS), dtype=F32)
    return base[None] - (slopes * LOG2E)[:, None, None] * dist.astype(F32)[None]


def _tile(n, pref):
    t = min(pref, n)
    assert n % t == 0
    return t


def kernel(x, w_in, q_norm, w_q_up, kv_norm, w_kv_up, sgu_ln_g, sgu_ln_b, sgu_w, sgu_b, mix_norm, w_out,
           ln1_g, ln1_b, w_ff1, b_ff1, w_ff2, b_ff2, ln2_g, ln2_b):
    batch, s_len, _ = x.shape
    assert s_len >= WIN_A and s_len % TQ_A == 0
    t_len = batch * s_len
    tabs = _rope_tables(s_len)
    bias = _dilated_bias()
    x2 = x.reshape(t_len, D_MODEL)
    for l in range(DEPTH):
        pw = _pack_layer(l, w_in, q_norm, w_q_up, kv_norm, w_kv_up, sgu_ln_g, sgu_ln_b, sgu_w, sgu_b, mix_norm,
                         w_out, ln1_g, ln1_b, w_ff1, b_ff1, w_ff2, b_ff2, ln2_g, ln2_b)
        qaT, ka, vaT, qbT, kb, vbT, yc = _inproj(x2, pw, tabs, s_len, _tile(s_len, 1024))
        oaT = _dilated(qaT, ka, vaT, bias, batch, s_len)
        obT = _mla(qbT, kb, vbT, batch, s_len, _tile(s_len, 1024), _tile(s_len, 512))
        x2 = _post(oaT, obT, yc, x2, pw, _tile(t_len, 512))
    return x2.reshape(batch, s_len, D_MODEL)
```

```python
import functools
import math

import numpy as np
import jax
import jax.numpy as jnp
from jax import lax
from jax.experimental import pallas as pl
from jax.experimental.pallas import tpu as pltpu

BF16 = jnp.bfloat16
F32 = jnp.float32

D_MODEL = 1024
DEPTH = 2
HEAD_DIM = 64
A_HEADS = 6
A_WIDTH = A_HEADS * HEAD_DIM
DILATED_PATTERNS = ((128, 1), (512, 4), (2048, 16))
B_HEADS = 6
B_NOPE = 64
B_ROPE = 32
B_VDIM = 64
Q_RANK = 192
KV_RANK = 128
B_WIDTH = B_HEADS * B_VDIM
ROPE_THETA = 10000.0
C_GROUPS = 4
C_GROUP_DIM = 64
C_WIDTH = C_GROUPS * C_GROUP_DIM
CHUNK = 128
D_FF = 4 * D_MODEL
ALPHA = (2 * DEPTH) ** 0.25
EPS = 1e-5
NEG = -1e30
LOG2E = math.log2(math.e)

LANES = 128
BF16_SUBLANES = 16
VT_ROWS = B_VDIM + BF16_SUBLANES
HEAD_PAD = LANES

P_QA, P_KA, P_VA = 0, A_WIDTH, 2 * A_WIDTH
P_CKV = 3 * A_WIDTH
P_CQ = P_CKV + KV_RANK
P_KPE = P_CQ + 256
P_U = P_KPE + LANES
P_V = P_U + C_WIDTH
P_PACKED = P_V + C_WIDTH

HALO = max((w // (2 * d)) * d for w, d in DILATED_PATTERNS)
TQ_A = 256
WIN_A = TQ_A + 2 * HALO
BIAS_ROWS = WIN_A + 2 * HALO

VMEM_LIMIT = 48 * 1024 * 1024
INPROJ_TM, INPROJ_SUB = 1024, 256
DIL_CHUNK, DIL_UNROLL = 4096, 4
MLA_TQ, MLA_TK, MLA_UNROLL = 1024, 256, 32
MERGE_TM, MERGE_SUB = 1024, 256
FFN_TM, FFN_SUB = 512, 256


def _rms_scale(x):
    return x * lax.rsqrt(jnp.mean(x * x, axis=-1, keepdims=True) + EPS)


def _layer_norm(x, g, b):
    mu = jnp.mean(x, axis=-1, keepdims=True)
    xc = x - mu
    var = jnp.mean(xc * xc, axis=-1, keepdims=True)
    return xc * lax.rsqrt(var + EPS) * g + b


def _dot(a, b):
    return jnp.dot(a, b, preferred_element_type=F32)


def _tile(n, pref):
    t = min(pref, n)
    assert n % t == 0
    return t


def _inproj_kernel(x_ref, w_ref, qn_ref, wq_ref, kvn_ref, wk_ref, wv_ref, cq_ref, sq_ref, ck_ref, sk_ref,
                   lng_ref, lnb_ref, ws_ref, bs_ref, mnc_ref,
                   qaT_ref, ka_ref, vaT_ref, qbT_ref, kb_ref, vbT_ref, yc_ref):
    sub = INPROJ_SUB
    ones_tile = (lax.broadcasted_iota(jnp.int32, (BF16_SUBLANES, sub), 0) == 0).astype(BF16)
    zeros_half = jnp.zeros((HEAD_DIM, sub), BF16)
    group = lax.broadcasted_iota(jnp.int32, (CHUNK, C_WIDTH), 1) // C_GROUP_DIM

    def put_values_t(vT, out_ref, width, rows):
        for hd in range(A_HEADS):
            out_ref[hd * VT_ROWS:hd * VT_ROWS + width, rows] = vT[hd * width:(hd + 1) * width].astype(BF16)
            out_ref[hd * VT_ROWS + width:(hd + 1) * VT_ROWS, rows] = ones_tile

    for r in range(0, x_ref.shape[0], sub):
        rows = slice(r, r + sub)
        h = _dot(x_ref[rows, :].astype(BF16), w_ref[...])

        qaT = (h[:, P_QA:P_QA + A_WIDTH] * (HEAD_DIM ** -0.5 * LOG2E)).T.astype(BF16)
        for hd in range(A_HEADS):
            q_h = qaT[hd * HEAD_DIM:(hd + 1) * HEAD_DIM]
            lo, hi = (q_h, zeros_half) if hd % 2 == 0 else (zeros_half, q_h)
            qaT_ref[hd * HEAD_PAD:hd * HEAD_PAD + HEAD_DIM, rows] = lo
            qaT_ref[hd * HEAD_PAD + HEAD_DIM:(hd + 1) * HEAD_PAD, rows] = hi
        ka_ref[rows, :] = h[:, P_KA:P_KA + A_WIDTH].astype(BF16)
        put_values_t(h[:, P_VA:P_VA + A_WIDTH].T, vaT_ref, HEAD_DIM, rows)

        c_kv = (_rms_scale(h[:, P_CKV:P_CKV + KV_RANK]) * kvn_ref[...]).astype(BF16)
        kmat = _dot(c_kv, wk_ref[...])
        vmat = _dot(c_kv, wv_ref[...])
        g_k = h[:, P_KPE:P_KPE + LANES]
        kpe = g_k * ck_ref[rows, :] + pltpu.roll(g_k, 96, 1) * sk_ref[rows, :]
        c_q = (_rms_scale(h[:, P_CQ:P_CQ + Q_RANK]) * qn_ref[...]).astype(BF16)
        qmat = _dot(c_q, wq_ref[...])
        cq_t, sq_t = cq_ref[rows, :], sq_ref[rows, :]
        for hd in range(B_HEADS):
            sl = slice(hd * HEAD_PAD, (hd + 1) * HEAD_PAD)
            kb_ref[rows, sl] = (kmat[:, sl] + kpe).astype(BF16)
            g_q = qmat[:, sl]
            q_h = g_q * cq_t + pltpu.roll(g_q, 96, 1) * sq_t
            qbT_ref[sl, rows] = q_h.T.astype(BF16)
        put_values_t(vmat.T, vbT_ref, B_VDIM, rows)

        z = jax.nn.gelu(h[:, P_U:P_U + 2 * C_WIDTH])
        u = z[:, :C_WIDTH]
        v = _layer_norm(z[:, C_WIDTH:], lng_ref[...], lnb_ref[...])
        for c in range(sub // CHUNK):
            crows = slice(c * CHUNK, (c + 1) * CHUNK)
            vc = v[crows]
            stack = jnp.concatenate([jnp.where(group == g, vc, 0.0) for g in range(C_GROUPS)], axis=0).astype(BF16)
            mixed = _dot(ws_ref[...], stack) + bs_ref[...]
            yc = u[crows] * mixed
            yc_ref[r + c * CHUNK:r + (c + 1) * CHUNK, :] = (_rms_scale(yc) * mnc_ref[...]).astype(BF16)


def _inproj(x2, pw, tabs, s_len):
    t_len = x2.shape[0]
    tm = _tile(s_len, INPROJ_TM)
    nt = t_len // tm
    ns = s_len // tm
    row = lambda i: (i, 0)
    col = lambda i: (0, i)
    fixed = lambda i: (0, 0)
    pos = lambda i: (i % ns, 0)
    full = lambda a: pl.BlockSpec(a.shape, fixed)
    tab = pl.BlockSpec((tm, LANES), pos)
    in_specs = [pl.BlockSpec((tm, D_MODEL), row), full(pw['w_in']), full(pw['q_norm']), full(pw['w_q']),
                full(pw['kv_norm']), full(pw['w_k']), full(pw['w_v']), tab, tab, tab, tab,
                full(pw['sgu_g']), full(pw['sgu_b']), full(pw['ws_cat']), full(pw['bs_tab']), full(pw['mn_c'])]
    out_shape = [jax.ShapeDtypeStruct((A_HEADS * HEAD_PAD, t_len), BF16),
                 jax.ShapeDtypeStruct((t_len, A_WIDTH), BF16),
                 jax.ShapeDtypeStruct((A_HEADS * VT_ROWS, t_len), BF16),
                 jax.ShapeDtypeStruct((B_HEADS * HEAD_PAD, t_len), BF16),
                 jax.ShapeDtypeStruct((t_len, B_HEADS * HEAD_PAD), BF16),
                 jax.ShapeDtypeStruct((B_HEADS * VT_ROWS, t_len), BF16),
                 jax.ShapeDtypeStruct((t_len, C_WIDTH), BF16)]
    out_specs = [pl.BlockSpec((A_HEADS * HEAD_PAD, tm), col),
                 pl.BlockSpec((tm, A_WIDTH), row),
                 pl.BlockSpec((A_HEADS * VT_ROWS, tm), col),
                 pl.BlockSpec((B_HEADS * HEAD_PAD, tm), col),
                 pl.BlockSpec((tm, B_HEADS * HEAD_PAD), row),
                 pl.BlockSpec((B_HEADS * VT_ROWS, tm), col),
                 pl.BlockSpec((tm, C_WIDTH), row)]
    return pl.pallas_call(
        _inproj_kernel, grid=(nt,), in_specs=in_specs, out_specs=out_specs, out_shape=out_shape,
        compiler_params=pltpu.CompilerParams(dimension_semantics=("arbitrary",), vmem_limit_bytes=VMEM_LIMIT),
        name="inproj",
    )(x2, pw['w_in'], pw['q_norm'], pw['w_q'], pw['kv_norm'], pw['w_k'], pw['w_v'],
      tabs['cq'], tabs['sq'], tabs['ck'], tabs['sk'],
      pw['sgu_g'], pw['sgu_b'], pw['ws_cat'], pw['bs_tab'], pw['mn_c'])


def _dilated_kernel(qT_ref, k_ref, vT_ref, bias_ref, o_ref, s_ref, *, s_len):
    nblk = qT_ref.shape[1] // TQ_A
    first = pl.program_id(2) * nblk

    def window(i):
        t0 = (first + i) * TQ_A
        w0 = jnp.clip(t0 - HALO, 0, s_len - WIN_A)
        b0 = w0 - t0 + 2 * HALO
        return pl.multiple_of(w0, TQ_A), pl.multiple_of(b0, TQ_A), pl.multiple_of(i * TQ_A, TQ_A)

    def scores(i, slot):
        w0, b0, q0 = window(i)
        qT = jnp.concatenate([qT_ref[hh * HEAD_PAD:(hh + 1) * HEAD_PAD, pl.ds(q0, TQ_A)] for hh in range(2)], axis=1)
        s = _dot(k_ref[pl.ds(w0, WIN_A), :], qT)
        m = []
        for hh in range(2):
            s_h = s[:, hh * TQ_A:(hh + 1) * TQ_A] + bias_ref[hh, pl.ds(b0, WIN_A), :]
            s_ref[slot, :, hh * TQ_A:(hh + 1) * TQ_A] = s_h
            m.append(jnp.max(s_h, axis=0, keepdims=True))
        return tuple(m)

    def finish(i, slot, m):
        w0, _, q0 = window(i)
        for hh in range(2):
            p = jnp.exp2(s_ref[slot, :, hh * TQ_A:(hh + 1) * TQ_A] - m[hh]).astype(BF16)
            acc = _dot(vT_ref[hh * VT_ROWS:(hh + 1) * VT_ROWS, pl.ds(w0, WIN_A)], p)
            o_ref[hh * HEAD_DIM:(hh + 1) * HEAD_DIM, pl.ds(q0, TQ_A)] = acc[:HEAD_DIM] / acc[HEAD_DIM:HEAD_DIM + 1]

    def body(i, m):
        for u in range(DIL_UNROLL):
            j = DIL_UNROLL * i + u
            m_next = scores(jnp.minimum(j + 1, nblk - 1), (u + 1) % 2)
            finish(j, u % 2, m)
            m = m_next
        return m

    lax.fori_loop(0, nblk // DIL_UNROLL, body, scores(0, 0))


def _dilated(qaT, ka, vaT, bias, batch, s_len):
    chunk = _tile(s_len, DIL_CHUNK)
    assert (chunk // TQ_A) % DIL_UNROLL == 0 and DIL_UNROLL % 2 == 0
    nc = s_len // chunk
    t_len = batch * s_len
    once = pl.Buffered(1)
    return pl.pallas_call(
        functools.partial(_dilated_kernel, s_len=s_len),
        grid=(batch, A_HEADS // 2, nc),
        in_specs=[pl.BlockSpec((2 * HEAD_PAD, chunk), lambda b, h, c: (h, b * nc + c)),
                  pl.BlockSpec((s_len, LANES), lambda b, h, c: (b, h), pipeline_mode=once),
                  pl.BlockSpec((2 * VT_ROWS, s_len), lambda b, h, c: (h, b), pipeline_mode=once),
                  pl.BlockSpec((2, BIAS_ROWS, TQ_A), lambda b, h, c: (h, 0, 0), pipeline_mode=once)],
        out_specs=pl.BlockSpec((2 * HEAD_DIM, chunk), lambda b, h, c: (h, b * nc + c)),
        out_shape=jax.ShapeDtypeStruct((A_WIDTH, t_len), F32),
        scratch_shapes=[pltpu.VMEM((2, WIN_A, 2 * TQ_A + LANES), F32)],
        compiler_params=pltpu.CompilerParams(dimension_semantics=("arbitrary",) * 3, vmem_limit_bytes=VMEM_LIMIT),
        name="dilated",
    )(qaT, ka, vaT, bias)


def _mla_kernel(qT_ref, k_ref, vT_ref, o_ref, acc_ref, s_ref, *, tq, tk, unroll):
    s_len = k_ref.shape[0]
    nk = s_len // tk

    def query_tile(qi, _):
        q0 = pl.multiple_of(qi * tq, tq)
        qT = qT_ref[:, pl.ds(q0, tq)]
        acc_ref[...] = jnp.zeros_like(acc_ref)

        def scores(j, slot):
            off = pl.multiple_of(j * tk, tk)
            s = _dot(k_ref[pl.ds(off, tk), :], qT)
            s_ref[slot, :, :tq] = s
            return jnp.max(s, axis=0, keepdims=True)

        def step(j, slot, m_prev, m_tile):
            m_tile_next = scores(jnp.minimum(j + 1, nk - 1), 1 - slot)
            off = pl.multiple_of(j * tk, tk)
            m_new = jnp.maximum(m_prev, m_tile)
            alpha = jnp.exp2(m_prev - m_new)
            p = jnp.exp2(s_ref[slot, :, :tq] - m_new).astype(BF16)
            acc_ref[...] = acc_ref[...] * alpha + _dot(vT_ref[:, pl.ds(off, tk)], p)
            return m_new, m_tile_next

        def body(i, carry):
            for u in range(unroll):
                carry = step(unroll * i + u, u % 2, *carry)
            return carry

        lax.fori_loop(0, nk // unroll, body, (jnp.full((1, tq), NEG, F32), scores(0, 0)))
        acc = acc_ref[...]
        o_ref[:, pl.ds(q0, tq)] = acc[:B_VDIM] / acc[B_VDIM:B_VDIM + 1]
        return 0

    lax.fori_loop(0, s_len // tq, query_tile, 0)


def _mla(qbT, kb, vbT, batch, s_len):
    tq, tk = _tile(s_len, MLA_TQ), _tile(s_len, MLA_TK)
    nk = s_len // tk
    unroll = min(MLA_UNROLL, nk)
    assert nk % unroll == 0 and unroll % 2 == 0
    t_len = batch * s_len
    return pl.pallas_call(
        functools.partial(_mla_kernel, tq=tq, tk=tk, unroll=unroll),
        grid=(batch, B_HEADS),
        in_specs=[pl.BlockSpec((HEAD_PAD, s_len), lambda b, h: (h, b)),
                  pl.BlockSpec((s_len, HEAD_PAD), lambda b, h: (b, h)),
                  pl.BlockSpec((VT_ROWS, s_len), lambda b, h: (h, b))],
        out_specs=pl.BlockSpec((B_VDIM, s_len), lambda b, h: (h, b)),
        out_shape=jax.ShapeDtypeStruct((B_WIDTH, t_len), F32),
        scratch_shapes=[pltpu.VMEM((VT_ROWS, tq), F32), pltpu.VMEM((2, tk, tq + LANES), F32)],
        compiler_params=pltpu.CompilerParams(dimension_semantics=("arbitrary",) * 2, vmem_limit_bytes=VMEM_LIMIT),
        name="mla",
    )(qbT, kb, vbT)


def _merge_kernel(oaT_ref, obT_ref, yc_ref, x_ref, wa_ref, wb_ref, wc_ref, mna_ref, mnb_ref, g_ref, b_ref, o_ref):
    for r in range(0, x_ref.shape[0], MERGE_SUB):
        rows = slice(r, r + MERGE_SUB)
        ya = (_rms_scale(oaT_ref[:, rows].T) * mna_ref[...]).astype(BF16)
        yb = (_rms_scale(obT_ref[:, rows].T) * mnb_ref[...]).astype(BF16)
        y = _dot(ya, wa_ref[...]) + _dot(yb, wb_ref[...]) + _dot(yc_ref[rows, :], wc_ref[...])
        o_ref[rows, :] = _layer_norm(ALPHA * x_ref[rows, :] + y, g_ref[...], b_ref[...])


def _merge(oaT, obT, yc, x2, pw):
    t_len = x2.shape[0]
    tm = _tile(t_len, MERGE_TM)
    row = lambda i: (i, 0)
    col = lambda i: (0, i)
    full = lambda a: pl.BlockSpec(a.shape, lambda i: (0, 0))
    return pl.pallas_call(
        _merge_kernel, grid=(t_len // tm,),
        in_specs=[pl.BlockSpec((A_WIDTH, tm), col), pl.BlockSpec((B_WIDTH, tm), col),
                  pl.BlockSpec((tm, C_WIDTH), row), pl.BlockSpec((tm, D_MODEL), row),
                  full(pw['wo_a']), full(pw['wo_b']), full(pw['wo_c']), full(pw['mn_a']), full(pw['mn_b']),
                  full(pw['ln1_g']), full(pw['ln1_b'])],
        out_specs=pl.BlockSpec((tm, D_MODEL), row),
        out_shape=jax.ShapeDtypeStruct((t_len, D_MODEL), F32),
        compiler_params=pltpu.CompilerParams(dimension_semantics=("arbitrary",), vmem_limit_bytes=VMEM_LIMIT),
        name="merge",
    )(oaT, obT, yc, x2, pw['wo_a'], pw['wo_b'], pw['wo_c'], pw['mn_a'], pw['mn_b'], pw['ln1_g'], pw['ln1_b'])


def _ffn_kernel(x_ref, w1_ref, b1_ref, w2_ref, b2_ref, g_ref, b_ref, o_ref):
    for r in range(0, x_ref.shape[0], FFN_SUB):
        rows = slice(r, r + FFN_SUB)
        x = x_ref[rows, :]
        hid = _dot(x.astype(BF16), w1_ref[...]) + b1_ref[...]
        hid = jnp.square(jnp.maximum(hid, 0.0)).astype(BF16)
        f = _dot(hid, w2_ref[...]) + b2_ref[...]
        o_ref[rows, :] = _layer_norm(ALPHA * x + f, g_ref[...], b_ref[...])


def _ffn(x2, pw):
    t_len = x2.shape[0]
    tm = _tile(t_len, FFN_TM)
    fixed = lambda i: (0, 0)
    once = pl.Buffered(1)
    return pl.pallas_call(
        _ffn_kernel, grid=(t_len // tm,),
        in_specs=[pl.BlockSpec((tm, D_MODEL), lambda i: (i, 0)),
                  pl.BlockSpec((D_MODEL, D_FF), fixed, pipeline_mode=once),
                  pl.BlockSpec((1, D_FF), fixed),
                  pl.BlockSpec((D_FF, D_MODEL), fixed, pipeline_mode=once),
                  pl.BlockSpec((1, D_MODEL), fixed), pl.BlockSpec((1, D_MODEL), fixed),
                  pl.BlockSpec((1, D_MODEL), fixed)],
        out_specs=pl.BlockSpec((tm, D_MODEL), lambda i: (i, 0)),
        out_shape=jax.ShapeDtypeStruct((t_len, D_MODEL), F32),
        compiler_params=pltpu.CompilerParams(dimension_semantics=("arbitrary",), vmem_limit_bytes=VMEM_LIMIT),
        name="ffn",
    )(x2, pw['w_ff1'], pw['b_ff1'], pw['w_ff2'], pw['b_ff2'], pw['ln2_g'], pw['ln2_b'])


def _swap_halves(w):
    half = w.shape[-1] // 2
    return jnp.concatenate([w[..., half:], w[..., :half]], axis=-1)


def _pack_layer(l, w_in, q_norm, w_q_up, kv_norm, w_kv_up, sgu_ln_g, sgu_ln_b, sgu_w, sgu_b, mix_norm,
                w_out, ln1_g, ln1_b, w_ff1, b_ff1, w_ff2, b_ff2, ln2_g, ln2_b):
    w = w_in[l]
    zc = lambda n: jnp.zeros((D_MODEL, n), F32)
    off_cq = 3 * A_WIDTH
    off_ckv = off_cq + Q_RANK
    off_kpe = off_ckv + KV_RANK
    off_u = off_kpe + B_ROPE
    kpe = w[:, off_kpe:off_u]
    w_packed = jnp.concatenate([w[:, :off_cq], w[:, off_ckv:off_kpe], w[:, off_cq:off_ckv], zc(64),
                                zc(64), kpe, _swap_halves(kpe), w[:, off_u:]], axis=1)
    wq = w_q_up[l].reshape(Q_RANK, B_HEADS, B_NOPE + B_ROPE)
    wq = jnp.concatenate([wq, _swap_halves(wq[..., B_NOPE:])], axis=-1).reshape(Q_RANK, B_HEADS * HEAD_PAD)
    wkv = w_kv_up[l].reshape(KV_RANK, B_HEADS, B_NOPE + B_VDIM)
    wk = jnp.concatenate([wkv[..., :B_NOPE], jnp.zeros((KV_RANK, B_HEADS, HEAD_PAD - B_NOPE), F32)], axis=-1)
    row = lambda a: a.reshape(1, -1)
    mn = mix_norm[l]
    return {
        'w_in': w_packed.astype(BF16),
        'q_norm': row(q_norm[l]), 'w_q': wq.astype(BF16),
        'kv_norm': row(kv_norm[l]), 'w_k': wk.reshape(KV_RANK, B_HEADS * HEAD_PAD).astype(BF16),
        'w_v': wkv[..., B_NOPE:].reshape(KV_RANK, B_WIDTH).astype(BF16),
        'sgu_g': row(sgu_ln_g[l]), 'sgu_b': row(sgu_ln_b[l]),
        'ws_cat': jnp.concatenate([sgu_w[l][g] for g in range(C_GROUPS)], axis=1).astype(BF16),
        'bs_tab': jnp.repeat(sgu_b[l].T, C_GROUP_DIM, axis=1),
        'mn_a': row(mn[:A_WIDTH]), 'mn_b': row(mn[A_WIDTH:A_WIDTH + B_WIDTH]), 'mn_c': row(mn[A_WIDTH + B_WIDTH:]),
        'wo_a': w_out[l][:A_WIDTH].astype(BF16), 'wo_b': w_out[l][A_WIDTH:A_WIDTH + B_WIDTH].astype(BF16),
        'wo_c': w_out[l][A_WIDTH + B_WIDTH:].astype(BF16),
        'ln1_g': row(ln1_g[l]), 'ln1_b': row(ln1_b[l]),
        'w_ff1': w_ff1[l].astype(BF16), 'b_ff1': row(b_ff1[l]), 'w_ff2': w_ff2[l].astype(BF16),
        'b_ff2': row(b_ff2[l]), 'ln2_g': row(ln2_g[l]), 'ln2_b': row(ln2_b[l]),
    }


def _rope_tables(s_len):
    inv_freq = ROPE_THETA ** (-jnp.arange(0, B_ROPE, 2, dtype=F32) / B_ROPE)
    ang = jnp.arange(s_len, dtype=F32)[:, None] * inv_freq[None, :]
    cos, sin = jnp.cos(ang), jnp.sin(ang)
    ones = jnp.ones((s_len, B_NOPE), F32)
    zeros = lambda n: jnp.zeros((s_len, n), F32)
    pad = HEAD_PAD - B_NOPE - B_ROPE
    c_tab = jnp.concatenate([cos, cos, zeros(pad)], axis=1)
    s_tab = jnp.concatenate([-sin, sin, zeros(pad)], axis=1)
    q_scale = (B_NOPE + B_ROPE) ** -0.5 * LOG2E
    return {'cq': q_scale * jnp.concatenate([ones, c_tab], axis=1),
            'sq': q_scale * jnp.concatenate([zeros(B_NOPE), s_tab], axis=1),
            'ck': jnp.concatenate([zeros(B_NOPE), c_tab], axis=1),
            'sk': jnp.concatenate([zeros(B_NOPE), s_tab], axis=1)}


def _dilated_bias():
    e = jnp.arange(BIAS_ROWS)[:, None]
    ql = jnp.arange(TQ_A)[None, :]
    dist = jnp.abs(e - 2 * HALO - ql)
    cnt = sum(((dist % d == 0) & (dist <= (w // (2 * d)) * d)).astype(F32) for w, d in DILATED_PATTERNS)
    base = jnp.where(cnt > 0, jnp.log2(jnp.maximum(cnt, 1.0)), NEG)
    slopes = jnp.asarray(2.0 ** (-8.0 * np.arange(1, A_HEADS + 1) / A_HEADS), dtype=F32)
    return base[None] - (slopes * LOG2E)[:, None, None] * dist.astype(F32)[None]


def kernel(x, w_in, q_norm, w_q_up, kv_norm, w_kv_up, sgu_ln_g, sgu_ln_b, sgu_w, sgu_b, mix_norm, w_out,
           ln1_g, ln1_b, w_ff1, b_ff1, w_ff2, b_ff2, ln2_g, ln2_b):
    batch, s_len, _ = x.shape
    assert s_len >= WIN_A and s_len % TQ_A == 0
    t_len = batch * s_len
    tabs = _rope_tables(s_len)
    bias = _dilated_bias()
    x2 = x.reshape(t_len, D_MODEL)
    for l in range(DEPTH):
        pw = _pack_layer(l, w_in, q_norm, w_q_up, kv_norm, w_kv_up, sgu_ln_g, sgu_ln_b, sgu_w, sgu_b, mix_norm,
                         w_out, ln1_g, ln1_b, w_ff1, b_ff1, w_ff2, b_ff2, ln2_g, ln2_b)
        qaT, ka, vaT, qbT, kb, vbT, yc = _inproj(x2, pw, tabs, s_len)
        oaT = _dilated(qaT, ka, vaT, bias, batch, s_len)
        obT = _mla(qbT, kb, vbT, batch, s_len)
        x1 = _merge(oaT, obT, yc, x2, pw)
        x2 = _ffn(x1, pw)
    return x2.reshape(batch, s_len, D_MODEL)
```

```python
import functools
import math

import numpy as np
import jax
import jax.numpy as jnp
from jax import lax
from jax.experimental import pallas as pl
from jax.experimental.pallas import tpu as pltpu

BF16 = jnp.bfloat16
F32 = jnp.float32

D_MODEL = 1024
DEPTH = 2
HEAD_DIM = 64
A_HEADS = 6
A_WIDTH = A_HEADS * HEAD_DIM
DILATED_PATTERNS = ((128, 1), (512, 4), (2048, 16))
B_HEADS = 6
B_NOPE = 64
B_ROPE = 32
B_VDIM = 64
Q_RANK = 192
KV_RANK = 128
B_WIDTH = B_HEADS * B_VDIM
ROPE_THETA = 10000.0
C_GROUPS = 4
C_GROUP_DIM = 64
C_WIDTH = C_GROUPS * C_GROUP_DIM
CHUNK = 128
D_FF = 4 * D_MODEL
ALPHA = (2 * DEPTH) ** 0.25
EPS = 1e-5
NEG = -1e30
LOG2E = math.log2(math.e)

LANES = 128
BF16_SUBLANES = 16
VT_ROWS = B_VDIM + BF16_SUBLANES
HEAD_PAD = LANES

P_QA, P_KA, P_VA = 0, A_WIDTH, 2 * A_WIDTH
P_CKV = 3 * A_WIDTH
P_CQ = P_CKV + KV_RANK
P_KPE = P_CQ + 256
P_U = P_KPE + LANES
P_V = P_U + C_WIDTH
P_PACKED = P_V + C_WIDTH

HALO = max((w // (2 * d)) * d for w, d in DILATED_PATTERNS)
TQ_A = 256
WIN_A = TQ_A + 2 * HALO
BIAS_ROWS = WIN_A + 2 * HALO

VMEM_LIMIT = 48 * 1024 * 1024
INPROJ_TM, INPROJ_SUB = 1024, 256
DIL_CHUNK, DIL_UNROLL = 4096, 4
MLA_TQ, MLA_TK, MLA_UNROLL = 1024, 256, 32
MERGE_TM, MERGE_SUB = 1024, 256
FFN_TM, FFN_SUB = 1024, 256


def _rms_scale(x):
    return x * lax.rsqrt(jnp.mean(x * x, axis=-1, keepdims=True) + EPS)


def _layer_norm(x, g, b):
    mu = jnp.mean(x, axis=-1, keepdims=True)
    xc = x - mu
    var = jnp.mean(xc * xc, axis=-1, keepdims=True)
    return xc * lax.rsqrt(var + EPS) * g + b


def _dot(a, b):
    return jnp.dot(a, b, preferred_element_type=F32)


def _tile(n, pref):
    t = min(pref, n)
    assert n % t == 0
    return t


def _inproj_kernel(x_ref, w_ref, qn_ref, wq_ref, kvn_ref, wk_ref, wv_ref, cq_ref, sq_ref, ck_ref, sk_ref,
                   lng_ref, lnb_ref, ws_ref, bs_ref, mnc_ref,
                   qaT_ref, ka_ref, vaT_ref, qbT_ref, kb_ref, vbT_ref, yc_ref):
    sub = INPROJ_SUB
    ones_tile = (lax.broadcasted_iota(jnp.int32, (BF16_SUBLANES, sub), 0) == 0).astype(BF16)
    zeros_half = jnp.zeros((HEAD_DIM, sub), BF16)
    group = lax.broadcasted_iota(jnp.int32, (CHUNK, C_WIDTH), 1) // C_GROUP_DIM

    def put_values_t(vT, out_ref, width, rows):
        for hd in range(A_HEADS):
            out_ref[hd * VT_ROWS:hd * VT_ROWS + width, rows] = vT[hd * width:(hd + 1) * width].astype(BF16)
            out_ref[hd * VT_ROWS + width:(hd + 1) * VT_ROWS, rows] = ones_tile

    for r in range(0, x_ref.shape[0], sub):
        rows = slice(r, r + sub)
        h = _dot(x_ref[rows, :].astype(BF16), w_ref[...])

        qaT = (h[:, P_QA:P_QA + A_WIDTH] * (HEAD_DIM ** -0.5 * LOG2E)).T.astype(BF16)
        for hd in range(A_HEADS):
            q_h = qaT[hd * HEAD_DIM:(hd + 1) * HEAD_DIM]
            lo, hi = (q_h, zeros_half) if hd % 2 == 0 else (zeros_half, q_h)
            qaT_ref[hd * HEAD_PAD:hd * HEAD_PAD + HEAD_DIM, rows] = lo
            qaT_ref[hd * HEAD_PAD + HEAD_DIM:(hd + 1) * HEAD_PAD, rows] = hi
        ka_ref[rows, :] = h[:, P_KA:P_KA + A_WIDTH].astype(BF16)
        put_values_t(h[:, P_VA:P_VA + A_WIDTH].T, vaT_ref, HEAD_DIM, rows)

        c_kv = (_rms_scale(h[:, P_CKV:P_CKV + KV_RANK]) * kvn_ref[...]).astype(BF16)
        kmat = _dot(c_kv, wk_ref[...])
        vmat = _dot(c_kv, wv_ref[...])
        g_k = h[:, P_KPE:P_KPE + LANES]
        kpe = g_k * ck_ref[rows, :] + pltpu.roll(g_k, 96, 1) * sk_ref[rows, :]
        c_q = (_rms_scale(h[:, P_CQ:P_CQ + Q_RANK]) * qn_ref[...]).astype(BF16)
        qmat = _dot(c_q, wq_ref[...])
        cq_t, sq_t = cq_ref[rows, :], sq_ref[rows, :]
        for hd in range(B_HEADS):
            sl = slice(hd * HEAD_PAD, (hd + 1) * HEAD_PAD)
            kb_ref[rows, sl] = (kmat[:, sl] + kpe).astype(BF16)
            g_q = qmat[:, sl]
            q_h = g_q * cq_t + pltpu.roll(g_q, 96, 1) * sq_t
            qbT_ref[sl, rows] = q_h.T.astype(BF16)
        put_values_t(vmat.T, vbT_ref, B_VDIM, rows)

        z = jax.nn.gelu(h[:, P_U:P_U + 2 * C_WIDTH])
        u = z[:, :C_WIDTH]
        v = _layer_norm(z[:, C_WIDTH:], lng_ref[...], lnb_ref[...])
        for c in range(sub // CHUNK):
            crows = slice(c * CHUNK, (c + 1) * CHUNK)
            vc = v[crows]
            stack = jnp.concatenate([jnp.where(group == g, vc, 0.0) for g in range(C_GROUPS)], axis=0).astype(BF16)
            mixed = _dot(ws_ref[...], stack) + bs_ref[...]
            yc = u[crows] * mixed
            yc_ref[r + c * CHUNK:r + (c + 1) * CHUNK, :] = (_rms_scale(yc) * mnc_ref[...]).astype(BF16)


def _inproj(x2, pw, tabs, s_len):
    t_len = x2.shape[0]
    tm = _tile(s_len, INPROJ_TM)
    nt = t_len // tm
    ns = s_len // tm
    row = lambda i: (i, 0)
    col = lambda i: (0, i)
    fixed = lambda i: (0, 0)
    pos = lambda i: (i % ns, 0)
    full = lambda a: pl.BlockSpec(a.shape, fixed)
    tab = pl.BlockSpec((tm, LANES), pos)
    in_specs = [pl.BlockSpec((tm, D_MODEL), row), full(pw['w_in']), full(pw['q_norm']), full(pw['w_q']),
                full(pw['kv_norm']), full(pw['w_k']), full(pw['w_v']), tab, tab, tab, tab,
                full(pw['sgu_g']), full(pw['sgu_b']), full(pw['ws_cat']), full(pw['bs_tab']), full(pw['mn_c'])]
    out_shape = [jax.ShapeDtypeStruct((A_HEADS * HEAD_PAD, t_len), BF16),
                 jax.ShapeDtypeStruct((t_len, A_WIDTH), BF16),
                 jax.ShapeDtypeStruct((A_HEADS * VT_ROWS, t_len), BF16),
                 jax.ShapeDtypeStruct((B_HEADS * HEAD_PAD, t_len), BF16),
                 jax.ShapeDtypeStruct((t_len, B_HEADS * HEAD_PAD), BF16),
                 jax.ShapeDtypeStruct((B_HEADS * VT_ROWS, t_len), BF16),
                 jax.ShapeDtypeStruct((t_len, C_WIDTH), BF16)]
    out_specs = [pl.BlockSpec((A_HEADS * HEAD_PAD, tm), col),
                 pl.BlockSpec((tm, A_WIDTH), row),
                 pl.BlockSpec((A_HEADS * VT_ROWS, tm), col),
                 pl.BlockSpec((B_HEADS * HEAD_PAD, tm), col),
                 pl.BlockSpec((tm, B_HEADS * HEAD_PAD), row),
                 pl.BlockSpec((B_HEADS * VT_ROWS, tm), col),
                 pl.BlockSpec((tm, C_WIDTH), row)]
    return pl.pallas_call(
        _inproj_kernel, grid=(nt,), in_specs=in_specs, out_specs=out_specs, out_shape=out_shape,
        compiler_params=pltpu.CompilerParams(dimension_semantics=("arbitrary",), vmem_limit_bytes=VMEM_LIMIT),
        name="inproj",
    )(x2, pw['w_in'], pw['q_norm'], pw['w_q'], pw['kv_norm'], pw['w_k'], pw['w_v'],
      tabs['cq'], tabs['sq'], tabs['ck'], tabs['sk'],
      pw['sgu_g'], pw['sgu_b'], pw['ws_cat'], pw['bs_tab'], pw['mn_c'])


def _dilated_kernel(qT_ref, k_ref, vT_ref, bias_ref, o_ref, s_ref, *, s_len):
    nblk = qT_ref.shape[1] // TQ_A
    first = pl.program_id(2) * nblk

    def window(i):
        t0 = (first + i) * TQ_A
        w0 = jnp.clip(t0 - HALO, 0, s_len - WIN_A)
        b0 = w0 - t0 + 2 * HALO
        return pl.multiple_of(w0, TQ_A), pl.multiple_of(b0, TQ_A), pl.multiple_of(i * TQ_A, TQ_A)

    def scores(i, slot):
        w0, b0, q0 = window(i)
        qT = jnp.concatenate([qT_ref[hh * HEAD_PAD:(hh + 1) * HEAD_PAD, pl.ds(q0, TQ_A)] for hh in range(2)], axis=1)
        s = _dot(k_ref[pl.ds(w0, WIN_A), :], qT)
        m = []
        for hh in range(2):
            s_h = s[:, hh * TQ_A:(hh + 1) * TQ_A] + bias_ref[hh, pl.ds(b0, WIN_A), :]
            s_ref[slot, :, hh * TQ_A:(hh + 1) * TQ_A] = s_h
            m.append(jnp.max(s_h, axis=0, keepdims=True))
        return tuple(m)

    def finish(i, slot, m):
        w0, _, q0 = window(i)
        for hh in range(2):
            p = jnp.exp2(s_ref[slot, :, hh * TQ_A:(hh + 1) * TQ_A] - m[hh]).astype(BF16)
            acc = _dot(vT_ref[hh * VT_ROWS:(hh + 1) * VT_ROWS, pl.ds(w0, WIN_A)], p)
            o_ref[hh * HEAD_DIM:(hh + 1) * HEAD_DIM, pl.ds(q0, TQ_A)] = acc[:HEAD_DIM] / acc[HEAD_DIM:HEAD_DIM + 1]

    def body(i, m):
        for u in range(DIL_UNROLL):
            j = DIL_UNROLL * i + u
            m_next = scores(jnp.minimum(j + 1, nblk - 1), (u + 1) % 2)
            finish(j, u % 2, m)
            m = m_next
        return m

    lax.fori_loop(0, nblk // DIL_UNROLL, body, scores(0, 0))


def _dilated(qaT, ka, vaT, bias, batch, s_len):
    chunk = _tile(s_len, DIL_CHUNK)
    assert (chunk // TQ_A) % DIL_UNROLL == 0 and DIL_UNROLL % 2 == 0
    nc = s_len // chunk
    t_len = batch * s_len
    once = pl.Buffered(1)
    return pl.pallas_call(
        functools.partial(_dilated_kernel, s_len=s_len),
        grid=(batch, A_HEADS // 2, nc),
        in_specs=[pl.BlockSpec((2 * HEAD_PAD, chunk), lambda b, h, c: (h, b * nc + c)),
                  pl.BlockSpec((s_len, LANES), lambda b, h, c: (b, h), pipeline_mode=once),
                  pl.BlockSpec((2 * VT_ROWS, s_len), lambda b, h, c: (h, b), pipeline_mode=once),
                  pl.BlockSpec((2, BIAS_ROWS, TQ_A), lambda b, h, c: (h, 0, 0), pipeline_mode=once)],
        out_specs=pl.BlockSpec((2 * HEAD_DIM, chunk), lambda b, h, c: (h, b * nc + c)),
        out_shape=jax.ShapeDtypeStruct((A_WIDTH, t_len), F32),
        scratch_shapes=[pltpu.VMEM((2, WIN_A, 2 * TQ_A + LANES), F32)],
        compiler_params=pltpu.CompilerParams(dimension_semantics=("arbitrary",) * 3, vmem_limit_bytes=VMEM_LIMIT),
        name="dilated",
    )(qaT, ka, vaT, bias)


def _mla_kernel(qT_ref, k_ref, vT_ref, o_ref, acc_ref, s_ref, *, tq, tk, unroll):
    s_len = k_ref.shape[0]
    nk = s_len // tk

    def query_tile(qi, _):
        q0 = pl.multiple_of(qi * tq, tq)
        qT = qT_ref[:, pl.ds(q0, tq)]
        acc_ref[...] = jnp.zeros_like(acc_ref)

        def scores(j, slot):
            off = pl.multiple_of(j * tk, tk)
            s = _dot(k_ref[pl.ds(off, tk), :], qT)
            s_ref[slot, :, :tq] = s
            return jnp.max(s, axis=0, keepdims=True)

        def step(j, slot, m_prev, m_tile):
            m_tile_next = scores(jnp.minimum(j + 1, nk - 1), 1 - slot)
            off = pl.multiple_of(j * tk, tk)
            m_new = jnp.maximum(m_prev, m_tile)
            alpha = jnp.exp2(m_prev - m_new)
            p = jnp.exp2(s_ref[slot, :, :tq] - m_new).astype(BF16)
            acc_ref[...] = acc_ref[...] * alpha + _dot(vT_ref[:, pl.ds(off, tk)], p)
            return m_new, m_tile_next

        def body(i, carry):
            for u in range(unroll):
                carry = step(unroll * i + u, u % 2, *carry)
            return carry

        lax.fori_loop(0, nk // unroll, body, (jnp.full((1, tq), NEG, F32), scores(0, 0)))
        acc = acc_ref[...]
        o_ref[:, pl.ds(q0, tq)] = acc[:B_VDIM] / acc[B_VDIM:B_VDIM + 1]
        return 0

    lax.fori_loop(0, s_len // tq, query_tile, 0)


def _mla(qbT, kb, vbT, batch, s_len):
    tq, tk = _tile(s_len, MLA_TQ), _tile(s_len, MLA_TK)
    nk = s_len // tk
    unroll = min(MLA_UNROLL, nk)
    assert nk % unroll == 0 and unroll % 2 == 0
    t_len = batch * s_len
    return pl.pallas_call(
        functools.partial(_mla_kernel, tq=tq, tk=tk, unroll=unroll),
        grid=(batch, B_HEADS),
        in_specs=[pl.BlockSpec((HEAD_PAD, s_len), lambda b, h: (h, b)),
                  pl.BlockSpec((s_len, HEAD_PAD), lambda b, h: (b, h)),
                  pl.BlockSpec((VT_ROWS, s_len), lambda b, h: (h, b))],
        out_specs=pl.BlockSpec((B_VDIM, s_len), lambda b, h: (h, b)),
        out_shape=jax.ShapeDtypeStruct((B_WIDTH, t_len), F32),
        scratch_shapes=[pltpu.VMEM((VT_ROWS, tq), F32), pltpu.VMEM((2, tk, tq + LANES), F32)],
        compiler_params=pltpu.CompilerParams(dimension_semantics=("arbitrary",) * 2, vmem_limit_bytes=VMEM_LIMIT),
        name="mla",
    )(qbT, kb, vbT)


def _merge_kernel(oaT_ref, obT_ref, yc_ref, x_ref, wo_ref, mna_ref, mnb_ref, g_ref, b_ref, o_ref):
    for r in range(0, x_ref.shape[0], MERGE_SUB):
        rows = slice(r, r + MERGE_SUB)
        ya = (_rms_scale(oaT_ref[:, rows].T) * mna_ref[...]).astype(BF16)
        yb = (_rms_scale(obT_ref[:, rows].T) * mnb_ref[...]).astype(BF16)
        y = _dot(jnp.concatenate([ya, yb, yc_ref[rows, :]], axis=1), wo_ref[...])
        o_ref[rows, :] = _layer_norm(ALPHA * x_ref[rows, :] + y, g_ref[...], b_ref[...])


def _merge(oaT, obT, yc, x2, pw):
    t_len = x2.shape[0]
    tm = _tile(t_len, MERGE_TM)
    row = lambda i: (i, 0)
    col = lambda i: (0, i)
    full = lambda a: pl.BlockSpec(a.shape, lambda i: (0, 0))
    return pl.pallas_call(
        _merge_kernel, grid=(t_len // tm,),
        in_specs=[pl.BlockSpec((A_WIDTH, tm), col), pl.BlockSpec((B_WIDTH, tm), col),
                  pl.BlockSpec((tm, C_WIDTH), row), pl.BlockSpec((tm, D_MODEL), row),
                  full(pw['w_out']), full(pw['mn_a']), full(pw['mn_b']), full(pw['ln1_g']), full(pw['ln1_b'])],
        out_specs=pl.BlockSpec((tm, D_MODEL), row),
        out_shape=jax.ShapeDtypeStruct((t_len, D_MODEL), F32),
        compiler_params=pltpu.CompilerParams(dimension_semantics=("arbitrary",), vmem_limit_bytes=VMEM_LIMIT),
        name="merge",
    )(oaT, obT, yc, x2, pw['w_out'], pw['mn_a'], pw['mn_b'], pw['ln1_g'], pw['ln1_b'])


def _ffn_kernel(x_ref, w1_ref, b1_ref, w2_ref, b2_ref, g_ref, b_ref, o_ref):
    for r in range(0, x_ref.shape[0], FFN_SUB):
        rows = slice(r, r + FFN_SUB)
        x = x_ref[rows, :]
        hid = _dot(x.astype(BF16), w1_ref[...]) + b1_ref[...]
        hid = jnp.square(jnp.maximum(hid, 0.0)).astype(BF16)
        f = _dot(hid, w2_ref[...]) + b2_ref[...]
        o_ref[rows, :] = _layer_norm(ALPHA * x + f, g_ref[...], b_ref[...])


def _ffn(x2, pw):
    t_len = x2.shape[0]
    tm = _tile(t_len, FFN_TM)
    fixed = lambda i: (0, 0)
    once = pl.Buffered(1)
    return pl.pallas_call(
        _ffn_kernel, grid=(t_len // tm,),
        in_specs=[pl.BlockSpec((tm, D_MODEL), lambda i: (i, 0)),
                  pl.BlockSpec((D_MODEL, D_FF), fixed, pipeline_mode=once),
                  pl.BlockSpec((1, D_FF), fixed),
                  pl.BlockSpec((D_FF, D_MODEL), fixed, pipeline_mode=once),
                  pl.BlockSpec((1, D_MODEL), fixed), pl.BlockSpec((1, D_MODEL), fixed),
                  pl.BlockSpec((1, D_MODEL), fixed)],
        out_specs=pl.BlockSpec((tm, D_MODEL), lambda i: (i, 0)),
        out_shape=jax.ShapeDtypeStruct((t_len, D_MODEL), F32),
        compiler_params=pltpu.CompilerParams(dimension_semantics=("arbitrary",), vmem_limit_bytes=VMEM_LIMIT),
        name="ffn",
    )(x2, pw['w_ff1'], pw['b_ff1'], pw['w_ff2'], pw['b_ff2'], pw['ln2_g'], pw['ln2_b'])


def _swap_halves(w):
    half = w.shape[-1] // 2
    return jnp.concatenate([w[..., half:], w[..., :half]], axis=-1)


def _pack_layer(l, w_in, q_norm, w_q_up, kv_norm, w_kv_up, sgu_ln_g, sgu_ln_b, sgu_w, sgu_b, mix_norm,
                w_out, ln1_g, ln1_b, w_ff1, b_ff1, w_ff2, b_ff2, ln2_g, ln2_b):
    w = w_in[l]
    zc = lambda n: jnp.zeros((D_MODEL, n), F32)
    off_cq = 3 * A_WIDTH
    off_ckv = off_cq + Q_RANK
    off_kpe = off_ckv + KV_RANK
    off_u = off_kpe + B_ROPE
    kpe = w[:, off_kpe:off_u]
    w_packed = jnp.concatenate([w[:, :off_cq], w[:, off_ckv:off_kpe], w[:, off_cq:off_ckv], zc(64),
                                zc(64), kpe, _swap_halves(kpe), w[:, off_u:]], axis=1)
    wq = w_q_up[l].reshape(Q_RANK, B_HEADS, B_NOPE + B_ROPE)
    wq = jnp.concatenate([wq, _swap_halves(wq[..., B_NOPE:])], axis=-1).reshape(Q_RANK, B_HEADS * HEAD_PAD)
    wkv = w_kv_up[l].reshape(KV_RANK, B_HEADS, B_NOPE + B_VDIM)
    wk = jnp.concatenate([wkv[..., :B_NOPE], jnp.zeros((KV_RANK, B_HEADS, HEAD_PAD - B_NOPE), F32)], axis=-1)
    row = lambda a: a.reshape(1, -1)
    mn = mix_norm[l]
    return {
        'w_in': w_packed.astype(BF16),
        'q_norm': row(q_norm[l]), 'w_q': wq.astype(BF16),
        'kv_norm': row(kv_norm[l]), 'w_k': wk.reshape(KV_RANK, B_HEADS * HEAD_PAD).astype(BF16),
        'w_v': wkv[..., B_NOPE:].reshape(KV_RANK, B_WIDTH).astype(BF16),
        'sgu_g': row(sgu_ln_g[l]), 'sgu_b': row(sgu_ln_b[l]),
        'ws_cat': jnp.concatenate([sgu_w[l][g] for g in range(C_GROUPS)], axis=1).astype(BF16),
        'bs_tab': jnp.repeat(sgu_b[l].T, C_GROUP_DIM, axis=1),
        'mn_a': row(mn[:A_WIDTH]), 'mn_b': row(mn[A_WIDTH:A_WIDTH + B_WIDTH]), 'mn_c': row(mn[A_WIDTH + B_WIDTH:]),
        'w_out': w_out[l].astype(BF16),
        'ln1_g': row(ln1_g[l]), 'ln1_b': row(ln1_b[l]),
        'w_ff1': w_ff1[l].astype(BF16), 'b_ff1': row(b_ff1[l]), 'w_ff2': w_ff2[l].astype(BF16),
        'b_ff2': row(b_ff2[l]), 'ln2_g': row(ln2_g[l]), 'ln2_b': row(ln2_b[l]),
    }


def _rope_tables(s_len):
    inv_freq = ROPE_THETA ** (-jnp.arange(0, B_ROPE, 2, dtype=F32) / B_ROPE)
    ang = jnp.arange(s_len, dtype=F32)[:, None] * inv_freq[None, :]
    cos, sin = jnp.cos(ang), jnp.sin(ang)
    ones = jnp.ones((s_len, B_NOPE), F32)
    zeros = lambda n: jnp.zeros((s_len, n), F32)
    pad = HEAD_PAD - B_NOPE - B_ROPE
    c_tab = jnp.concatenate([cos, cos, zeros(pad)], axis=1)
    s_tab = jnp.concatenate([-sin, sin, zeros(pad)], axis=1)
    q_scale = (B_NOPE + B_ROPE) ** -0.5 * LOG2E
    return {'cq': q_scale * jnp.concatenate([ones, c_tab], axis=1),
            'sq': q_scale * jnp.concatenate([zeros(B_NOPE), s_tab], axis=1),
            'ck': jnp.concatenate([zeros(B_NOPE), c_tab], axis=1),
            'sk': jnp.concatenate([zeros(B_NOPE), s_tab], axis=1)}


def _dilated_bias():
    e = jnp.arange(BIAS_ROWS)[:, None]
    ql = jnp.arange(TQ_A)[None, :]
    dist = jnp.abs(e - 2 * HALO - ql)
    cnt = sum(((dist % d == 0) & (dist <= (w // (2 * d)) * d)).astype(F32) for w, d in DILATED_PATTERNS)
    base = jnp.where(cnt > 0, jnp.log2(jnp.maximum(cnt, 1.0)), NEG)
    base, dist = lax.optimization_barrier((base, dist.astype(F32)))
    slopes = jnp.asarray(2.0 ** (-8.0 * np.arange(1, A_HEADS + 1) / A_HEADS), dtype=F32)
    return base[None] - (slopes * LOG2E)[:, None, None] * dist[None]


def kernel(x, w_in, q_norm, w_q_up, kv_norm, w_kv_up, sgu_ln_g, sgu_ln_b, sgu_w, sgu_b, mix_norm, w_out,
           ln1_g, ln1_b, w_ff1, b_ff1, w_ff2, b_ff2, ln2_g, ln2_b):
    batch, s_len, _ = x.shape
    assert s_len >= WIN_A and s_len % TQ_A == 0
    t_len = batch * s_len
    tabs = _rope_tables(s_len)
    bias = _dilated_bias()
    x2 = x.reshape(t_len, D_MODEL)
    for l in range(DEPTH):
        pw = _pack_layer(l, w_in, q_norm, w_q_up, kv_norm, w_kv_up, sgu_ln_g, sgu_ln_b, sgu_w, sgu_b, mix_norm,
                         w_out, ln1_g, ln1_b, w_ff1, b_ff1, w_ff2, b_ff2, ln2_g, ln2_b)
        qaT, ka, vaT, qbT, kb, vbT, yc = _inproj(x2, pw, tabs, s_len)
        oaT = _dilated(qaT, ka, vaT, bias, batch, s_len)
        obT = _mla(qbT, kb, vbT, batch, s_len)
        x1 = _merge(oaT, obT, yc, x2, pw)
        x2 = _ffn(x1, pw)
    return x2.reshape(batch, s_len, D_MODEL)
```

```python
import functools
import math

import numpy as np
import jax
import jax.numpy as jnp
from jax import lax
from jax.experimental import pallas as pl
from jax.experimental.pallas import tpu as pltpu

BF16 = jnp.bfloat16
F32 = jnp.float32

D_MODEL = 1024
DEPTH = 2
HEAD_DIM = 64
A_HEADS = 6
A_WIDTH = A_HEADS * HEAD_DIM
DILATED_PATTERNS = ((128, 1), (512, 4), (2048, 16))
B_HEADS = 6
B_NOPE = 64
B_ROPE = 32
B_VDIM = 64
Q_RANK = 192
KV_RANK = 128
B_WIDTH = B_HEADS * B_VDIM
ROPE_THETA = 10000.0
C_GROUPS = 4
C_GROUP_DIM = 64
C_WIDTH = C_GROUPS * C_GROUP_DIM
CHUNK = 128
D_FF = 4 * D_MODEL
ALPHA = (2 * DEPTH) ** 0.25
EPS = 1e-5
NEG = -1e30
LOG2E = math.log2(math.e)

LANES = 128
BF16_SUBLANES = 16
VT_ROWS = B_VDIM + BF16_SUBLANES
HEAD_PAD = LANES

P_QA, P_KA, P_VA = 0, A_WIDTH, 2 * A_WIDTH
P_CKV = 3 * A_WIDTH
P_CQ = P_CKV + KV_RANK
P_KPE = P_CQ + 256
P_U = P_KPE + LANES
P_V = P_U + C_WIDTH
P_PACKED = P_V + C_WIDTH

HALO = max((w // (2 * d)) * d for w, d in DILATED_PATTERNS)
TQ_A = 256
WIN_A = TQ_A + 2 * HALO
BIAS_ROWS = WIN_A + 2 * HALO

VMEM_LIMIT = 48 * 1024 * 1024
SCORE_PAD = 0
INPROJ_TM, INPROJ_SUB = 1024, 256
DIL_CHUNK, DIL_UNROLL = 4096, 4
MLA_TQ, MLA_TK, MLA_UNROLL = 1024, 256, 32
MERGE_TM, MERGE_SUB = 1024, 256
FFN_TM, FFN_SUB = 1024, 256


def _rms_scale(x):
    return x * lax.rsqrt(jnp.mean(x * x, axis=-1, keepdims=True) + EPS)


def _layer_norm(x, g, b):
    mu = jnp.mean(x, axis=-1, keepdims=True)
    xc = x - mu
    var = jnp.mean(xc * xc, axis=-1, keepdims=True)
    return xc * lax.rsqrt(var + EPS) * g + b


def _dot(a, b):
    return jnp.dot(a, b, preferred_element_type=F32)


def _tile(n, pref):
    t = min(pref, n)
    assert n % t == 0
    return t


def _inproj_kernel(x_ref, w_ref, qn_ref, wq_ref, kvn_ref, wk_ref, wv_ref, cq_ref, sq_ref, ck_ref, sk_ref,
                   lng_ref, lnb_ref, ws_ref, bs_ref, mnc_ref,
                   qaT_ref, ka_ref, vaT_ref, qbT_ref, kb_ref, vbT_ref, yc_ref):
    sub = INPROJ_SUB
    ones_tile = (lax.broadcasted_iota(jnp.int32, (BF16_SUBLANES, sub), 0) == 0).astype(BF16)
    zeros_half = jnp.zeros((HEAD_DIM, sub), BF16)
    group = lax.broadcasted_iota(jnp.int32, (CHUNK, C_WIDTH), 1) // C_GROUP_DIM

    def put_values_t(vT, out_ref, width, rows):
        for hd in range(A_HEADS):
            out_ref[hd * VT_ROWS:hd * VT_ROWS + width, rows] = vT[hd * width:(hd + 1) * width].astype(BF16)
            out_ref[hd * VT_ROWS + width:(hd + 1) * VT_ROWS, rows] = ones_tile

    for r in range(0, x_ref.shape[0], sub):
        rows = slice(r, r + sub)
        h = _dot(x_ref[rows, :].astype(BF16), w_ref[...])

        qaT = (h[:, P_QA:P_QA + A_WIDTH] * (HEAD_DIM ** -0.5 * LOG2E)).T.astype(BF16)
        for hd in range(A_HEADS):
            q_h = qaT[hd * HEAD_DIM:(hd + 1) * HEAD_DIM]
            lo, hi = (q_h, zeros_half) if hd % 2 == 0 else (zeros_half, q_h)
            qaT_ref[hd * HEAD_PAD:hd * HEAD_PAD + HEAD_DIM, rows] = lo
            qaT_ref[hd * HEAD_PAD + HEAD_DIM:(hd + 1) * HEAD_PAD, rows] = hi
        ka_ref[rows, :] = h[:, P_KA:P_KA + A_WIDTH].astype(BF16)
        put_values_t(h[:, P_VA:P_VA + A_WIDTH].T, vaT_ref, HEAD_DIM, rows)

        c_kv = (_rms_scale(h[:, P_CKV:P_CKV + KV_RANK]) * kvn_ref[...]).astype(BF16)
        kmat = _dot(c_kv, wk_ref[...])
        vmat = _dot(c_kv, wv_ref[...])
        g_k = h[:, P_KPE:P_KPE + LANES]
        kpe = g_k * ck_ref[rows, :] + pltpu.roll(g_k, 96, 1) * sk_ref[rows, :]
        c_q = (_rms_scale(h[:, P_CQ:P_CQ + Q_RANK]) * qn_ref[...]).astype(BF16)
        qmat = _dot(c_q, wq_ref[...])
        cq_t, sq_t = cq_ref[rows, :], sq_ref[rows, :]
        for hd in range(B_HEADS):
            sl = slice(hd * HEAD_PAD, (hd + 1) * HEAD_PAD)
            kb_ref[rows, sl] = (kmat[:, sl] + kpe).astype(BF16)
            g_q = qmat[:, sl]
            q_h = g_q * cq_t + pltpu.roll(g_q, 96, 1) * sq_t
            qbT_ref[sl, rows] = q_h.T.astype(BF16)
        put_values_t(vmat.T, vbT_ref, B_VDIM, rows)

        z = jax.nn.gelu(h[:, P_U:P_U + 2 * C_WIDTH])
        u = z[:, :C_WIDTH]
        v = _layer_norm(z[:, C_WIDTH:], lng_ref[...], lnb_ref[...])
        for c in range(sub // CHUNK):
            crows = slice(c * CHUNK, (c + 1) * CHUNK)
            vc = v[crows]
            stack = jnp.concatenate([jnp.where(group == g, vc, 0.0) for g in range(C_GROUPS)], axis=0).astype(BF16)
            mixed = _dot(ws_ref[...], stack) + bs_ref[...]
            yc = u[crows] * mixed
            yc_ref[r + c * CHUNK:r + (c + 1) * CHUNK, :] = (_rms_scale(yc) * mnc_ref[...]).astype(BF16)


def _inproj(x2, pw, tabs, s_len):
    t_len = x2.shape[0]
    tm = _tile(s_len, INPROJ_TM)
    nt = t_len // tm
    ns = s_len // tm
    row = lambda i: (i, 0)
    col = lambda i: (0, i)
    fixed = lambda i: (0, 0)
    pos = lambda i: (i % ns, 0)
    full = lambda a: pl.BlockSpec(a.shape, fixed)
    tab = pl.BlockSpec((tm, LANES), pos)
    in_specs = [pl.BlockSpec((tm, D_MODEL), row), full(pw['w_in']), full(pw['q_norm']), full(pw['w_q']),
                full(pw['kv_norm']), full(pw['w_k']), full(pw['w_v']), tab, tab, tab, tab,
                full(pw['sgu_g']), full(pw['sgu_b']), full(pw['ws_cat']), full(pw['bs_tab']), full(pw['mn_c'])]
    out_shape = [jax.ShapeDtypeStruct((A_HEADS * HEAD_PAD, t_len), BF16),
                 jax.ShapeDtypeStruct((t_len, A_WIDTH), BF16),
                 jax.ShapeDtypeStruct((A_HEADS * VT_ROWS, t_len), BF16),
                 jax.ShapeDtypeStruct((B_HEADS * HEAD_PAD, t_len), BF16),
                 jax.ShapeDtypeStruct((t_len, B_HEADS * HEAD_PAD), BF16),
                 jax.ShapeDtypeStruct((B_HEADS * VT_ROWS, t_len), BF16),
                 jax.ShapeDtypeStruct((t_len, C_WIDTH), BF16)]
    out_specs = [pl.BlockSpec((A_HEADS * HEAD_PAD, tm), col),
                 pl.BlockSpec((tm, A_WIDTH), row),
                 pl.BlockSpec((A_HEADS * VT_ROWS, tm), col),
                 pl.BlockSpec((B_HEADS * HEAD_PAD, tm), col),
                 pl.BlockSpec((tm, B_HEADS * HEAD_PAD), row),
                 pl.BlockSpec((B_HEADS * VT_ROWS, tm), col),
                 pl.BlockSpec((tm, C_WIDTH), row)]
    return pl.pallas_call(
        _inproj_kernel, grid=(nt,), in_specs=in_specs, out_specs=out_specs, out_shape=out_shape,
        compiler_params=pltpu.CompilerParams(dimension_semantics=("arbitrary",), vmem_limit_bytes=VMEM_LIMIT),
        name="inproj",
    )(x2, pw['w_in'], pw['q_norm'], pw['w_q'], pw['kv_norm'], pw['w_k'], pw['w_v'],
      tabs['cq'], tabs['sq'], tabs['ck'], tabs['sk'],
      pw['sgu_g'], pw['sgu_b'], pw['ws_cat'], pw['bs_tab'], pw['mn_c'])


def _dilated_kernel(qT_ref, k_ref, vT_ref, bias_ref, o_ref, s_ref, *, s_len):
    nblk = qT_ref.shape[1] // TQ_A
    first = pl.program_id(2) * nblk

    def window(i):
        t0 = (first + i) * TQ_A
        w0 = jnp.clip(t0 - HALO, 0, s_len - WIN_A)
        b0 = w0 - t0 + 2 * HALO
        return pl.multiple_of(w0, TQ_A), pl.multiple_of(b0, TQ_A), pl.multiple_of(i * TQ_A, TQ_A)

    def scores(i, slot):
        w0, b0, q0 = window(i)
        qT = jnp.concatenate([qT_ref[hh * HEAD_PAD:(hh + 1) * HEAD_PAD, pl.ds(q0, TQ_A)] for hh in range(2)], axis=1)
        s = _dot(k_ref[pl.ds(w0, WIN_A), :], qT)
        m = []
        for hh in range(2):
            s_h = s[:, hh * TQ_A:(hh + 1) * TQ_A] + bias_ref[hh, pl.ds(b0, WIN_A), :]
            s_ref[slot, :, hh * TQ_A:(hh + 1) * TQ_A] = s_h
            m.append(jnp.max(s_h, axis=0, keepdims=True))
        return tuple(m)

    def finish(i, slot, m):
        w0, _, q0 = window(i)
        for hh in range(2):
            p = jnp.exp2(s_ref[slot, :, hh * TQ_A:(hh + 1) * TQ_A] - m[hh]).astype(BF16)
            acc = _dot(vT_ref[hh * VT_ROWS:(hh + 1) * VT_ROWS, pl.ds(w0, WIN_A)], p)
            o_ref[hh * HEAD_DIM:(hh + 1) * HEAD_DIM, pl.ds(q0, TQ_A)] = acc[:HEAD_DIM] / acc[HEAD_DIM:HEAD_DIM + 1]

    def body(i, m):
        for u in range(DIL_UNROLL):
            j = DIL_UNROLL * i + u
            m_next = scores(jnp.minimum(j + 1, nblk - 1), (u + 1) % 2)
            finish(j, u % 2, m)
            m = m_next
        return m

    lax.fori_loop(0, nblk // DIL_UNROLL, body, scores(0, 0))


def _dilated(qaT, ka, vaT, bias, batch, s_len):
    chunk = _tile(s_len, DIL_CHUNK)
    assert (chunk // TQ_A) % DIL_UNROLL == 0 and DIL_UNROLL % 2 == 0
    nc = s_len // chunk
    t_len = batch * s_len
    once = pl.Buffered(1)
    return pl.pallas_call(
        functools.partial(_dilated_kernel, s_len=s_len),
        grid=(batch, A_HEADS // 2, nc),
        in_specs=[pl.BlockSpec((2 * HEAD_PAD, chunk), lambda b, h, c: (h, b * nc + c)),
                  pl.BlockSpec((s_len, LANES), lambda b, h, c: (b, h), pipeline_mode=once),
                  pl.BlockSpec((2 * VT_ROWS, s_len), lambda b, h, c: (h, b), pipeline_mode=once),
                  pl.BlockSpec((2, BIAS_ROWS, TQ_A), lambda b, h, c: (h, 0, 0), pipeline_mode=once)],
        out_specs=pl.BlockSpec((2 * HEAD_DIM, chunk), lambda b, h, c: (h, b * nc + c)),
        out_shape=jax.ShapeDtypeStruct((A_WIDTH, t_len), F32),
        scratch_shapes=[pltpu.VMEM((2, WIN_A, 2 * TQ_A + SCORE_PAD), F32)],
        compiler_params=pltpu.CompilerParams(dimension_semantics=("arbitrary",) * 3, vmem_limit_bytes=VMEM_LIMIT),
        name="dilated",
    )(qaT, ka, vaT, bias)


def _mla_kernel(qT_ref, k_ref, vT_ref, o_ref, acc_ref, s_ref, *, tq, tk, unroll):
    s_len = k_ref.shape[0]
    nk = s_len // tk

    def query_tile(qi, _):
        q0 = pl.multiple_of(qi * tq, tq)
        qT = qT_ref[:, pl.ds(q0, tq)]
        acc_ref[...] = jnp.zeros_like(acc_ref)

        def scores(j, slot):
            off = pl.multiple_of(j * tk, tk)
            s = _dot(k_ref[pl.ds(off, tk), :], qT)
            s_ref[slot, :, :tq] = s
            return jnp.max(s, axis=0, keepdims=True)

        def step(j, slot, m_prev, m_tile):
            m_tile_next = scores(jnp.minimum(j + 1, nk - 1), 1 - slot)
            off = pl.multiple_of(j * tk, tk)
            m_new = jnp.maximum(m_prev, m_tile)
            alpha = jnp.exp2(m_prev - m_new)
            p = jnp.exp2(s_ref[slot, :, :tq] - m_new).astype(BF16)
            acc_ref[...] = acc_ref[...] * alpha + _dot(vT_ref[:, pl.ds(off, tk)], p)
            return m_new, m_tile_next

        def body(i, carry):
            for u in range(unroll):
                carry = step(unroll * i + u, u % 2, *carry)
            return carry

        lax.fori_loop(0, nk // unroll, body, (jnp.full((1, tq), NEG, F32), scores(0, 0)))
        acc = acc_ref[...]
        o_ref[:, pl.ds(q0, tq)] = acc[:B_VDIM] / acc[B_VDIM:B_VDIM + 1]
        return 0

    lax.fori_loop(0, s_len // tq, query_tile, 0)


def _mla(qbT, kb, vbT, batch, s_len):
    tq, tk = _tile(s_len, MLA_TQ), _tile(s_len, MLA_TK)
    nk = s_len // tk
    unroll = min(MLA_UNROLL, nk)
    assert nk % unroll == 0 and unroll % 2 == 0
    t_len = batch * s_len
    return pl.pallas_call(
        functools.partial(_mla_kernel, tq=tq, tk=tk, unroll=unroll),
        grid=(batch, B_HEADS),
        in_specs=[pl.BlockSpec((HEAD_PAD, s_len), lambda b, h: (h, b)),
                  pl.BlockSpec((s_len, HEAD_PAD), lambda b, h: (b, h)),
                  pl.BlockSpec((VT_ROWS, s_len), lambda b, h: (h, b))],
        out_specs=pl.BlockSpec((B_VDIM, s_len), lambda b, h: (h, b)),
        out_shape=jax.ShapeDtypeStruct((B_WIDTH, t_len), F32),
        scratch_shapes=[pltpu.VMEM((VT_ROWS, tq), F32), pltpu.VMEM((2, tk, tq + SCORE_PAD), F32)],
        compiler_params=pltpu.CompilerParams(dimension_semantics=("arbitrary",) * 2, vmem_limit_bytes=VMEM_LIMIT),
        name="mla",
    )(qbT, kb, vbT)


def _merge_kernel(oaT_ref, obT_ref, yc_ref, x_ref, wo_ref, mna_ref, mnb_ref, g_ref, b_ref, o_ref):
    for r in range(0, x_ref.shape[0], MERGE_SUB):
        rows = slice(r, r + MERGE_SUB)
        ya = (_rms_scale(oaT_ref[:, rows].T) * mna_ref[...]).astype(BF16)
        yb = (_rms_scale(obT_ref[:, rows].T) * mnb_ref[...]).astype(BF16)
        y = _dot(jnp.concatenate([ya, yb, yc_ref[rows, :]], axis=1), wo_ref[...])
        o_ref[rows, :] = _layer_norm(ALPHA * x_ref[rows, :] + y, g_ref[...], b_ref[...])


def _merge(oaT, obT, yc, x2, pw):
    t_len = x2.shape[0]
    tm = _tile(t_len, MERGE_TM)
    row = lambda i: (i, 0)
    col = lambda i: (0, i)
    full = lambda a: pl.BlockSpec(a.shape, lambda i: (0, 0))
    return pl.pallas_call(
        _merge_kernel, grid=(t_len // tm,),
        in_specs=[pl.BlockSpec((A_WIDTH, tm), col), pl.BlockSpec((B_WIDTH, tm), col),
                  pl.BlockSpec((tm, C_WIDTH), row), pl.BlockSpec((tm, D_MODEL), row),
                  full(pw['w_out']), full(pw['mn_a']), full(pw['mn_b']), full(pw['ln1_g']), full(pw['ln1_b'])],
        out_specs=pl.BlockSpec((tm, D_MODEL), row),
        out_shape=jax.ShapeDtypeStruct((t_len, D_MODEL), F32),
        compiler_params=pltpu.CompilerParams(dimension_semantics=("arbitrary",), vmem_limit_bytes=VMEM_LIMIT),
        name="merge",
    )(oaT, obT, yc, x2, pw['w_out'], pw['mn_a'], pw['mn_b'], pw['ln1_g'], pw['ln1_b'])


def _ffn_kernel(x_ref, w1_ref, b1_ref, w2_ref, b2_ref, g_ref, b_ref, o_ref):
    for r in range(0, x_ref.shape[0], FFN_SUB):
        rows = slice(r, r + FFN_SUB)
        x = x_ref[rows, :]
        hid = _dot(x.astype(BF16), w1_ref[...]) + b1_ref[...]
        hid = jnp.square(jnp.maximum(hid, 0.0)).astype(BF16)
        f = _dot(hid, w2_ref[...]) + b2_ref[...]
        o_ref[rows, :] = _layer_norm(ALPHA * x + f, g_ref[...], b_ref[...])


def _ffn(x2, pw):
    t_len = x2.shape[0]
    tm = _tile(t_len, FFN_TM)
    fixed = lambda i: (0, 0)
    once = pl.Buffered(1)
    return pl.pallas_call(
        _ffn_kernel, grid=(t_len // tm,),
        in_specs=[pl.BlockSpec((tm, D_MODEL), lambda i: (i, 0)),
                  pl.BlockSpec((D_MODEL, D_FF), fixed, pipeline_mode=once),
                  pl.BlockSpec((1, D_FF), fixed),
                  pl.BlockSpec((D_FF, D_MODEL), fixed, pipeline_mode=once),
                  pl.BlockSpec((1, D_MODEL), fixed), pl.BlockSpec((1, D_MODEL), fixed),
                  pl.BlockSpec((1, D_MODEL), fixed)],
        out_specs=pl.BlockSpec((tm, D_MODEL), lambda i: (i, 0)),
        out_shape=jax.ShapeDtypeStruct((t_len, D_MODEL), F32),
        compiler_params=pltpu.CompilerParams(dimension_semantics=("arbitrary",), vmem_limit_bytes=VMEM_LIMIT),
        name="ffn",
    )(x2, pw['w_ff1'], pw['b_ff1'], pw['w_ff2'], pw['b_ff2'], pw['ln2_g'], pw['ln2_b'])


def _swap_halves(w):
    half = w.shape[-1] // 2
    return jnp.concatenate([w[..., half:], w[..., :half]], axis=-1)


def _pack_layer(l, w_in, q_norm, w_q_up, kv_norm, w_kv_up, sgu_ln_g, sgu_ln_b, sgu_w, sgu_b, mix_norm,
                w_out, ln1_g, ln1_b, w_ff1, b_ff1, w_ff2, b_ff2, ln2_g, ln2_b):
    w = w_in[l]
    zc = lambda n: jnp.zeros((D_MODEL, n), F32)
    off_cq = 3 * A_WIDTH
    off_ckv = off_cq + Q_RANK
    off_kpe = off_ckv + KV_RANK
    off_u = off_kpe + B_ROPE
    kpe = w[:, off_kpe:off_u]
    w_packed = jnp.concatenate([w[:, :off_cq], w[:, off_ckv:off_kpe], w[:, off_cq:off_ckv], zc(64),
                                zc(64), kpe, _swap_halves(kpe), w[:, off_u:]], axis=1)
    wq = w_q_up[l].reshape(Q_RANK, B_HEADS, B_NOPE + B_ROPE)
    wq = jnp.concatenate([wq, _swap_halves(wq[..., B_NOPE:])], axis=-1).reshape(Q_RANK, B_HEADS * HEAD_PAD)
    wkv = w_kv_up[l].reshape(KV_RANK, B_HEADS, B_NOPE + B_VDIM)
    wk = jnp.concatenate([wkv[..., :B_NOPE], jnp.zeros((KV_RANK, B_HEADS, HEAD_PAD - B_NOPE), F32)], axis=-1)
    row = lambda a: a.reshape(1, -1)
    mn = mix_norm[l]
    return {
        'w_in': w_packed.astype(BF16),
        'q_norm': row(q_norm[l]), 'w_q': wq.astype(BF16),
        'kv_norm': row(kv_norm[l]), 'w_k': wk.reshape(KV_RANK, B_HEADS * HEAD_PAD).astype(BF16),
        'w_v': wkv[..., B_NOPE:].reshape(KV_RANK, B_WIDTH).astype(BF16),
        'sgu_g': row(sgu_ln_g[l]), 'sgu_b': row(sgu_ln_b[l]),
        'ws_cat': jnp.concatenate([sgu_w[l][g] for g in range(C_GROUPS)], axis=1).astype(BF16),
        'bs_tab': jnp.repeat(sgu_b[l].T, C_GROUP_DIM, axis=1),
        'mn_a': row(mn[:A_WIDTH]), 'mn_b': row(mn[A_WIDTH:A_WIDTH + B_WIDTH]), 'mn_c': row(mn[A_WIDTH + B_WIDTH:]),
        'w_out': w_out[l].astype(BF16),
        'ln1_g': row(ln1_g[l]), 'ln1_b': row(ln1_b[l]),
        'w_ff1': w_ff1[l].astype(BF16), 'b_ff1': row(b_ff1[l]), 'w_ff2': w_ff2[l].astype(BF16),
        'b_ff2': row(b_ff2[l]), 'ln2_g': row(ln2_g[l]), 'ln2_b': row(ln2_b[l]),
    }


def _rope_tables(s_len):
    inv_freq = ROPE_THETA ** (-jnp.arange(0, B_ROPE, 2, dtype=F32) / B_ROPE)
    ang = jnp.arange(s_len, dtype=F32)[:, None] * inv_freq[None, :]
    cos, sin = jnp.cos(ang), jnp.sin(ang)
    ones = jnp.ones((s_len, B_NOPE), F32)
    zeros = lambda n: jnp.zeros((s_len, n), F32)
    pad = HEAD_PAD - B_NOPE - B_ROPE
    c_tab = jnp.concatenate([cos, cos, zeros(pad)], axis=1)
    s_tab = jnp.concatenate([-sin, sin, zeros(pad)], axis=1)
    q_scale = (B_NOPE + B_ROPE) ** -0.5 * LOG2E
    return {'cq': q_scale * jnp.concatenate([ones, c_tab], axis=1),
            'sq': q_scale * jnp.concatenate([zeros(B_NOPE), s_tab], axis=1),
            'ck': jnp.concatenate([zeros(B_NOPE), c_tab], axis=1),
            'sk': jnp.concatenate([zeros(B_NOPE), s_tab], axis=1)}


def _dilated_bias():
    e = jnp.arange(BIAS_ROWS)[:, None]
    ql = jnp.arange(TQ_A)[None, :]
    dist = jnp.abs(e - 2 * HALO - ql)
    cnt = sum(((dist % d == 0) & (dist <= (w // (2 * d)) * d)).astype(F32) for w, d in DILATED_PATTERNS)
    base = jnp.where(cnt > 0, jnp.log2(jnp.maximum(cnt, 1.0)), NEG)
    base, dist = lax.optimization_barrier((base, dist.astype(F32)))
    slopes = jnp.asarray(2.0 ** (-8.0 * np.arange(1, A_HEADS + 1) / A_HEADS), dtype=F32)
    return base[None] - (slopes * LOG2E)[:, None, None] * dist[None]


def kernel(x, w_in, q_norm, w_q_up, kv_norm, w_kv_up, sgu_ln_g, sgu_ln_b, sgu_w, sgu_b, mix_norm, w_out,
           ln1_g, ln1_b, w_ff1, b_ff1, w_ff2, b_ff2, ln2_g, ln2_b):
    batch, s_len, _ = x.shape
    assert s_len >= WIN_A and s_len % TQ_A == 0
    t_len = batch * s_len
    tabs = _rope_tables(s_len)
    bias = _dilated_bias()
    x2 = x.reshape(t_len, D_MODEL)
    for l in range(DEPTH):
        pw = _pack_layer(l, w_in, q_norm, w_q_up, kv_norm, w_kv_up, sgu_ln_g, sgu_ln_b, sgu_w, sgu_b, mix_norm,
                         w_out, ln1_g, ln1_b, w_ff1, b_ff1, w_ff2, b_ff2, ln2_g, ln2_b)
        qaT, ka, vaT, qbT, kb, vbT, yc = _inproj(x2, pw, tabs, s_len)
        oaT = _dilated(qaT, ka, vaT, bias, batch, s_len)
        obT = _mla(qbT, kb, vbT, batch, s_len)
        x1 = _merge(oaT, obT, yc, x2, pw)
        x2 = _ffn(x1, pw)
    return x2.reshape(batch, s_len, D_MODEL)
```

```python
import functools
import math

import numpy as np
import jax
import jax.numpy as jnp
from jax import lax
from jax.experimental import pallas as pl
from jax.experimental.pallas import tpu as pltpu

BF16 = jnp.bfloat16
F32 = jnp.float32

D_MODEL = 1024
DEPTH = 2
HEAD_DIM = 64
A_HEADS = 6
A_WIDTH = A_HEADS * HEAD_DIM
DILATED_PATTERNS = ((128, 1), (512, 4), (2048, 16))
B_HEADS = 6
B_NOPE = 64
B_ROPE = 32
B_VDIM = 64
Q_RANK = 192
KV_RANK = 128
B_WIDTH = B_HEADS * B_VDIM
ROPE_THETA = 10000.0
C_GROUPS = 4
C_GROUP_DIM = 64
C_WIDTH = C_GROUPS * C_GROUP_DIM
CHUNK = 128
D_FF = 4 * D_MODEL
ALPHA = (2 * DEPTH) ** 0.25
EPS = 1e-5
NEG = -1e30
LOG2E = math.log2(math.e)

LANES = 128
BF16_SUBLANES = 16
VT_ROWS = B_VDIM + BF16_SUBLANES
HEAD_PAD = LANES

P_QA, P_KA, P_VA = 0, A_WIDTH, 2 * A_WIDTH
P_CKV = 3 * A_WIDTH
P_CQ = P_CKV + KV_RANK
P_KPE = P_CQ + 256
P_U = P_KPE + LANES
P_V = P_U + C_WIDTH
P_PACKED = P_V + C_WIDTH

HALO = max((w // (2 * d)) * d for w, d in DILATED_PATTERNS)
TQ_A = 256
WIN_A = TQ_A + 2 * HALO
BIAS_ROWS = WIN_A + 2 * HALO

VMEM_LIMIT = 48 * 1024 * 1024
INPROJ_TM, INPROJ_SUB = 1024, 256
DIL_CHUNK, DIL_UNROLL = 8192, 4
MLA_TQ, MLA_TK, MLA_UNROLL = 1024, 256, 32
MERGE_TM, MERGE_SUB = 1024, 256
FFN_TM, FFN_SUB = 1024, 256


def _rms_scale(x):
    return x * lax.rsqrt(jnp.mean(x * x, axis=-1, keepdims=True) + EPS)


def _layer_norm(x, g, b):
    mu = jnp.mean(x, axis=-1, keepdims=True)
    xc = x - mu
    var = jnp.mean(xc * xc, axis=-1, keepdims=True)
    return xc * lax.rsqrt(var + EPS) * g + b


def _dot(a, b):
    return jnp.dot(a, b, preferred_element_type=F32)


def _tile(n, pref):
    t = min(pref, n)
    assert n % t == 0
    return t


def _inproj_kernel(x_ref, w_ref, qn_ref, wq_ref, kvn_ref, wk_ref, wv_ref, cq_ref, sq_ref, ck_ref, sk_ref,
                   lng_ref, lnb_ref, ws_ref, bs_ref, mnc_ref,
                   qaT_ref, ka_ref, vaT_ref, qbT_ref, kb_ref, vbT_ref, yc_ref):
    sub = INPROJ_SUB
    ones_tile = (lax.broadcasted_iota(jnp.int32, (BF16_SUBLANES, sub), 0) == 0).astype(BF16)
    zeros_half = jnp.zeros((HEAD_DIM, sub), BF16)
    group = lax.broadcasted_iota(jnp.int32, (CHUNK, C_WIDTH), 1) // C_GROUP_DIM

    def put_values_t(vT, out_ref, width, rows):
        for hd in range(A_HEADS):
            out_ref[hd * VT_ROWS:hd * VT_ROWS + width, rows] = vT[hd * width:(hd + 1) * width].astype(BF16)
            out_ref[hd * VT_ROWS + width:(hd + 1) * VT_ROWS, rows] = ones_tile

    for r in range(0, x_ref.shape[0], sub):
        rows = slice(r, r + sub)
        h = _dot(x_ref[rows, :].astype(BF16), w_ref[...])

        qaT = (h[:, P_QA:P_QA + A_WIDTH] * (HEAD_DIM ** -0.5 * LOG2E)).T.astype(BF16)
        for hd in range(A_HEADS):
            q_h = qaT[hd * HEAD_DIM:(hd + 1) * HEAD_DIM]
            lo, hi = (q_h, zeros_half) if hd % 2 == 0 else (zeros_half, q_h)
            qaT_ref[hd * HEAD_PAD:hd * HEAD_PAD + HEAD_DIM, rows] = lo
            qaT_ref[hd * HEAD_PAD + HEAD_DIM:(hd + 1) * HEAD_PAD, rows] = hi
        ka_ref[rows, :] = h[:, P_KA:P_KA + A_WIDTH].astype(BF16)
        put_values_t(h[:, P_VA:P_VA + A_WIDTH].T, vaT_ref, HEAD_DIM, rows)

        c_kv = (_rms_scale(h[:, P_CKV:P_CKV + KV_RANK]) * kvn_ref[...]).astype(BF16)
        kmat = _dot(c_kv, wk_ref[...])
        vmat = _dot(c_kv, wv_ref[...])
        g_k = h[:, P_KPE:P_KPE + LANES]
        kpe = g_k * ck_ref[rows, :] + pltpu.roll(g_k, 96, 1) * sk_ref[rows, :]
        c_q = (_rms_scale(h[:, P_CQ:P_CQ + Q_RANK]) * qn_ref[...]).astype(BF16)
        qmat = _dot(c_q, wq_ref[...])
        cq_t, sq_t = cq_ref[rows, :], sq_ref[rows, :]
        for hd in range(B_HEADS):
            sl = slice(hd * HEAD_PAD, (hd + 1) * HEAD_PAD)
            kb_ref[rows, sl] = (kmat[:, sl] + kpe).astype(BF16)
            g_q = qmat[:, sl]
            q_h = g_q * cq_t + pltpu.roll(g_q, 96, 1) * sq_t
            qbT_ref[sl, rows] = q_h.T.astype(BF16)
        put_values_t(vmat.T, vbT_ref, B_VDIM, rows)

        z = jax.nn.gelu(h[:, P_U:P_U + 2 * C_WIDTH])
        u = z[:, :C_WIDTH]
        v = _layer_norm(z[:, C_WIDTH:], lng_ref[...], lnb_ref[...])
        for c in range(sub // CHUNK):
            crows = slice(c * CHUNK, (c + 1) * CHUNK)
            vc = v[crows]
            stack = jnp.concatenate([jnp.where(group == g, vc, 0.0) for g in range(C_GROUPS)], axis=0).astype(BF16)
            mixed = _dot(ws_ref[...], stack) + bs_ref[...]
            yc = u[crows] * mixed
            yc_ref[r + c * CHUNK:r + (c + 1) * CHUNK, :] = (_rms_scale(yc) * mnc_ref[...]).astype(BF16)


def _inproj(x2, pw, tabs, s_len):
    t_len = x2.shape[0]
    tm = _tile(s_len, INPROJ_TM)
    nt = t_len // tm
    ns = s_len // tm
    row = lambda i: (i, 0)
    col = lambda i: (0, i)
    fixed = lambda i: (0, 0)
    pos = lambda i: (i % ns, 0)
    full = lambda a: pl.BlockSpec(a.shape, fixed)
    tab = pl.BlockSpec((tm, LANES), pos)
    in_specs = [pl.BlockSpec((tm, D_MODEL), row), full(pw['w_in']), full(pw['q_norm']), full(pw['w_q']),
                full(pw['kv_norm']), full(pw['w_k']), full(pw['w_v']), tab, tab, tab, tab,
                full(pw['sgu_g']), full(pw['sgu_b']), full(pw['ws_cat']), full(pw['bs_tab']), full(pw['mn_c'])]
    out_shape = [jax.ShapeDtypeStruct((A_HEADS * HEAD_PAD, t_len), BF16),
                 jax.ShapeDtypeStruct((t_len, A_WIDTH), BF16),
                 jax.ShapeDtypeStruct((A_HEADS * VT_ROWS, t_len), BF16),
                 jax.ShapeDtypeStruct((B_HEADS * HEAD_PAD, t_len), BF16),
                 jax.ShapeDtypeStruct((t_len, B_HEADS * HEAD_PAD), BF16),
                 jax.ShapeDtypeStruct((B_HEADS * VT_ROWS, t_len), BF16),
                 jax.ShapeDtypeStruct((t_len, C_WIDTH), BF16)]
    out_specs = [pl.BlockSpec((A_HEADS * HEAD_PAD, tm), col),
                 pl.BlockSpec((tm, A_WIDTH), row),
                 pl.BlockSpec((A_HEADS * VT_ROWS, tm), col),
                 pl.BlockSpec((B_HEADS * HEAD_PAD, tm), col),
                 pl.BlockSpec((tm, B_HEADS * HEAD_PAD), row),
                 pl.BlockSpec((B_HEADS * VT_ROWS, tm), col),
                 pl.BlockSpec((tm, C_WIDTH), row)]
    return pl.pallas_call(
        _inproj_kernel, grid=(nt,), in_specs=in_specs, out_specs=out_specs, out_shape=out_shape,
        compiler_params=pltpu.CompilerParams(dimension_semantics=("arbitrary",), vmem_limit_bytes=VMEM_LIMIT),
        name="inproj",
    )(x2, pw['w_in'], pw['q_norm'], pw['w_q'], pw['kv_norm'], pw['w_k'], pw['w_v'],
      tabs['cq'], tabs['sq'], tabs['ck'], tabs['sk'],
      pw['sgu_g'], pw['sgu_b'], pw['ws_cat'], pw['bs_tab'], pw['mn_c'])


def _dilated_kernel(qT_ref, k_ref, vT_ref, bias_ref, o_ref, s_ref, *, s_len):
    nblk = qT_ref.shape[1] // TQ_A
    first = pl.program_id(2) * nblk

    def window(i):
        t0 = (first + i) * TQ_A
        w0 = jnp.clip(t0 - HALO, 0, s_len - WIN_A)
        b0 = w0 - t0 + 2 * HALO
        return pl.multiple_of(w0, TQ_A), pl.multiple_of(b0, TQ_A), pl.multiple_of(i * TQ_A, TQ_A)

    def scores(i, slot):
        w0, b0, q0 = window(i)
        qT = jnp.concatenate([qT_ref[hh * HEAD_PAD:(hh + 1) * HEAD_PAD, pl.ds(q0, TQ_A)] for hh in range(2)], axis=1)
        s = _dot(k_ref[pl.ds(w0, WIN_A), :], qT)
        m = []
        for hh in range(2):
            s_h = s[:, hh * TQ_A:(hh + 1) * TQ_A] + bias_ref[hh, pl.ds(b0, WIN_A), :]
            s_ref[slot, :, hh * TQ_A:(hh + 1) * TQ_A] = s_h
            m.append(jnp.max(s_h, axis=0, keepdims=True))
        return tuple(m)

    def finish(i, slot, m):
        w0, _, q0 = window(i)
        for hh in range(2):
            p = jnp.exp2(s_ref[slot, :, hh * TQ_A:(hh + 1) * TQ_A] - m[hh]).astype(BF16)
            acc = _dot(vT_ref[hh * VT_ROWS:(hh + 1) * VT_ROWS, pl.ds(w0, WIN_A)], p)
            o_ref[hh * HEAD_DIM:(hh + 1) * HEAD_DIM, pl.ds(q0, TQ_A)] = acc[:HEAD_DIM] / acc[HEAD_DIM:HEAD_DIM + 1]

    def body(i, m):
        for u in range(DIL_UNROLL):
            j = DIL_UNROLL * i + u
            m_next = scores(jnp.minimum(j + 1, nblk - 1), (u + 1) % 2)
            finish(j, u % 2, m)
            m = m_next
        return m

    lax.fori_loop(0, nblk // DIL_UNROLL, body, scores(0, 0))


def _dilated(qaT, ka, vaT, bias, batch, s_len):
    chunk = _tile(s_len, DIL_CHUNK)
    assert (chunk // TQ_A) % DIL_UNROLL == 0 and DIL_UNROLL % 2 == 0
    nc = s_len // chunk
    t_len = batch * s_len
    once = pl.Buffered(1)
    return pl.pallas_call(
        functools.partial(_dilated_kernel, s_len=s_len),
        grid=(batch, A_HEADS // 2, nc),
        in_specs=[pl.BlockSpec((2 * HEAD_PAD, chunk), lambda b, h, c: (h, b * nc + c)),
                  pl.BlockSpec((s_len, LANES), lambda b, h, c: (b, h), pipeline_mode=once),
                  pl.BlockSpec((2 * VT_ROWS, s_len), lambda b, h, c: (h, b), pipeline_mode=once),
                  pl.BlockSpec((2, BIAS_ROWS, TQ_A), lambda b, h, c: (h, 0, 0), pipeline_mode=once)],
        out_specs=pl.BlockSpec((2 * HEAD_DIM, chunk), lambda b, h, c: (h, b * nc + c)),
        out_shape=jax.ShapeDtypeStruct((A_WIDTH, t_len), F32),
        scratch_shapes=[pltpu.VMEM((2, WIN_A, 2 * TQ_A), F32)],
        compiler_params=pltpu.CompilerParams(dimension_semantics=("arbitrary",) * 3, vmem_limit_bytes=VMEM_LIMIT),
        name="dilated",
    )(qaT, ka, vaT, bias)


def _mla_kernel(qT_ref, k_ref, vT_ref, o_ref, acc_ref, s_ref, *, tq, tk, unroll):
    s_len = k_ref.shape[0]
    nk = s_len // tk

    def query_tile(qi, _):
        q0 = pl.multiple_of(qi * tq, tq)
        qT = qT_ref[:, pl.ds(q0, tq)]
        acc_ref[...] = jnp.zeros_like(acc_ref)

        def scores(j, slot):
            off = pl.multiple_of(j * tk, tk)
            s = _dot(k_ref[pl.ds(off, tk), :], qT)
            s_ref[slot, :, :tq] = s
            return jnp.max(s, axis=0, keepdims=True)

        def step(j, slot, m_prev, m_tile):
            m_tile_next = scores(jnp.minimum(j + 1, nk - 1), 1 - slot)
            off = pl.multiple_of(j * tk, tk)
            m_new = jnp.maximum(m_prev, m_tile)
            alpha = jnp.exp2(m_prev - m_new)
            p = jnp.exp2(s_ref[slot, :, :tq] - m_new).astype(BF16)
            acc_ref[...] = acc_ref[...] * alpha + _dot(vT_ref[:, pl.ds(off, tk)], p)
            return m_new, m_tile_next

        def body(i, carry):
            for u in range(unroll):
                carry = step(unroll * i + u, u % 2, *carry)
            return carry

        lax.fori_loop(0, nk // unroll, body, (jnp.full((1, tq), NEG, F32), scores(0, 0)))
        acc = acc_ref[...]
        o_ref[:, pl.ds(q0, tq)] = acc[:B_VDIM] / acc[B_VDIM:B_VDIM + 1]
        return 0

    lax.fori_loop(0, s_len // tq, query_tile, 0)


def _mla(qbT, kb, vbT, batch, s_len):
    tq, tk = _tile(s_len, MLA_TQ), _tile(s_len, MLA_TK)
    nk = s_len // tk
    unroll = min(MLA_UNROLL, nk)
    assert nk % unroll == 0 and unroll % 2 == 0
    t_len = batch * s_len
    return pl.pallas_call(
        functools.partial(_mla_kernel, tq=tq, tk=tk, unroll=unroll),
        grid=(batch, B_HEADS),
        in_specs=[pl.BlockSpec((HEAD_PAD, s_len), lambda b, h: (h, b)),
                  pl.BlockSpec((s_len, HEAD_PAD), lambda b, h: (b, h)),
                  pl.BlockSpec((VT_ROWS, s_len), lambda b, h: (h, b))],
        out_specs=pl.BlockSpec((B_VDIM, s_len), lambda b, h: (h, b)),
        out_shape=jax.ShapeDtypeStruct((B_WIDTH, t_len), F32),
        scratch_shapes=[pltpu.VMEM((VT_ROWS, tq), F32), pltpu.VMEM((2, tk, tq), F32)],
        compiler_params=pltpu.CompilerParams(dimension_semantics=("arbitrary",) * 2, vmem_limit_bytes=VMEM_LIMIT),
        name="mla",
    )(qbT, kb, vbT)


def _merge_kernel(oaT_ref, obT_ref, yc_ref, x_ref, wo_ref, mna_ref, mnb_ref, g_ref, b_ref, o_ref):
    for r in range(0, x_ref.shape[0], MERGE_SUB):
        rows = slice(r, r + MERGE_SUB)
        ya = (_rms_scale(oaT_ref[:, rows].T) * mna_ref[...]).astype(BF16)
        yb = (_rms_scale(obT_ref[:, rows].T) * mnb_ref[...]).astype(BF16)
        y = _dot(jnp.concatenate([ya, yb, yc_ref[rows, :]], axis=1), wo_ref[...])
        o_ref[rows, :] = _layer_norm(ALPHA * x_ref[rows, :] + y, g_ref[...], b_ref[...])


def _merge(oaT, obT, yc, x2, pw):
    t_len = x2.shape[0]
    tm = _tile(t_len, MERGE_TM)
    row = lambda i: (i, 0)
    col = lambda i: (0, i)
    full = lambda a: pl.BlockSpec(a.shape, lambda i: (0, 0))
    return pl.pallas_call(
        _merge_kernel, grid=(t_len // tm,),
        in_specs=[pl.BlockSpec((A_WIDTH, tm), col), pl.BlockSpec((B_WIDTH, tm), col),
                  pl.BlockSpec((tm, C_WIDTH), row), pl.BlockSpec((tm, D_MODEL), row),
                  full(pw['w_out']), full(pw['mn_a']), full(pw['mn_b']), full(pw['ln1_g']), full(pw['ln1_b'])],
        out_specs=pl.BlockSpec((tm, D_MODEL), row),
        out_shape=jax.ShapeDtypeStruct((t_len, D_MODEL), F32),
        compiler_params=pltpu.CompilerParams(dimension_semantics=("arbitrary",), vmem_limit_bytes=VMEM_LIMIT),
        name="merge",
    )(oaT, obT, yc, x2, pw['w_out'], pw['mn_a'], pw['mn_b'], pw['ln1_g'], pw['ln1_b'])


def _ffn_kernel(x_ref, w1_ref, b1_ref, w2_ref, b2_ref, g_ref, b_ref, o_ref):
    for r in range(0, x_ref.shape[0], FFN_SUB):
        rows = slice(r, r + FFN_SUB)
        x = x_ref[rows, :]
        hid = _dot(x.astype(BF16), w1_ref[...]) + b1_ref[...]
        hid = jnp.square(jnp.maximum(hid, 0.0)).astype(BF16)
        f = _dot(hid, w2_ref[...]) + b2_ref[...]
        o_ref[rows, :] = _layer_norm(ALPHA * x + f, g_ref[...], b_ref[...])


def _ffn(x2, pw):
    t_len = x2.shape[0]
    tm = _tile(t_len, FFN_TM)
    fixed = lambda i: (0, 0)
    once = pl.Buffered(1)
    return pl.pallas_call(
        _ffn_kernel, grid=(t_len // tm,),
        in_specs=[pl.BlockSpec((tm, D_MODEL), lambda i: (i, 0)),
                  pl.BlockSpec((D_MODEL, D_FF), fixed, pipeline_mode=once),
                  pl.BlockSpec((1, D_FF), fixed),
                  pl.BlockSpec((D_FF, D_MODEL), fixed, pipeline_mode=once),
                  pl.BlockSpec((1, D_MODEL), fixed), pl.BlockSpec((1, D_MODEL), fixed),
                  pl.BlockSpec((1, D_MODEL), fixed)],
        out_specs=pl.BlockSpec((tm, D_MODEL), lambda i: (i, 0)),
        out_shape=jax.ShapeDtypeStruct((t_len, D_MODEL), F32),
        compiler_params=pltpu.CompilerParams(dimension_semantics=("arbitrary",), vmem_limit_bytes=VMEM_LIMIT),
        name="ffn",
    )(x2, pw['w_ff1'], pw['b_ff1'], pw['w_ff2'], pw['b_ff2'], pw['ln2_g'], pw['ln2_b'])


def _swap_halves(w):
    half = w.shape[-1] // 2
    return jnp.concatenate([w[..., half:], w[..., :half]], axis=-1)


def _pack_layer(l, w_in, q_norm, w_q_up, kv_norm, w_kv_up, sgu_ln_g, sgu_ln_b, sgu_w, sgu_b, mix_norm,
                w_out, ln1_g, ln1_b, w_ff1, b_ff1, w_ff2, b_ff2, ln2_g, ln2_b):
    w = w_in[l]
    zc = lambda n: jnp.zeros((D_MODEL, n), F32)
    off_cq = 3 * A_WIDTH
    off_ckv = off_cq + Q_RANK
    off_kpe = off_ckv + KV_RANK
    off_u = off_kpe + B_ROPE
    kpe = w[:, off_kpe:off_u]
    w_packed = jnp.concatenate([w[:, :off_cq], w[:, off_ckv:off_kpe], w[:, off_cq:off_ckv], zc(64),
                                zc(64), kpe, _swap_halves(kpe), w[:, off_u:]], axis=1)
    wq = w_q_up[l].reshape(Q_RANK, B_HEADS, B_NOPE + B_ROPE)
    wq = jnp.concatenate([wq, _swap_halves(wq[..., B_NOPE:])], axis=-1).reshape(Q_RANK, B_HEADS * HEAD_PAD)
    wkv = w_kv_up[l].reshape(KV_RANK, B_HEADS, B_NOPE + B_VDIM)
    wk = jnp.concatenate([wkv[..., :B_NOPE], jnp.zeros((KV_RANK, B_HEADS, HEAD_PAD - B_NOPE), F32)], axis=-1)
    row = lambda a: a.reshape(1, -1)
    mn = mix_norm[l]
    return {
        'w_in': w_packed.astype(BF16),
        'q_norm': row(q_norm[l]), 'w_q': wq.astype(BF16),
        'kv_norm': row(kv_norm[l]), 'w_k': wk.reshape(KV_RANK, B_HEADS * HEAD_PAD).astype(BF16),
        'w_v': wkv[..., B_NOPE:].reshape(KV_RANK, B_WIDTH).astype(BF16),
        'sgu_g': row(sgu_ln_g[l]), 'sgu_b': row(sgu_ln_b[l]),
        'ws_cat': jnp.concatenate([sgu_w[l][g] for g in range(C_GROUPS)], axis=1).astype(BF16),
        'bs_tab': jnp.repeat(sgu_b[l].T, C_GROUP_DIM, axis=1),
        'mn_a': row(mn[:A_WIDTH]), 'mn_b': row(mn[A_WIDTH:A_WIDTH + B_WIDTH]), 'mn_c': row(mn[A_WIDTH + B_WIDTH:]),
        'w_out': w_out[l].astype(BF16),
        'ln1_g': row(ln1_g[l]), 'ln1_b': row(ln1_b[l]),
        'w_ff1': w_ff1[l].astype(BF16), 'b_ff1': row(b_ff1[l]), 'w_ff2': w_ff2[l].astype(BF16),
        'b_ff2': row(b_ff2[l]), 'ln2_g': row(ln2_g[l]), 'ln2_b': row(ln2_b[l]),
    }


def _rope_tables(s_len):
    inv_freq = ROPE_THETA ** (-jnp.arange(0, B_ROPE, 2, dtype=F32) / B_ROPE)
    ang = jnp.arange(s_len, dtype=F32)[:, None] * inv_freq[None, :]
    cos, sin = jnp.cos(ang), jnp.sin(ang)
    ones = jnp.ones((s_len, B_NOPE), F32)
    zeros = lambda n: jnp.zeros((s_len, n), F32)
    pad = HEAD_PAD - B_NOPE - B_ROPE
    c_tab = jnp.concatenate([cos, cos, zeros(pad)], axis=1)
    s_tab = jnp.concatenate([-sin, sin, zeros(pad)], axis=1)
    q_scale = (B_NOPE + B_ROPE) ** -0.5 * LOG2E
    return {'cq': q_scale * jnp.concatenate([ones, c_tab], axis=1),
            'sq': q_scale * jnp.concatenate([zeros(B_NOPE), s_tab], axis=1),
            'ck': jnp.concatenate([zeros(B_NOPE), c_tab], axis=1),
            'sk': jnp.concatenate([zeros(B_NOPE), s_tab], axis=1)}


def _dilated_bias():
    e = jnp.arange(BIAS_ROWS)[:, None]
    ql = jnp.arange(TQ_A)[None, :]
    dist = jnp.abs(e - 2 * HALO - ql)
    cnt = sum(((dist % d == 0) & (dist <= (w // (2 * d)) * d)).astype(F32) for w, d in DILATED_PATTERNS)
    base = jnp.where(cnt > 0, jnp.log2(jnp.maximum(cnt, 1.0)), NEG)
    base, dist = lax.optimization_barrier((base, dist.astype(F32)))
    slopes = jnp.asarray(2.0 ** (-8.0 * np.arange(1, A_HEADS + 1) / A_HEADS), dtype=F32)
    return base[None] - (slopes * LOG2E)[:, None, None] * dist[None]


def kernel(x, w_in, q_norm, w_q_up, kv_norm, w_kv_up, sgu_ln_g, sgu_ln_b, sgu_w, sgu_b, mix_norm, w_out,
           ln1_g, ln1_b, w_ff1, b_ff1, w_ff2, b_ff2, ln2_g, ln2_b):
    batch, s_len, _ = x.shape
    assert s_len >= WIN_A and s_len % TQ_A == 0
    t_len = batch * s_len
    tabs = _rope_tables(s_len)
    bias = _dilated_bias()
    x2 = x.reshape(t_len, D_MODEL)
    for l in range(DEPTH):
        pw = _pack_layer(l, w_in, q_norm, w_q_up, kv_norm, w_kv_up, sgu_ln_g, sgu_ln_b, sgu_w, sgu_b, mix_norm,
                         w_out, ln1_g, ln1_b, w_ff1, b_ff1, w_ff2, b_ff2, ln2_g, ln2_b)
        qaT, ka, vaT, qbT, kb, vbT, yc = _inproj(x2, pw, tabs, s_len)
        oaT = _dilated(qaT, ka, vaT, bias, batch, s_len)
        obT = _mla(qbT, kb, vbT, batch, s_len)
        x1 = _merge(oaT, obT, yc, x2, pw)
        x2 = _ffn(x1, pw)
    return x2.reshape(batch, s_len, D_MODEL)
```

```python
import functools
import math

import numpy as np
import jax
import jax.numpy as jnp
from jax import lax
from jax.experimental import pallas as pl
from jax.experimental.pallas import tpu as pltpu

BF16 = jnp.bfloat16
F32 = jnp.float32

D_MODEL = 1024
DEPTH = 2
HEAD_DIM = 64
A_HEADS = 6
A_WIDTH = A_HEADS * HEAD_DIM
DILATED_PATTERNS = ((128, 1), (512, 4), (2048, 16))
B_HEADS = 6
B_NOPE = 64
B_ROPE = 32
B_VDIM = 64
Q_RANK = 192
KV_RANK = 128
B_WIDTH = B_HEADS * B_VDIM
ROPE_THETA = 10000.0
C_GROUPS = 4
C_GROUP_DIM = 64
C_WIDTH = C_GROUPS * C_GROUP_DIM
CHUNK = 128
D_FF = 4 * D_MODEL
ALPHA = (2 * DEPTH) ** 0.25
EPS = 1e-5
NEG = -1e30
LOG2E = math.log2(math.e)

LANES = 128
MXU_N = 256
BF16_SUBLANES = 16
VT_ROWS = B_VDIM + BF16_SUBLANES
HEAD_PAD = LANES

P_QA, P_KA, P_VA = 0, A_WIDTH, 2 * A_WIDTH
P_CKV = 3 * A_WIDTH
P_CQ = P_CKV + KV_RANK
P_KPE = P_CQ + 256
P_U = P_KPE + LANES
P_V = P_U + C_WIDTH
P_PACKED = P_V + C_WIDTH

HALO = max((w // (2 * d)) * d for w, d in DILATED_PATTERNS)
TQ_A = 256
WIN_A = TQ_A + 2 * HALO
BIAS_ROWS = WIN_A + 2 * HALO

VMEM_LIMIT = 48 * 1024 * 1024
INPROJ_TM, INPROJ_SUB = 1024, 256
DIL_CHUNK, DIL_UNROLL = 8192, 4
MLA_TQ, MLA_TK, MLA_UNROLL = 1024, 256, 32
MERGE_TM, MERGE_SUB = 1024, 256
FFN_TM, FFN_SUB = 1024, 256


def _rms_scale(x):
    return x * lax.rsqrt(jnp.mean(x * x, axis=-1, keepdims=True) + EPS)


def _layer_norm(x, g, b):
    mu = jnp.mean(x, axis=-1, keepdims=True)
    xc = x - mu
    var = jnp.mean(xc * xc, axis=-1, keepdims=True)
    return xc * lax.rsqrt(var + EPS) * g + b


def _dot(a, b):
    return jnp.dot(a, b, preferred_element_type=F32)


def _tile(n, pref):
    t = min(pref, n)
    assert n % t == 0
    return t


def _inproj_kernel(x_ref, w_ref, qn_ref, wq_ref, kvn_ref, wk_ref, wv_ref, cq_ref, sq_ref, ck_ref, sk_ref,
                   lng_ref, lnb_ref, ws_ref, bs_ref, mnc_ref,
                   qaT_ref, ka_ref, vaT_ref, qbT_ref, kb_ref, vbT_ref, yc_ref):
    sub = INPROJ_SUB
    ones_tile = (lax.broadcasted_iota(jnp.int32, (BF16_SUBLANES, sub), 0) == 0).astype(BF16)
    zeros_half = jnp.zeros((HEAD_DIM, sub), BF16)
    group = lax.broadcasted_iota(jnp.int32, (CHUNK, C_WIDTH), 1) // C_GROUP_DIM

    def put_values_t(vT, out_ref, width, rows):
        for hd in range(A_HEADS):
            out_ref[hd * VT_ROWS:hd * VT_ROWS + width, rows] = vT[hd * width:(hd + 1) * width].astype(BF16)
            out_ref[hd * VT_ROWS + width:(hd + 1) * VT_ROWS, rows] = ones_tile

    for r in range(0, x_ref.shape[0], sub):
        rows = slice(r, r + sub)
        h = _dot(x_ref[rows, :].astype(BF16), w_ref[...])

        qaT = (h[:, P_QA:P_QA + A_WIDTH] * (HEAD_DIM ** -0.5 * LOG2E)).T.astype(BF16)
        for hd in range(A_HEADS):
            q_h = qaT[hd * HEAD_DIM:(hd + 1) * HEAD_DIM]
            lo, hi = (q_h, zeros_half) if hd % 2 == 0 else (zeros_half, q_h)
            qaT_ref[hd * HEAD_PAD:hd * HEAD_PAD + HEAD_DIM, rows] = lo
            qaT_ref[hd * HEAD_PAD + HEAD_DIM:(hd + 1) * HEAD_PAD, rows] = hi
        ka_ref[rows, :] = h[:, P_KA:P_KA + A_WIDTH].astype(BF16)
        put_values_t(h[:, P_VA:P_VA + A_WIDTH].T, vaT_ref, HEAD_DIM, rows)

        c_kv = (_rms_scale(h[:, P_CKV:P_CKV + KV_RANK]) * kvn_ref[...]).astype(BF16)
        kmat = _dot(c_kv, wk_ref[...])
        vmat = _dot(c_kv, wv_ref[...])
        g_k = h[:, P_KPE:P_KPE + LANES]
        kpe = g_k * ck_ref[rows, :] + pltpu.roll(g_k, 96, 1) * sk_ref[rows, :]
        c_q = (_rms_scale(h[:, P_CQ:P_CQ + Q_RANK]) * qn_ref[...]).astype(BF16)
        qmat = _dot(c_q, wq_ref[...])
        cq_t, sq_t = cq_ref[rows, :], sq_ref[rows, :]
        for hd in range(B_HEADS):
            sl = slice(hd * HEAD_PAD, (hd + 1) * HEAD_PAD)
            kb_ref[rows, sl] = (kmat[:, sl] + kpe).astype(BF16)
            g_q = qmat[:, sl]
            q_h = g_q * cq_t + pltpu.roll(g_q, 96, 1) * sq_t
            qbT_ref[sl, rows] = q_h.T.astype(BF16)
        put_values_t(vmat.T, vbT_ref, B_VDIM, rows)

        z = jax.nn.gelu(h[:, P_U:P_U + 2 * C_WIDTH])
        u = z[:, :C_WIDTH]
        v = _layer_norm(z[:, C_WIDTH:], lng_ref[...], lnb_ref[...])
        for c in range(sub // CHUNK):
            crows = slice(c * CHUNK, (c + 1) * CHUNK)
            vc = v[crows]
            stack = jnp.concatenate([jnp.where(group == g, vc, 0.0) for g in range(C_GROUPS)], axis=0).astype(BF16)
            mixed = _dot(ws_ref[...], stack) + bs_ref[...]
            yc = u[crows] * mixed
            yc_ref[r + c * CHUNK:r + (c + 1) * CHUNK, :] = (_rms_scale(yc) * mnc_ref[...]).astype(BF16)


def _inproj(x2, pw, tabs, s_len):
    t_len = x2.shape[0]
    tm = _tile(s_len, INPROJ_TM)
    nt = t_len // tm
    ns = s_len // tm
    row = lambda i: (i, 0)
    col = lambda i: (0, i)
    fixed = lambda i: (0, 0)
    pos = lambda i: (i % ns, 0)
    full = lambda a: pl.BlockSpec(a.shape, fixed)
    tab = pl.BlockSpec((tm, LANES), pos)
    in_specs = [pl.BlockSpec((tm, D_MODEL), row), full(pw['w_in']), full(pw['q_norm']), full(pw['w_q']),
                full(pw['kv_norm']), full(pw['w_k']), full(pw['w_v']), tab, tab, tab, tab,
                full(pw['sgu_g']), full(pw['sgu_b']), full(pw['ws_cat']), full(pw['bs_tab']), full(pw['mn_c'])]
    out_shape = [jax.ShapeDtypeStruct((A_HEADS * HEAD_PAD, t_len), BF16),
                 jax.ShapeDtypeStruct((t_len, A_WIDTH), BF16),
                 jax.ShapeDtypeStruct((A_HEADS * VT_ROWS, t_len), BF16),
                 jax.ShapeDtypeStruct((B_HEADS * HEAD_PAD, t_len), BF16),
                 jax.ShapeDtypeStruct((t_len, B_HEADS * HEAD_PAD), BF16),
                 jax.ShapeDtypeStruct((B_HEADS * VT_ROWS, t_len), BF16),
                 jax.ShapeDtypeStruct((t_len, C_WIDTH), BF16)]
    out_specs = [pl.BlockSpec((A_HEADS * HEAD_PAD, tm), col),
                 pl.BlockSpec((tm, A_WIDTH), row),
                 pl.BlockSpec((A_HEADS * VT_ROWS, tm), col),
                 pl.BlockSpec((B_HEADS * HEAD_PAD, tm), col),
                 pl.BlockSpec((tm, B_HEADS * HEAD_PAD), row),
                 pl.BlockSpec((B_HEADS * VT_ROWS, tm), col),
                 pl.BlockSpec((tm, C_WIDTH), row)]
    return pl.pallas_call(
        _inproj_kernel, grid=(nt,), in_specs=in_specs, out_specs=out_specs, out_shape=out_shape,
        compiler_params=pltpu.CompilerParams(dimension_semantics=("arbitrary",), vmem_limit_bytes=VMEM_LIMIT),
        name="inproj",
    )(x2, pw['w_in'], pw['q_norm'], pw['w_q'], pw['kv_norm'], pw['w_k'], pw['w_v'],
      tabs['cq'], tabs['sq'], tabs['ck'], tabs['sk'],
      pw['sgu_g'], pw['sgu_b'], pw['ws_cat'], pw['bs_tab'], pw['mn_c'])


def _dilated_kernel(qT_ref, k_ref, vT_ref, bias_ref, o_ref, s_ref, *, s_len):
    nblk = qT_ref.shape[1] // TQ_A
    first = pl.program_id(2) * nblk

    def window(i):
        t0 = (first + i) * TQ_A
        w0 = jnp.clip(t0 - HALO, 0, s_len - WIN_A)
        b0 = w0 - t0 + 2 * HALO
        return pl.multiple_of(w0, TQ_A), pl.multiple_of(b0, TQ_A), pl.multiple_of(i * TQ_A, TQ_A)

    def scores(i, slot):
        w0, b0, q0 = window(i)
        qT = jnp.concatenate([qT_ref[hh * HEAD_PAD:(hh + 1) * HEAD_PAD, pl.ds(q0, TQ_A)] for hh in range(2)], axis=1)
        s = _dot(k_ref[pl.ds(w0, WIN_A), :], qT)
        m = []
        for hh in range(2):
            s_h = s[:, hh * TQ_A:(hh + 1) * TQ_A] + bias_ref[hh, pl.ds(b0, WIN_A), :]
            s_ref[slot, :, hh * TQ_A:(hh + 1) * TQ_A] = s_h
            m.append(jnp.max(s_h, axis=0, keepdims=True))
        return tuple(m)

    def finish(i, slot, m):
        w0, _, q0 = window(i)
        for hh in range(2):
            p = jnp.exp2(s_ref[slot, :, hh * TQ_A:(hh + 1) * TQ_A] - m[hh]).astype(BF16)
            acc = _dot(vT_ref[hh * VT_ROWS:(hh + 1) * VT_ROWS, pl.ds(w0, WIN_A)], p)
            o_ref[hh * HEAD_DIM:(hh + 1) * HEAD_DIM, pl.ds(q0, TQ_A)] = acc[:HEAD_DIM] / acc[HEAD_DIM:HEAD_DIM + 1]

    def body(i, m):
        for u in range(DIL_UNROLL):
            j = DIL_UNROLL * i + u
            m_next = scores(jnp.minimum(j + 1, nblk - 1), (u + 1) % 2)
            finish(j, u % 2, m)
            m = m_next
        return m

    lax.fori_loop(0, nblk // DIL_UNROLL, body, scores(0, 0))


def _dilated(qaT, ka, vaT, bias, batch, s_len):
    chunk = _tile(s_len, DIL_CHUNK)
    assert (chunk // TQ_A) % DIL_UNROLL == 0 and DIL_UNROLL % 2 == 0
    nc = s_len // chunk
    t_len = batch * s_len
    once = pl.Buffered(1)
    return pl.pallas_call(
        functools.partial(_dilated_kernel, s_len=s_len),
        grid=(batch, A_HEADS // 2, nc),
        in_specs=[pl.BlockSpec((2 * HEAD_PAD, chunk), lambda b, h, c: (h, b * nc + c)),
                  pl.BlockSpec((s_len, LANES), lambda b, h, c: (b, h), pipeline_mode=once),
                  pl.BlockSpec((2 * VT_ROWS, s_len), lambda b, h, c: (h, b), pipeline_mode=once),
                  pl.BlockSpec((2, BIAS_ROWS, TQ_A), lambda b, h, c: (h, 0, 0), pipeline_mode=once)],
        out_specs=pl.BlockSpec((2 * HEAD_DIM, chunk), lambda b, h, c: (h, b * nc + c)),
        out_shape=jax.ShapeDtypeStruct((A_WIDTH, t_len), F32),
        scratch_shapes=[pltpu.VMEM((2, WIN_A, 2 * TQ_A), F32)],
        compiler_params=pltpu.CompilerParams(dimension_semantics=("arbitrary",) * 3, vmem_limit_bytes=VMEM_LIMIT),
        name="dilated",
    )(qaT, ka, vaT, bias)


def _mla_kernel(qT_ref, k_ref, vT_ref, o_ref, acc_ref, s_ref, *, tq, tk, unroll):
    s_len = k_ref.shape[0]
    nk = s_len // tk

    def query_tile(qi, _):
        q0 = pl.multiple_of(qi * tq, tq)
        qT = qT_ref[:, pl.ds(q0, tq)]
        acc_ref[...] = jnp.zeros_like(acc_ref)

        def scores(j, slot):
            off = pl.multiple_of(j * tk, tk)
            s = _dot(k_ref[pl.ds(off, tk), :], qT)
            s_ref[slot, :, :tq] = s
            return jnp.max(s, axis=0, keepdims=True)

        def step(j, slot, m_prev, m_tile):
            off = pl.multiple_of(j * tk, tk)
            off_n = pl.multiple_of(jnp.minimum(j + 1, nk - 1) * tk, tk)
            m_new = jnp.maximum(m_prev, m_tile)
            alpha = jnp.exp2(m_prev - m_new)
            k_next = k_ref[pl.ds(off_n, tk), :]
            vT = vT_ref[:, pl.ds(off, tk)]
            m_parts = []
            for n in range(tq // MXU_N):
                cols = slice(n * MXU_N, (n + 1) * MXU_N)
                s_n = _dot(k_next, qT[:, cols])
                s_ref[1 - slot, :, cols] = s_n
                m_parts.append(jnp.max(s_n, axis=0, keepdims=True))
                p_n = jnp.exp2(s_ref[slot, :, cols] - m_new[:, cols]).astype(BF16)
                acc_ref[:, cols] = acc_ref[:, cols] * alpha[:, cols] + _dot(vT, p_n)
            return m_new, jnp.concatenate(m_parts, axis=1)

        def body(i, carry):
            for u in range(unroll):
                carry = step(unroll * i + u, u % 2, *carry)
            return carry

        lax.fori_loop(0, nk // unroll, body, (jnp.full((1, tq), NEG, F32), scores(0, 0)))
        acc = acc_ref[...]
        o_ref[:, pl.ds(q0, tq)] = acc[:B_VDIM] / acc[B_VDIM:B_VDIM + 1]
        return 0

    lax.fori_loop(0, s_len // tq, query_tile, 0)


def _mla(qbT, kb, vbT, batch, s_len):
    tq, tk = _tile(s_len, MLA_TQ), _tile(s_len, MLA_TK)
    nk = s_len // tk
    unroll = min(MLA_UNROLL, nk)
    assert nk % unroll == 0 and unroll % 2 == 0
    t_len = batch * s_len
    return pl.pallas_call(
        functools.partial(_mla_kernel, tq=tq, tk=tk, unroll=unroll),
        grid=(batch, B_HEADS),
        in_specs=[pl.BlockSpec((HEAD_PAD, s_len), lambda b, h: (h, b)),
                  pl.BlockSpec((s_len, HEAD_PAD), lambda b, h: (b, h)),
                  pl.BlockSpec((VT_ROWS, s_len), lambda b, h: (h, b))],
        out_specs=pl.BlockSpec((B_VDIM, s_len), lambda b, h: (h, b)),
        out_shape=jax.ShapeDtypeStruct((B_WIDTH, t_len), F32),
        scratch_shapes=[pltpu.VMEM((VT_ROWS, tq), F32), pltpu.VMEM((2, tk, tq), F32)],
        compiler_params=pltpu.CompilerParams(dimension_semantics=("arbitrary",) * 2, vmem_limit_bytes=VMEM_LIMIT),
        name="mla",
    )(qbT, kb, vbT)


def _merge_kernel(oaT_ref, obT_ref, yc_ref, x_ref, wo_ref, mna_ref, mnb_ref, g_ref, b_ref, o_ref):
    for r in range(0, x_ref.shape[0], MERGE_SUB):
        rows = slice(r, r + MERGE_SUB)
        ya = (_rms_scale(oaT_ref[:, rows].T) * mna_ref[...]).astype(BF16)
        yb = (_rms_scale(obT_ref[:, rows].T) * mnb_ref[...]).astype(BF16)
        y = _dot(jnp.concatenate([ya, yb, yc_ref[rows, :]], axis=1), wo_ref[...])
        o_ref[rows, :] = _layer_norm(ALPHA * x_ref[rows, :] + y, g_ref[...], b_ref[...])


def _merge(oaT, obT, yc, x2, pw):
    t_len = x2.shape[0]
    tm = _tile(t_len, MERGE_TM)
    row = lambda i: (i, 0)
    col = lambda i: (0, i)
    full = lambda a: pl.BlockSpec(a.shape, lambda i: (0, 0))
    return pl.pallas_call(
        _merge_kernel, grid=(t_len // tm,),
        in_specs=[pl.BlockSpec((A_WIDTH, tm), col), pl.BlockSpec((B_WIDTH, tm), col),
                  pl.BlockSpec((tm, C_WIDTH), row), pl.BlockSpec((tm, D_MODEL), row),
                  full(pw['w_out']), full(pw['mn_a']), full(pw['mn_b']), full(pw['ln1_g']), full(pw['ln1_b'])],
        out_specs=pl.BlockSpec((tm, D_MODEL), row),
        out_shape=jax.ShapeDtypeStruct((t_len, D_MODEL), F32),
        compiler_params=pltpu.CompilerParams(dimension_semantics=("arbitrary",), vmem_limit_bytes=VMEM_LIMIT),
        name="merge",
    )(oaT, obT, yc, x2, pw['w_out'], pw['mn_a'], pw['mn_b'], pw['ln1_g'], pw['ln1_b'])


def _ffn_kernel(x_ref, w1_ref, b1_ref, w2_ref, b2_ref, g_ref, b_ref, o_ref):
    for r in range(0, x_ref.shape[0], FFN_SUB):
        rows = slice(r, r + FFN_SUB)
        x = x_ref[rows, :]
        hid = _dot(x.astype(BF16), w1_ref[...]) + b1_ref[...]
        hid = jnp.square(jnp.maximum(hid, 0.0)).astype(BF16)
        f = _dot(hid, w2_ref[...]) + b2_ref[...]
        o_ref[rows, :] = _layer_norm(ALPHA * x + f, g_ref[...], b_ref[...])


def _ffn(x2, pw):
    t_len = x2.shape[0]
    tm = _tile(t_len, FFN_TM)
    fixed = lambda i: (0, 0)
    once = pl.Buffered(1)
    return pl.pallas_call(
        _ffn_kernel, grid=(t_len // tm,),
        in_specs=[pl.BlockSpec((tm, D_MODEL), lambda i: (i, 0)),
                  pl.BlockSpec((D_MODEL, D_FF), fixed, pipeline_mode=once),
                  pl.BlockSpec((1, D_FF), fixed),
                  pl.BlockSpec((D_FF, D_MODEL), fixed, pipeline_mode=once),
                  pl.BlockSpec((1, D_MODEL), fixed), pl.BlockSpec((1, D_MODEL), fixed),
                  pl.BlockSpec((1, D_MODEL), fixed)],
        out_specs=pl.BlockSpec((tm, D_MODEL), lambda i: (i, 0)),
        out_shape=jax.ShapeDtypeStruct((t_len, D_MODEL), F32),
        compiler_params=pltpu.CompilerParams(dimension_semantics=("arbitrary",), vmem_limit_bytes=VMEM_LIMIT),
        name="ffn",
    )(x2, pw['w_ff1'], pw['b_ff1'], pw['w_ff2'], pw['b_ff2'], pw['ln2_g'], pw['ln2_b'])


def _swap_halves(w):
    half = w.shape[-1] // 2
    return jnp.concatenate([w[..., half:], w[..., :half]], axis=-1)


def _pack_layer(l, w_in, q_norm, w_q_up, kv_norm, w_kv_up, sgu_ln_g, sgu_ln_b, sgu_w, sgu_b, mix_norm,
                w_out, ln1_g, ln1_b, w_ff1, b_ff1, w_ff2, b_ff2, ln2_g, ln2_b):
    w = w_in[l]
    zc = lambda n: jnp.zeros((D_MODEL, n), F32)
    off_cq = 3 * A_WIDTH
    off_ckv = off_cq + Q_RANK
    off_kpe = off_ckv + KV_RANK
    off_u = off_kpe + B_ROPE
    kpe = w[:, off_kpe:off_u]
    w_packed = jnp.concatenate([w[:, :off_cq], w[:, off_ckv:off_kpe], w[:, off_cq:off_ckv], zc(64),
                                zc(64), kpe, _swap_halves(kpe), w[:, off_u:]], axis=1)
    wq = w_q_up[l].reshape(Q_RANK, B_HEADS, B_NOPE + B_ROPE)
    wq = jnp.concatenate([wq, _swap_halves(wq[..., B_NOPE:])], axis=-1).reshape(Q_RANK, B_HEADS * HEAD_PAD)
    wkv = w_kv_up[l].reshape(KV_RANK, B_HEADS, B_NOPE + B_VDIM)
    wk = jnp.concatenate([wkv[..., :B_NOPE], jnp.zeros((KV_RANK, B_HEADS, HEAD_PAD - B_NOPE), F32)], axis=-1)
    row = lambda a: a.reshape(1, -1)
    mn = mix_norm[l]
    return {
        'w_in': w_packed.astype(BF16),
        'q_norm': row(q_norm[l]), 'w_q': wq.astype(BF16),
        'kv_norm': row(kv_norm[l]), 'w_k': wk.reshape(KV_RANK, B_HEADS * HEAD_PAD).astype(BF16),
        'w_v': wkv[..., B_NOPE:].reshape(KV_RANK, B_WIDTH).astype(BF16),
        'sgu_g': row(sgu_ln_g[l]), 'sgu_b': row(sgu_ln_b[l]),
        'ws_cat': jnp.concatenate([sgu_w[l][g] for g in range(C_GROUPS)], axis=1).astype(BF16),
        'bs_tab': jnp.repeat(sgu_b[l].T, C_GROUP_DIM, axis=1),
        'mn_a': row(mn[:A_WIDTH]), 'mn_b': row(mn[A_WIDTH:A_WIDTH + B_WIDTH]), 'mn_c': row(mn[A_WIDTH + B_WIDTH:]),
        'w_out': w_out[l].astype(BF16),
        'ln1_g': row(ln1_g[l]), 'ln1_b': row(ln1_b[l]),
        'w_ff1': w_ff1[l].astype(BF16), 'b_ff1': row(b_ff1[l]), 'w_ff2': w_ff2[l].astype(BF16),
        'b_ff2': row(b_ff2[l]), 'ln2_g': row(ln2_g[l]), 'ln2_b': row(ln2_b[l]),
    }


def _rope_tables(s_len):
    inv_freq = ROPE_THETA ** (-jnp.arange(0, B_ROPE, 2, dtype=F32) / B_ROPE)
    ang = jnp.arange(s_len, dtype=F32)[:, None] * inv_freq[None, :]
    cos, sin = jnp.cos(ang), jnp.sin(ang)
    ones = jnp.ones((s_len, B_NOPE), F32)
    zeros = lambda n: jnp.zeros((s_len, n), F32)
    pad = HEAD_PAD - B_NOPE - B_ROPE
    c_tab = jnp.concatenate([cos, cos, zeros(pad)], axis=1)
    s_tab = jnp.concatenate([-sin, sin, zeros(pad)], axis=1)
    q_scale = (B_NOPE + B_ROPE) ** -0.5 * LOG2E
    return {'cq': q_scale * jnp.concatenate([ones, c_tab], axis=1),
            'sq': q_scale * jnp.concatenate([zeros(B_NOPE), s_tab], axis=1),
            'ck': jnp.concatenate([zeros(B_NOPE), c_tab], axis=1),
            'sk': jnp.concatenate([zeros(B_NOPE), s_tab], axis=1)}


def _dilated_bias():
    e = jnp.arange(BIAS_ROWS)[:, None]
    ql = jnp.arange(TQ_A)[None, :]
    dist = jnp.abs(e - 2 * HALO - ql)
    cnt = sum(((dist % d == 0) & (dist <= (w // (2 * d)) * d)).astype(F32) for w, d in DILATED_PATTERNS)
    base = jnp.where(cnt > 0, jnp.log2(jnp.maximum(cnt, 1.0)), NEG)
    base, dist = lax.optimization_barrier((base, dist.astype(F32)))
    slopes = jnp.asarray(2.0 ** (-8.0 * np.arange(1, A_HEADS + 1) / A_HEADS), dtype=F32)
    return base[None] - (slopes * LOG2E)[:, None, None] * dist[None]


def kernel(x, w_in, q_norm, w_q_up, kv_norm, w_kv_up, sgu_ln_g, sgu_ln_b, sgu_w, sgu_b, mix_norm, w_out,
           ln1_g, ln1_b, w_ff1, b_ff1, w_ff2, b_ff2, ln2_g, ln2_b):
    batch, s_len, _ = x.shape
    assert s_len >= WIN_A and s_len % TQ_A == 0
    t_len = batch * s_len
    tabs = _rope_tables(s_len)
    bias = _dilated_bias()
    x2 = x.reshape(t_len, D_MODEL)
    for l in range(DEPTH):
        pw = _pack_layer(l, w_in, q_norm, w_q_up, kv_norm, w_kv_up, sgu_ln_g, sgu_ln_b, sgu_w, sgu_b, mix_norm,
                         w_out, ln1_g, ln1_b, w_ff1, b_ff1, w_ff2, b_ff2, ln2_g, ln2_b)
        qaT, ka, vaT, qbT, kb, vbT, yc = _inproj(x2, pw, tabs, s_len)
        oaT = _dilated(qaT, ka, vaT, bias, batch, s_len)
        obT = _mla(qbT, kb, vbT, batch, s_len)
        x1 = _merge(oaT, obT, yc, x2, pw)
        x2 = _ffn(x1, pw)
    return x2.reshape(batch, s_len, D_MODEL)
```

```python
import functools
import math

import numpy as np
import jax
import jax.numpy as jnp
from jax import lax
from jax.experimental import pallas as pl
from jax.experimental.pallas import tpu as pltpu

BF16 = jnp.bfloat16
F32 = jnp.float32

D_MODEL = 1024
DEPTH = 2
HEAD_DIM = 64
A_HEADS = 6
A_WIDTH = A_HEADS * HEAD_DIM
DILATED_PATTERNS = ((128, 1), (512, 4), (2048, 16))
B_HEADS = 6
B_NOPE = 64
B_ROPE = 32
B_VDIM = 64
Q_RANK = 192
KV_RANK = 128
B_WIDTH = B_HEADS * B_VDIM
ROPE_THETA = 10000.0
C_GROUPS = 4
C_GROUP_DIM = 64
C_WIDTH = C_GROUPS * C_GROUP_DIM
CHUNK = 128
D_FF = 4 * D_MODEL
ALPHA = (2 * DEPTH) ** 0.25
EPS = 1e-5
NEG = -1e30
LOG2E = math.log2(math.e)

LANES = 128
MXU_N = 256
BF16_SUBLANES = 16
VT_ROWS = B_VDIM + BF16_SUBLANES
HEAD_PAD = LANES

P_QA, P_KA, P_VA = 0, A_WIDTH, 2 * A_WIDTH
P_CKV = 3 * A_WIDTH
P_CQ = P_CKV + KV_RANK
P_KPE = P_CQ + 256
P_U = P_KPE + LANES
P_V = P_U + C_WIDTH
P_PACKED = P_V + C_WIDTH

HALO = max((w // (2 * d)) * d for w, d in DILATED_PATTERNS)
TQ_A = 256
WIN_A = TQ_A + 2 * HALO
BIAS_ROWS = WIN_A + 2 * HALO

VMEM_LIMIT = 48 * 1024 * 1024
INPROJ_TM, INPROJ_SUB = 1024, 256
DIL_CHUNK, DIL_UNROLL = 8192, 4
MLA_TQ, MLA_TK, MLA_UNROLL = 1024, 256, 32
MERGE_TM, MERGE_SUB = 1024, 256
FFN_TM, FFN_SUB = 1024, 256


def _rms_scale(x):
    return x * lax.rsqrt(jnp.mean(x * x, axis=-1, keepdims=True) + EPS)


def _layer_norm(x, g, b):
    mu = jnp.mean(x, axis=-1, keepdims=True)
    xc = x - mu
    var = jnp.mean(xc * xc, axis=-1, keepdims=True)
    return xc * lax.rsqrt(var + EPS) * g + b


def _dot(a, b):
    return jnp.dot(a, b, preferred_element_type=F32)


def _tile(n, pref):
    t = min(pref, n)
    assert n % t == 0
    return t


def _inproj_kernel(x_ref, w_ref, qn_ref, wq_ref, kvn_ref, wk_ref, wv_ref, cq_ref, sq_ref, ck_ref, sk_ref,
                   lng_ref, lnb_ref, ws_ref, bs_ref, mnc_ref,
                   qaT_ref, ka_ref, vaT_ref, qbT_ref, kb_ref, vbT_ref, yc_ref):
    sub = INPROJ_SUB
    ones_tile = (lax.broadcasted_iota(jnp.int32, (BF16_SUBLANES, sub), 0) == 0).astype(BF16)
    zeros_half = jnp.zeros((HEAD_DIM, sub), BF16)
    group = lax.broadcasted_iota(jnp.int32, (CHUNK, C_WIDTH), 1) // C_GROUP_DIM

    def put_values_t(vT, out_ref, width, rows):
        for hd in range(A_HEADS):
            out_ref[hd * VT_ROWS:hd * VT_ROWS + width, rows] = vT[hd * width:(hd + 1) * width].astype(BF16)
            out_ref[hd * VT_ROWS + width:(hd + 1) * VT_ROWS, rows] = ones_tile

    for r in range(0, x_ref.shape[0], sub):
        rows = slice(r, r + sub)
        h = _dot(x_ref[rows, :].astype(BF16), w_ref[...])

        qaT = (h[:, P_QA:P_QA + A_WIDTH] * (HEAD_DIM ** -0.5 * LOG2E)).T.astype(BF16)
        for hd in range(A_HEADS):
            q_h = qaT[hd * HEAD_DIM:(hd + 1) * HEAD_DIM]
            lo, hi = (q_h, zeros_half) if hd % 2 == 0 else (zeros_half, q_h)
            qaT_ref[hd * HEAD_PAD:hd * HEAD_PAD + HEAD_DIM, rows] = lo
            qaT_ref[hd * HEAD_PAD + HEAD_DIM:(hd + 1) * HEAD_PAD, rows] = hi
        ka_ref[rows, :] = h[:, P_KA:P_KA + A_WIDTH].astype(BF16)
        put_values_t(h[:, P_VA:P_VA + A_WIDTH].T, vaT_ref, HEAD_DIM, rows)

        c_kv = (_rms_scale(h[:, P_CKV:P_CKV + KV_RANK]) * kvn_ref[...]).astype(BF16)
        kmat = _dot(c_kv, wk_ref[...])
        vmat = _dot(c_kv, wv_ref[...])
        g_k = h[:, P_KPE:P_KPE + LANES]
        kpe = g_k * ck_ref[rows, :] + pltpu.roll(g_k, 96, 1) * sk_ref[rows, :]
        c_q = (_rms_scale(h[:, P_CQ:P_CQ + Q_RANK]) * qn_ref[...]).astype(BF16)
        qmat = _dot(c_q, wq_ref[...])
        cq_t, sq_t = cq_ref[rows, :], sq_ref[rows, :]
        for hd in range(B_HEADS):
            sl = slice(hd * HEAD_PAD, (hd + 1) * HEAD_PAD)
            kb_ref[rows, sl] = (kmat[:, sl] + kpe).astype(BF16)
            g_q = qmat[:, sl]
            q_h = g_q * cq_t + pltpu.roll(g_q, 96, 1) * sq_t
            qbT_ref[sl, rows] = q_h.T.astype(BF16)
        put_values_t(vmat.T, vbT_ref, B_VDIM, rows)

        z = jax.nn.gelu(h[:, P_U:P_U + 2 * C_WIDTH])
        u = z[:, :C_WIDTH]
        v = _layer_norm(z[:, C_WIDTH:], lng_ref[...], lnb_ref[...])
        for c in range(sub // CHUNK):
            crows = slice(c * CHUNK, (c + 1) * CHUNK)
            vc = v[crows]
            stack = jnp.concatenate([jnp.where(group == g, vc, 0.0) for g in range(C_GROUPS)], axis=0).astype(BF16)
            mixed = _dot(ws_ref[...], stack) + bs_ref[...]
            yc = u[crows] * mixed
            yc_ref[r + c * CHUNK:r + (c + 1) * CHUNK, :] = (_rms_scale(yc) * mnc_ref[...]).astype(BF16)


def _inproj(x2, pw, tabs, s_len):
    t_len = x2.shape[0]
    tm = _tile(s_len, INPROJ_TM)
    nt = t_len // tm
    ns = s_len // tm
    row = lambda i: (i, 0)
    col = lambda i: (0, i)
    fixed = lambda i: (0, 0)
    pos = lambda i: (i % ns, 0)
    full = lambda a: pl.BlockSpec(a.shape, fixed)
    tab = pl.BlockSpec((tm, LANES), pos)
    in_specs = [pl.BlockSpec((tm, D_MODEL), row), full(pw['w_in']), full(pw['q_norm']), full(pw['w_q']),
                full(pw['kv_norm']), full(pw['w_k']), full(pw['w_v']), tab, tab, tab, tab,
                full(pw['sgu_g']), full(pw['sgu_b']), full(pw['ws_cat']), full(pw['bs_tab']), full(pw['mn_c'])]
    out_shape = [jax.ShapeDtypeStruct((A_HEADS * HEAD_PAD, t_len), BF16),
                 jax.ShapeDtypeStruct((t_len, A_WIDTH), BF16),
                 jax.ShapeDtypeStruct((A_HEADS * VT_ROWS, t_len), BF16),
                 jax.ShapeDtypeStruct((B_HEADS * HEAD_PAD, t_len), BF16),
                 jax.ShapeDtypeStruct((t_len, B_HEADS * HEAD_PAD), BF16),
                 jax.ShapeDtypeStruct((B_HEADS * VT_ROWS, t_len), BF16),
                 jax.ShapeDtypeStruct((t_len, C_WIDTH), BF16)]
    out_specs = [pl.BlockSpec((A_HEADS * HEAD_PAD, tm), col),
                 pl.BlockSpec((tm, A_WIDTH), row),
                 pl.BlockSpec((A_HEADS * VT_ROWS, tm), col),
                 pl.BlockSpec((B_HEADS * HEAD_PAD, tm), col),
                 pl.BlockSpec((tm, B_HEADS * HEAD_PAD), row),
                 pl.BlockSpec((B_HEADS * VT_ROWS, tm), col),
                 pl.BlockSpec((tm, C_WIDTH), row)]
    return pl.pallas_call(
        _inproj_kernel, grid=(nt,), in_specs=in_specs, out_specs=out_specs, out_shape=out_shape,
        compiler_params=pltpu.CompilerParams(dimension_semantics=("arbitrary",), vmem_limit_bytes=VMEM_LIMIT),
        name="inproj",
    )(x2, pw['w_in'], pw['q_norm'], pw['w_q'], pw['kv_norm'], pw['w_k'], pw['w_v'],
      tabs['cq'], tabs['sq'], tabs['ck'], tabs['sk'],
      pw['sgu_g'], pw['sgu_b'], pw['ws_cat'], pw['bs_tab'], pw['mn_c'])


def _dilated_kernel(qT_ref, k_ref, vT_ref, bias_ref, o_ref, s_ref, *, s_len):
    nblk = qT_ref.shape[1] // TQ_A
    first = pl.program_id(2) * nblk

    def window(i):
        t0 = (first + i) * TQ_A
        w0 = jnp.clip(t0 - HALO, 0, s_len - WIN_A)
        b0 = w0 - t0 + 2 * HALO
        return pl.multiple_of(w0, TQ_A), pl.multiple_of(b0, TQ_A), pl.multiple_of(i * TQ_A, TQ_A)

    def scores(i, slot, hh):
        w0, b0, q0 = window(i)
        qT = qT_ref[hh * HEAD_PAD:(hh + 1) * HEAD_PAD, pl.ds(q0, TQ_A)]
        s_h = _dot(k_ref[pl.ds(w0, WIN_A), :], qT) + bias_ref[hh, pl.ds(b0, WIN_A), :]
        s_ref[slot, :, hh * TQ_A:(hh + 1) * TQ_A] = s_h
        return jnp.max(s_h, axis=0, keepdims=True)

    def finish(i, slot, hh, m_h):
        w0, _, q0 = window(i)
        p = jnp.exp2(s_ref[slot, :, hh * TQ_A:(hh + 1) * TQ_A] - m_h).astype(BF16)
        acc = _dot(vT_ref[hh * VT_ROWS:(hh + 1) * VT_ROWS, pl.ds(w0, WIN_A)], p)
        o_ref[hh * HEAD_DIM:(hh + 1) * HEAD_DIM, pl.ds(q0, TQ_A)] = acc[:HEAD_DIM] / acc[HEAD_DIM:HEAD_DIM + 1]

    def body(i, m):
        for u in range(DIL_UNROLL):
            j = DIL_UNROLL * i + u
            m_next = []
            for hh in range(2):
                m_next.append(scores(jnp.minimum(j + 1, nblk - 1), (u + 1) % 2, hh))
                finish(j, u % 2, hh, m[hh])
            m = tuple(m_next)
        return m

    lax.fori_loop(0, nblk // DIL_UNROLL, body, (scores(0, 0, 0), scores(0, 0, 1)))


def _dilated(qaT, ka, vaT, bias, batch, s_len):
    chunk = _tile(s_len, DIL_CHUNK)
    assert (chunk // TQ_A) % DIL_UNROLL == 0 and DIL_UNROLL % 2 == 0
    nc = s_len // chunk
    t_len = batch * s_len
    once = pl.Buffered(1)
    return pl.pallas_call(
        functools.partial(_dilated_kernel, s_len=s_len),
        grid=(batch, A_HEADS // 2, nc),
        in_specs=[pl.BlockSpec((2 * HEAD_PAD, chunk), lambda b, h, c: (h, b * nc + c)),
                  pl.BlockSpec((s_len, LANES), lambda b, h, c: (b, h), pipeline_mode=once),
                  pl.BlockSpec((2 * VT_ROWS, s_len), lambda b, h, c: (h, b), pipeline_mode=once),
                  pl.BlockSpec((2, BIAS_ROWS, TQ_A), lambda b, h, c: (h, 0, 0), pipeline_mode=once)],
        out_specs=pl.BlockSpec((2 * HEAD_DIM, chunk), lambda b, h, c: (h, b * nc + c)),
        out_shape=jax.ShapeDtypeStruct((A_WIDTH, t_len), F32),
        scratch_shapes=[pltpu.VMEM((2, WIN_A, 2 * TQ_A), F32)],
        compiler_params=pltpu.CompilerParams(dimension_semantics=("arbitrary",) * 3, vmem_limit_bytes=VMEM_LIMIT),
        name="dilated",
    )(qaT, ka, vaT, bias)


def _mla_kernel(qT_ref, k_ref, vT_ref, o_ref, acc_ref, s_ref, *, tq, tk, unroll):
    s_len = k_ref.shape[0]
    nk = s_len // tk

    def query_tile(qi, _):
        q0 = pl.multiple_of(qi * tq, tq)
        qT = qT_ref[:, pl.ds(q0, tq)]
        acc_ref[...] = jnp.zeros_like(acc_ref)

        def scores(j, slot):
            off = pl.multiple_of(j * tk, tk)
            s = _dot(k_ref[pl.ds(off, tk), :], qT)
            s_ref[slot, :, :tq] = s
            return jnp.max(s, axis=0, keepdims=True)

        def step(j, slot, m_prev, m_tile):
            off = pl.multiple_of(j * tk, tk)
            off_n = pl.multiple_of(jnp.minimum(j + 1, nk - 1) * tk, tk)
            m_new = jnp.maximum(m_prev, m_tile)
            alpha = jnp.exp2(m_prev - m_new)
            k_next = k_ref[pl.ds(off_n, tk), :]
            vT = vT_ref[:, pl.ds(off, tk)]
            m_parts = []
            for n in range(tq // MXU_N):
                cols = slice(n * MXU_N, (n + 1) * MXU_N)
                s_n = _dot(k_next, qT[:, cols])
                s_ref[1 - slot, :, cols] = s_n
                m_parts.append(jnp.max(s_n, axis=0, keepdims=True))
                p_n = jnp.exp2(s_ref[slot, :, cols] - m_new[:, cols]).astype(BF16)
                acc_ref[:, cols] = acc_ref[:, cols] * alpha[:, cols] + _dot(vT, p_n)
            return m_new, jnp.concatenate(m_parts, axis=1)

        def body(i, carry):
            for u in range(unroll):
                carry = step(unroll * i + u, u % 2, *carry)
            return carry

        lax.fori_loop(0, nk // unroll, body, (jnp.full((1, tq), NEG, F32), scores(0, 0)))
        acc = acc_ref[...]
        o_ref[:, pl.ds(q0, tq)] = acc[:B_VDIM] / acc[B_VDIM:B_VDIM + 1]
        return 0

    lax.fori_loop(0, s_len // tq, query_tile, 0)


def _mla(qbT, kb, vbT, batch, s_len):
    tq, tk = _tile(s_len, MLA_TQ), _tile(s_len, MLA_TK)
    nk = s_len // tk
    unroll = min(MLA_UNROLL, nk)
    assert nk % unroll == 0 and unroll % 2 == 0
    t_len = batch * s_len
    return pl.pallas_call(
        functools.partial(_mla_kernel, tq=tq, tk=tk, unroll=unroll),
        grid=(batch, B_HEADS),
        in_specs=[pl.BlockSpec((HEAD_PAD, s_len), lambda b, h: (h, b)),
                  pl.BlockSpec((s_len, HEAD_PAD), lambda b, h: (b, h)),
                  pl.BlockSpec((VT_ROWS, s_len), lambda b, h: (h, b))],
        out_specs=pl.BlockSpec((B_VDIM, s_len), lambda b, h: (h, b)),
        out_shape=jax.ShapeDtypeStruct((B_WIDTH, t_len), F32),
        scratch_shapes=[pltpu.VMEM((VT_ROWS, tq), F32), pltpu.VMEM((2, tk, tq), F32)],
        compiler_params=pltpu.CompilerParams(dimension_semantics=("arbitrary",) * 2, vmem_limit_bytes=VMEM_LIMIT),
        name="mla",
    )(qbT, kb, vbT)


def _merge_kernel(oaT_ref, obT_ref, yc_ref, x_ref, wo_ref, mna_ref, mnb_ref, g_ref, b_ref, o_ref):
    for r in range(0, x_ref.shape[0], MERGE_SUB):
        rows = slice(r, r + MERGE_SUB)
        ya = (_rms_scale(oaT_ref[:, rows].T) * mna_ref[...]).astype(BF16)
        yb = (_rms_scale(obT_ref[:, rows].T) * mnb_ref[...]).astype(BF16)
        y = _dot(jnp.concatenate([ya, yb, yc_ref[rows, :]], axis=1), wo_ref[...])
        o_ref[rows, :] = _layer_norm(ALPHA * x_ref[rows, :] + y, g_ref[...], b_ref[...])


def _merge(oaT, obT, yc, x2, pw):
    t_len = x2.shape[0]
    tm = _tile(t_len, MERGE_TM)
    row = lambda i: (i, 0)
    col = lambda i: (0, i)
    full = lambda a: pl.BlockSpec(a.shape, lambda i: (0, 0))
    return pl.pallas_call(
        _merge_kernel, grid=(t_len // tm,),
        in_specs=[pl.BlockSpec((A_WIDTH, tm), col), pl.BlockSpec((B_WIDTH, tm), col),
                  pl.BlockSpec((tm, C_WIDTH), row), pl.BlockSpec((tm, D_MODEL), row),
                  full(pw['w_out']), full(pw['mn_a']), full(pw['mn_b']), full(pw['ln1_g']), full(pw['ln1_b'])],
        out_specs=pl.BlockSpec((tm, D_MODEL), row),
        out_shape=jax.ShapeDtypeStruct((t_len, D_MODEL), F32),
        compiler_params=pltpu.CompilerParams(dimension_semantics=("arbitrary",), vmem_limit_bytes=VMEM_LIMIT),
        name="merge",
    )(oaT, obT, yc, x2, pw['w_out'], pw['mn_a'], pw['mn_b'], pw['ln1_g'], pw['ln1_b'])


def _ffn_kernel(x_ref, w1_ref, b1_ref, w2_ref, b2_ref, g_ref, b_ref, o_ref):
    for r in range(0, x_ref.shape[0], FFN_SUB):
        rows = slice(r, r + FFN_SUB)
        x = x_ref[rows, :]
        hid = _dot(x.astype(BF16), w1_ref[...]) + b1_ref[...]
        hid = jnp.square(jnp.maximum(hid, 0.0)).astype(BF16)
        f = _dot(hid, w2_ref[...]) + b2_ref[...]
        o_ref[rows, :] = _layer_norm(ALPHA * x + f, g_ref[...], b_ref[...])


def _ffn(x2, pw):
    t_len = x2.shape[0]
    tm = _tile(t_len, FFN_TM)
    fixed = lambda i: (0, 0)
    once = pl.Buffered(1)
    return pl.pallas_call(
        _ffn_kernel, grid=(t_len // tm,),
        in_specs=[pl.BlockSpec((tm, D_MODEL), lambda i: (i, 0)),
                  pl.BlockSpec((D_MODEL, D_FF), fixed, pipeline_mode=once),
                  pl.BlockSpec((1, D_FF), fixed),
                  pl.BlockSpec((D_FF, D_MODEL), fixed, pipeline_mode=once),
                  pl.BlockSpec((1, D_MODEL), fixed), pl.BlockSpec((1, D_MODEL), fixed),
                  pl.BlockSpec((1, D_MODEL), fixed)],
        out_specs=pl.BlockSpec((tm, D_MODEL), lambda i: (i, 0)),
        out_shape=jax.ShapeDtypeStruct((t_len, D_MODEL), F32),
        compiler_params=pltpu.CompilerParams(dimension_semantics=("arbitrary",), vmem_limit_bytes=VMEM_LIMIT),
        name="ffn",
    )(x2, pw['w_ff1'], pw['b_ff1'], pw['w_ff2'], pw['b_ff2'], pw['ln2_g'], pw['ln2_b'])


def _swap_halves(w):
    half = w.shape[-1] // 2
    return jnp.concatenate([w[..., half:], w[..., :half]], axis=-1)


def _pack_layer(l, w_in, q_norm, w_q_up, kv_norm, w_kv_up, sgu_ln_g, sgu_ln_b, sgu_w, sgu_b, mix_norm,
                w_out, ln1_g, ln1_b, w_ff1, b_ff1, w_ff2, b_ff2, ln2_g, ln2_b):
    w = w_in[l]
    zc = lambda n: jnp.zeros((D_MODEL, n), F32)
    off_cq = 3 * A_WIDTH
    off_ckv = off_cq + Q_RANK
    off_kpe = off_ckv + KV_RANK
    off_u = off_kpe + B_ROPE
    kpe = w[:, off_kpe:off_u]
    w_packed = jnp.concatenate([w[:, :off_cq], w[:, off_ckv:off_kpe], w[:, off_cq:off_ckv], zc(64),
                                zc(64), kpe, _swap_halves(kpe), w[:, off_u:]], axis=1)
    wq = w_q_up[l].reshape(Q_RANK, B_HEADS, B_NOPE + B_ROPE)
    wq = jnp.concatenate([wq, _swap_halves(wq[..., B_NOPE:])], axis=-1).reshape(Q_RANK, B_HEADS * HEAD_PAD)
    wkv = w_kv_up[l].reshape(KV_RANK, B_HEADS, B_NOPE + B_VDIM)
    wk = jnp.concatenate([wkv[..., :B_NOPE], jnp.zeros((KV_RANK, B_HEADS, HEAD_PAD - B_NOPE), F32)], axis=-1)
    row = lambda a: a.reshape(1, -1)
    mn = mix_norm[l]
    return {
        'w_in': w_packed.astype(BF16),
        'q_norm': row(q_norm[l]), 'w_q': wq.astype(BF16),
        'kv_norm': row(kv_norm[l]), 'w_k': wk.reshape(KV_RANK, B_HEADS * HEAD_PAD).astype(BF16),
        'w_v': wkv[..., B_NOPE:].reshape(KV_RANK, B_WIDTH).astype(BF16),
        'sgu_g': row(sgu_ln_g[l]), 'sgu_b': row(sgu_ln_b[l]),
        'ws_cat': jnp.concatenate([sgu_w[l][g] for g in range(C_GROUPS)], axis=1).astype(BF16),
        'bs_tab': jnp.repeat(sgu_b[l].T, C_GROUP_DIM, axis=1),
        'mn_a': row(mn[:A_WIDTH]), 'mn_b': row(mn[A_WIDTH:A_WIDTH + B_WIDTH]), 'mn_c': row(mn[A_WIDTH + B_WIDTH:]),
        'w_out': w_out[l].astype(BF16),
        'ln1_g': row(ln1_g[l]), 'ln1_b': row(ln1_b[l]),
        'w_ff1': w_ff1[l].astype(BF16), 'b_ff1': row(b_ff1[l]), 'w_ff2': w_ff2[l].astype(BF16),
        'b_ff2': row(b_ff2[l]), 'ln2_g': row(ln2_g[l]), 'ln2_b': row(ln2_b[l]),
    }


def _rope_tables(s_len):
    inv_freq = ROPE_THETA ** (-jnp.arange(0, B_ROPE, 2, dtype=F32) / B_ROPE)
    ang = jnp.arange(s_len, dtype=F32)[:, None] * inv_freq[None, :]
    cos, sin = jnp.cos(ang), jnp.sin(ang)
    ones = jnp.ones((s_len, B_NOPE), F32)
    zeros = lambda n: jnp.zeros((s_len, n), F32)
    pad = HEAD_PAD - B_NOPE - B_ROPE
    c_tab = jnp.concatenate([cos, cos, zeros(pad)], axis=1)
    s_tab = jnp.concatenate([-sin, sin, zeros(pad)], axis=1)
    q_scale = (B_NOPE + B_ROPE) ** -0.5 * LOG2E
    return {'cq': q_scale * jnp.concatenate([ones, c_tab], axis=1),
            'sq': q_scale * jnp.concatenate([zeros(B_NOPE), s_tab], axis=1),
            'ck': jnp.concatenate([zeros(B_NOPE), c_tab], axis=1),
            'sk': jnp.concatenate([zeros(B_NOPE), s_tab], axis=1)}


def _dilated_bias():
    e = jnp.arange(BIAS_ROWS)[:, None]
    ql = jnp.arange(TQ_A)[None, :]
    dist = jnp.abs(e - 2 * HALO - ql)
    cnt = sum(((dist % d == 0) & (dist <= (w // (2 * d)) * d)).astype(F32) for w, d in DILATED_PATTERNS)
    base = jnp.where(cnt > 0, jnp.log2(jnp.maximum(cnt, 1.0)), NEG)
    base, dist = lax.optimization_barrier((base, dist.astype(F32)))
    slopes = jnp.asarray(2.0 ** (-8.0 * np.arange(1, A_HEADS + 1) / A_HEADS), dtype=F32)
    return base[None] - (slopes * LOG2E)[:, None, None] * dist[None]


def kernel(x, w_in, q_norm, w_q_up, kv_norm, w_kv_up, sgu_ln_g, sgu_ln_b, sgu_w, sgu_b, mix_norm, w_out,
           ln1_g, ln1_b, w_ff1, b_ff1, w_ff2, b_ff2, ln2_g, ln2_b):
    batch, s_len, _ = x.shape
    assert s_len >= WIN_A and s_len % TQ_A == 0
    t_len = batch * s_len
    tabs = _rope_tables(s_len)
    bias = _dilated_bias()
    x2 = x.reshape(t_len, D_MODEL)
    for l in range(DEPTH):
        pw = _pack_layer(l, w_in, q_norm, w_q_up, kv_norm, w_kv_up, sgu_ln_g, sgu_ln_b, sgu_w, sgu_b, mix_norm,
                         w_out, ln1_g, ln1_b, w_ff1, b_ff1, w_ff2, b_ff2, ln2_g, ln2_b)
        qaT, ka, vaT, qbT, kb, vbT, yc = _inproj(x2, pw, tabs, s_len)
        oaT = _dilated(qaT, ka, vaT, bias, batch, s_len)
        obT = _mla(qbT, kb, vbT, batch, s_len)
        x1 = _merge(oaT, obT, yc, x2, pw)
        x2 = _ffn(x1, pw)
    return x2.reshape(batch, s_len, D_MODEL)
```

```python
import functools
import math

import numpy as np
import jax
import jax.numpy as jnp
from jax import lax
from jax.experimental import pallas as pl
from jax.experimental.pallas import tpu as pltpu

BF16 = jnp.bfloat16
F32 = jnp.float32

D_MODEL = 1024
DEPTH = 2
HEAD_DIM = 64
A_HEADS = 6
A_WIDTH = A_HEADS * HEAD_DIM
DILATED_PATTERNS = ((128, 1), (512, 4), (2048, 16))
B_HEADS = 6
B_NOPE = 64
B_ROPE = 32
B_VDIM = 64
Q_RANK = 192
KV_RANK = 128
B_WIDTH = B_HEADS * B_VDIM
ROPE_THETA = 10000.0
C_GROUPS = 4
C_GROUP_DIM = 64
C_WIDTH = C_GROUPS * C_GROUP_DIM
CHUNK = 128
D_FF = 4 * D_MODEL
ALPHA = (2 * DEPTH) ** 0.25
EPS = 1e-5
NEG = -1e30
LOG2E = math.log2(math.e)

LANES = 128
MXU_N = 256
BF16_SUBLANES = 16
VT_ROWS = B_VDIM + BF16_SUBLANES
HEAD_PAD = LANES

P_QA, P_KA, P_VA = 0, A_WIDTH, 2 * A_WIDTH
P_CKV = 3 * A_WIDTH
P_CQ = P_CKV + KV_RANK
P_KPE = P_CQ + 256
P_U = P_KPE + LANES
P_V = P_U + C_WIDTH
P_PACKED = P_V + C_WIDTH

HALO = max((w // (2 * d)) * d for w, d in DILATED_PATTERNS)
TQ_A = 256
WIN_A = TQ_A + 2 * HALO
BIAS_ROWS = WIN_A + 2 * HALO

VMEM_LIMIT = 48 * 1024 * 1024
INPROJ_TM, INPROJ_SUB = 1024, 256
DIL_CHUNK, DIL_UNROLL = 8192, 4
MLA_TQ, MLA_TK, MLA_UNROLL = 2048, 256, 32
MERGE_TM, MERGE_SUB = 1024, 256
FFN_TM, FFN_SUB = 1024, 256


def _rms_scale(x):
    return x * lax.rsqrt(jnp.mean(x * x, axis=-1, keepdims=True) + EPS)


def _layer_norm(x, g, b):
    mu = jnp.mean(x, axis=-1, keepdims=True)
    xc = x - mu
    var = jnp.mean(xc * xc, axis=-1, keepdims=True)
    return xc * lax.rsqrt(var + EPS) * g + b


def _dot(a, b):
    return jnp.dot(a, b, preferred_element_type=F32)


def _tile(n, pref):
    t = min(pref, n)
    assert n % t == 0
    return t


def _inproj_kernel(x_ref, w_ref, qn_ref, wq_ref, kvn_ref, wk_ref, wv_ref, cq_ref, sq_ref, ck_ref, sk_ref,
                   lng_ref, lnb_ref, ws_ref, bs_ref, mnc_ref,
                   qaT_ref, ka_ref, vaT_ref, qbT_ref, kb_ref, vbT_ref, yc_ref):
    sub = INPROJ_SUB
    ones_tile = (lax.broadcasted_iota(jnp.int32, (BF16_SUBLANES, sub), 0) == 0).astype(BF16)
    zeros_half = jnp.zeros((HEAD_DIM, sub), BF16)
    group = lax.broadcasted_iota(jnp.int32, (CHUNK, C_WIDTH), 1) // C_GROUP_DIM

    def put_values_t(vT, out_ref, width, rows):
        for hd in range(A_HEADS):
            out_ref[hd * VT_ROWS:hd * VT_ROWS + width, rows] = vT[hd * width:(hd + 1) * width].astype(BF16)
            out_ref[hd * VT_ROWS + width:(hd + 1) * VT_ROWS, rows] = ones_tile

    for r in range(0, x_ref.shape[0], sub):
        rows = slice(r, r + sub)
        h = _dot(x_ref[rows, :].astype(BF16), w_ref[...])

        qaT = (h[:, P_QA:P_QA + A_WIDTH] * (HEAD_DIM ** -0.5 * LOG2E)).T.astype(BF16)
        for hd in range(A_HEADS):
            q_h = qaT[hd * HEAD_DIM:(hd + 1) * HEAD_DIM]
            lo, hi = (q_h, zeros_half) if hd % 2 == 0 else (zeros_half, q_h)
            qaT_ref[hd * HEAD_PAD:hd * HEAD_PAD + HEAD_DIM, rows] = lo
            qaT_ref[hd * HEAD_PAD + HEAD_DIM:(hd + 1) * HEAD_PAD, rows] = hi
        ka_ref[rows, :] = h[:, P_KA:P_KA + A_WIDTH].astype(BF16)
        put_values_t(h[:, P_VA:P_VA + A_WIDTH].T, vaT_ref, HEAD_DIM, rows)

        c_kv = (_rms_scale(h[:, P_CKV:P_CKV + KV_RANK]) * kvn_ref[...]).astype(BF16)
        kmat = _dot(c_kv, wk_ref[...])
        vmat = _dot(c_kv, wv_ref[...])
        g_k = h[:, P_KPE:P_KPE + LANES]
        kpe = g_k * ck_ref[rows, :] + pltpu.roll(g_k, 96, 1) * sk_ref[rows, :]
        c_q = (_rms_scale(h[:, P_CQ:P_CQ + Q_RANK]) * qn_ref[...]).astype(BF16)
        qmat = _dot(c_q, wq_ref[...])
        cq_t, sq_t = cq_ref[rows, :], sq_ref[rows, :]
        for hd in range(B_HEADS):
            sl = slice(hd * HEAD_PAD, (hd + 1) * HEAD_PAD)
            kb_ref[rows, sl] = (kmat[:, sl] + kpe).astype(BF16)
            g_q = qmat[:, sl]
            q_h = g_q * cq_t + pltpu.roll(g_q, 96, 1) * sq_t
            qbT_ref[sl, rows] = q_h.T.astype(BF16)
        put_values_t(vmat.T, vbT_ref, B_VDIM, rows)

        z = jax.nn.gelu(h[:, P_U:P_U + 2 * C_WIDTH])
        u = z[:, :C_WIDTH]
        v = _layer_norm(z[:, C_WIDTH:], lng_ref[...], lnb_ref[...])
        for c in range(sub // CHUNK):
            crows = slice(c * CHUNK, (c + 1) * CHUNK)
            vc = v[crows]
            stack = jnp.concatenate([jnp.where(group == g, vc, 0.0) for g in range(C_GROUPS)], axis=0).astype(BF16)
            mixed = _dot(ws_ref[...], stack) + bs_ref[...]
            yc = u[crows] * mixed
            yc_ref[r + c * CHUNK:r + (c + 1) * CHUNK, :] = (_rms_scale(yc) * mnc_ref[...]).astype(BF16)


def _inproj(x2, pw, tabs, s_len):
    t_len = x2.shape[0]
    tm = _tile(s_len, INPROJ_TM)
    nt = t_len // tm
    ns = s_len // tm
    row = lambda i: (i, 0)
    col = lambda i: (0, i)
    fixed = lambda i: (0, 0)
    pos = lambda i: (i % ns, 0)
    full = lambda a: pl.BlockSpec(a.shape, fixed)
    tab = pl.BlockSpec((tm, LANES), pos)
    in_specs = [pl.BlockSpec((tm, D_MODEL), row), full(pw['w_in']), full(pw['q_norm']), full(pw['w_q']),
                full(pw['kv_norm']), full(pw['w_k']), full(pw['w_v']), tab, tab, tab, tab,
                full(pw['sgu_g']), full(pw['sgu_b']), full(pw['ws_cat']), full(pw['bs_tab']), full(pw['mn_c'])]
    out_shape = [jax.ShapeDtypeStruct((A_HEADS * HEAD_PAD, t_len), BF16),
                 jax.ShapeDtypeStruct((t_len, A_WIDTH), BF16),
                 jax.ShapeDtypeStruct((A_HEADS * VT_ROWS, t_len), BF16),
                 jax.ShapeDtypeStruct((B_HEADS * HEAD_PAD, t_len), BF16),
                 jax.ShapeDtypeStruct((t_len, B_HEADS * HEAD_PAD), BF16),
                 jax.ShapeDtypeStruct((B_HEADS * VT_ROWS, t_len), BF16),
                 jax.ShapeDtypeStruct((t_len, C_WIDTH), BF16)]
    out_specs = [pl.BlockSpec((A_HEADS * HEAD_PAD, tm), col),
                 pl.BlockSpec((tm, A_WIDTH), row),
                 pl.BlockSpec((A_HEADS * VT_ROWS, tm), col),
                 pl.BlockSpec((B_HEADS * HEAD_PAD, tm), col),
                 pl.BlockSpec((tm, B_HEADS * HEAD_PAD), row),
                 pl.BlockSpec((B_HEADS * VT_ROWS, tm), col),
                 pl.BlockSpec((tm, C_WIDTH), row)]
    return pl.pallas_call(
        _inproj_kernel, grid=(nt,), in_specs=in_specs, out_specs=out_specs, out_shape=out_shape,
        compiler_params=pltpu.CompilerParams(dimension_semantics=("arbitrary",), vmem_limit_bytes=VMEM_LIMIT),
        name="inproj",
    )(x2, pw['w_in'], pw['q_norm'], pw['w_q'], pw['kv_norm'], pw['w_k'], pw['w_v'],
      tabs['cq'], tabs['sq'], tabs['ck'], tabs['sk'],
      pw['sgu_g'], pw['sgu_b'], pw['ws_cat'], pw['bs_tab'], pw['mn_c'])


def _dilated_kernel(qT_ref, k_ref, vT_ref, bias_ref, o_ref, s_ref, *, s_len):
    nblk = qT_ref.shape[1] // TQ_A
    first = pl.program_id(2) * nblk

    def window(i):
        t0 = (first + i) * TQ_A
        w0 = jnp.clip(t0 - HALO, 0, s_len - WIN_A)
        b0 = w0 - t0 + 2 * HALO
        return pl.multiple_of(w0, TQ_A), pl.multiple_of(b0, TQ_A), pl.multiple_of(i * TQ_A, TQ_A)

    def scores(i, slot, hh):
        w0, b0, q0 = window(i)
        qT = qT_ref[hh * HEAD_PAD:(hh + 1) * HEAD_PAD, pl.ds(q0, TQ_A)]
        s_h = _dot(k_ref[pl.ds(w0, WIN_A), :], qT) + bias_ref[hh, pl.ds(b0, WIN_A), :]
        s_ref[slot, :, hh * TQ_A:(hh + 1) * TQ_A] = s_h
        return jnp.max(s_h, axis=0, keepdims=True)

    def finish(i, slot, hh, m_h):
        w0, _, q0 = window(i)
        p = jnp.exp2(s_ref[slot, :, hh * TQ_A:(hh + 1) * TQ_A] - m_h).astype(BF16)
        acc = _dot(vT_ref[hh * VT_ROWS:(hh + 1) * VT_ROWS, pl.ds(w0, WIN_A)], p)
        o_ref[hh * HEAD_DIM:(hh + 1) * HEAD_DIM, pl.ds(q0, TQ_A)] = acc[:HEAD_DIM] / acc[HEAD_DIM:HEAD_DIM + 1]

    def body(i, m):
        for u in range(DIL_UNROLL):
            j = DIL_UNROLL * i + u
            m_next = []
            for hh in range(2):
                m_next.append(scores(jnp.minimum(j + 1, nblk - 1), (u + 1) % 2, hh))
                finish(j, u % 2, hh, m[hh])
            m = tuple(m_next)
        return m

    lax.fori_loop(0, nblk // DIL_UNROLL, body, (scores(0, 0, 0), scores(0, 0, 1)))


def _dilated(qaT, ka, vaT, bias, batch, s_len):
    chunk = _tile(s_len, DIL_CHUNK)
    assert (chunk // TQ_A) % DIL_UNROLL == 0 and DIL_UNROLL % 2 == 0
    nc = s_len // chunk
    t_len = batch * s_len
    once = pl.Buffered(1)
    return pl.pallas_call(
        functools.partial(_dilated_kernel, s_len=s_len),
        grid=(batch, A_HEADS // 2, nc),
        in_specs=[pl.BlockSpec((2 * HEAD_PAD, chunk), lambda b, h, c: (h, b * nc + c)),
                  pl.BlockSpec((s_len, LANES), lambda b, h, c: (b, h), pipeline_mode=once),
                  pl.BlockSpec((2 * VT_ROWS, s_len), lambda b, h, c: (h, b), pipeline_mode=once),
                  pl.BlockSpec((2, BIAS_ROWS, TQ_A), lambda b, h, c: (h, 0, 0), pipeline_mode=once)],
        out_specs=pl.BlockSpec((2 * HEAD_DIM, chunk), lambda b, h, c: (h, b * nc + c)),
        out_shape=jax.ShapeDtypeStruct((A_WIDTH, t_len), F32),
        scratch_shapes=[pltpu.VMEM((2, WIN_A, 2 * TQ_A), F32)],
        compiler_params=pltpu.CompilerParams(dimension_semantics=("arbitrary",) * 3, vmem_limit_bytes=VMEM_LIMIT),
        name="dilated",
    )(qaT, ka, vaT, bias)


def _mla_kernel(qT_ref, k_ref, vT_ref, o_ref, acc_ref, s_ref, *, tq, tk, unroll):
    s_len = k_ref.shape[0]
    nk = s_len // tk

    def query_tile(qi, _):
        q0 = pl.multiple_of(qi * tq, tq)
        qT = qT_ref[:, pl.ds(q0, tq)]
        acc_ref[...] = jnp.zeros_like(acc_ref)

        def scores(j, slot):
            off = pl.multiple_of(j * tk, tk)
            s = _dot(k_ref[pl.ds(off, tk), :], qT)
            s_ref[slot, :, :tq] = s
            return jnp.max(s, axis=0, keepdims=True)

        def step(j, slot, m_prev, m_tile):
            off = pl.multiple_of(j * tk, tk)
            off_n = pl.multiple_of(jnp.minimum(j + 1, nk - 1) * tk, tk)
            m_new = jnp.maximum(m_prev, m_tile)
            alpha = jnp.exp2(m_prev - m_new)
            k_next = k_ref[pl.ds(off_n, tk), :]
            vT = vT_ref[:, pl.ds(off, tk)]
            m_parts = []
            for n in range(tq // MXU_N):
                cols = slice(n * MXU_N, (n + 1) * MXU_N)
                s_n = _dot(k_next, qT[:, cols])
                s_ref[1 - slot, :, cols] = s_n
                m_parts.append(jnp.max(s_n, axis=0, keepdims=True))
                p_n = jnp.exp2(s_ref[slot, :, cols] - m_new[:, cols]).astype(BF16)
                acc_ref[:, cols] = acc_ref[:, cols] * alpha[:, cols] + _dot(vT, p_n)
            return m_new, jnp.concatenate(m_parts, axis=1)

        def body(i, carry):
            for u in range(unroll):
                carry = step(unroll * i + u, u % 2, *carry)
            return carry

        lax.fori_loop(0, nk // unroll, body, (jnp.full((1, tq), NEG, F32), scores(0, 0)))
        acc = acc_ref[...]
        o_ref[:, pl.ds(q0, tq)] = acc[:B_VDIM] / acc[B_VDIM:B_VDIM + 1]
        return 0

    lax.fori_loop(0, s_len // tq, query_tile, 0)


def _mla(qbT, kb, vbT, batch, s_len):
    tq, tk = _tile(s_len, MLA_TQ), _tile(s_len, MLA_TK)
    nk = s_len // tk
    unroll = min(MLA_UNROLL, nk)
    assert nk % unroll == 0 and unroll % 2 == 0
    t_len = batch * s_len
    return pl.pallas_call(
        functools.partial(_mla_kernel, tq=tq, tk=tk, unroll=unroll),
        grid=(batch, B_HEADS),
        in_specs=[pl.BlockSpec((HEAD_PAD, s_len), lambda b, h: (h, b)),
                  pl.BlockSpec((s_len, HEAD_PAD), lambda b, h: (b, h)),
                  pl.BlockSpec((VT_ROWS, s_len), lambda b, h: (h, b))],
        out_specs=pl.BlockSpec((B_VDIM, s_len), lambda b, h: (h, b)),
        out_shape=jax.ShapeDtypeStruct((B_WIDTH, t_len), F32),
        scratch_shapes=[pltpu.VMEM((VT_ROWS, tq), F32), pltpu.VMEM((2, tk, tq), F32)],
        compiler_params=pltpu.CompilerParams(dimension_semantics=("arbitrary",) * 2, vmem_limit_bytes=VMEM_LIMIT),
        name="mla",
    )(qbT, kb, vbT)


def _merge_kernel(oaT_ref, obT_ref, yc_ref, x_ref, wo_ref, mna_ref, mnb_ref, g_ref, b_ref, o_ref):
    for r in range(0, x_ref.shape[0], MERGE_SUB):
        rows = slice(r, r + MERGE_SUB)
        ya = (_rms_scale(oaT_ref[:, rows].T) * mna_ref[...]).astype(BF16)
        yb = (_rms_scale(obT_ref[:, rows].T) * mnb_ref[...]).astype(BF16)
        y = _dot(jnp.concatenate([ya, yb, yc_ref[rows, :]], axis=1), wo_ref[...])
        o_ref[rows, :] = _layer_norm(ALPHA * x_ref[rows, :] + y, g_ref[...], b_ref[...])


def _merge(oaT, obT, yc, x2, pw):
    t_len = x2.shape[0]
    tm = _tile(t_len, MERGE_TM)
    row = lambda i: (i, 0)
    col = lambda i: (0, i)
    full = lambda a: pl.BlockSpec(a.shape, lambda i: (0, 0))
    return pl.pallas_call(
        _merge_kernel, grid=(t_len // tm,),
        in_specs=[pl.BlockSpec((A_WIDTH, tm), col), pl.BlockSpec((B_WIDTH, tm), col),
                  pl.BlockSpec((tm, C_WIDTH), row), pl.BlockSpec((tm, D_MODEL), row),
                  full(pw['w_out']), full(pw['mn_a']), full(pw['mn_b']), full(pw['ln1_g']), full(pw['ln1_b'])],
        out_specs=pl.BlockSpec((tm, D_MODEL), row),
        out_shape=jax.ShapeDtypeStruct((t_len, D_MODEL), F32),
        compiler_params=pltpu.CompilerParams(dimension_semantics=("arbitrary",), vmem_limit_bytes=VMEM_LIMIT),
        name="merge",
    )(oaT, obT, yc, x2, pw['w_out'], pw['mn_a'], pw['mn_b'], pw['ln1_g'], pw['ln1_b'])


def _ffn_kernel(x_ref, w1_ref, b1_ref, w2_ref, b2_ref, g_ref, b_ref, o_ref):
    for r in range(0, x_ref.shape[0], FFN_SUB):
        rows = slice(r, r + FFN_SUB)
        x = x_ref[rows, :]
        hid = _dot(x.astype(BF16), w1_ref[...]) + b1_ref[...]
        hid = jnp.square(jnp.maximum(hid, 0.0)).astype(BF16)
        f = _dot(hid, w2_ref[...]) + b2_ref[...]
        o_ref[rows, :] = _layer_norm(ALPHA * x + f, g_ref[...], b_ref[...])


def _ffn(x2, pw):
    t_len = x2.shape[0]
    tm = _tile(t_len, FFN_TM)
    fixed = lambda i: (0, 0)
    once = pl.Buffered(1)
    return pl.pallas_call(
        _ffn_kernel, grid=(t_len // tm,),
        in_specs=[pl.BlockSpec((tm, D_MODEL), lambda i: (i, 0)),
                  pl.BlockSpec((D_MODEL, D_FF), fixed, pipeline_mode=once),
                  pl.BlockSpec((1, D_FF), fixed),
                  pl.BlockSpec((D_FF, D_MODEL), fixed, pipeline_mode=once),
                  pl.BlockSpec((1, D_MODEL), fixed), pl.BlockSpec((1, D_MODEL), fixed),
                  pl.BlockSpec((1, D_MODEL), fixed)],
        out_specs=pl.BlockSpec((tm, D_MODEL), lambda i: (i, 0)),
        out_shape=jax.ShapeDtypeStruct((t_len, D_MODEL), F32),
        compiler_params=pltpu.CompilerParams(dimension_semantics=("arbitrary",), vmem_limit_bytes=VMEM_LIMIT),
        name="ffn",
    )(x2, pw['w_ff1'], pw['b_ff1'], pw['w_ff2'], pw['b_ff2'], pw['ln2_g'], pw['ln2_b'])


def _swap_halves(w):
    half = w.shape[-1] // 2
    return jnp.concatenate([w[..., half:], w[..., :half]], axis=-1)


def _pack_layer(l, w_in, q_norm, w_q_up, kv_norm, w_kv_up, sgu_ln_g, sgu_ln_b, sgu_w, sgu_b, mix_norm,
                w_out, ln1_g, ln1_b, w_ff1, b_ff1, w_ff2, b_ff2, ln2_g, ln2_b):
    w = w_in[l]
    zc = lambda n: jnp.zeros((D_MODEL, n), F32)
    off_cq = 3 * A_WIDTH
    off_ckv = off_cq + Q_RANK
    off_kpe = off_ckv + KV_RANK
    off_u = off_kpe + B_ROPE
    kpe = w[:, off_kpe:off_u]
    w_packed = jnp.concatenate([w[:, :off_cq], w[:, off_ckv:off_kpe], w[:, off_cq:off_ckv], zc(64),
                                zc(64), kpe, _swap_halves(kpe), w[:, off_u:]], axis=1)
    wq = w_q_up[l].reshape(Q_RANK, B_HEADS, B_NOPE + B_ROPE)
    wq = jnp.concatenate([wq, _swap_halves(wq[..., B_NOPE:])], axis=-1).reshape(Q_RANK, B_HEADS * HEAD_PAD)
    wkv = w_kv_up[l].reshape(KV_RANK, B_HEADS, B_NOPE + B_VDIM)
    wk = jnp.concatenate([wkv[..., :B_NOPE], jnp.zeros((KV_RANK, B_HEADS, HEAD_PAD - B_NOPE), F32)], axis=-1)
    row = lambda a: a.reshape(1, -1)
    mn = mix_norm[l]
    return {
        'w_in': w_packed.astype(BF16),
        'q_norm': row(q_norm[l]), 'w_q': wq.astype(BF16),
        'kv_norm': row(kv_norm[l]), 'w_k': wk.reshape(KV_RANK, B_HEADS * HEAD_PAD).astype(BF16),
        'w_v': wkv[..., B_NOPE:].reshape(KV_RANK, B_WIDTH).astype(BF16),
        'sgu_g': row(sgu_ln_g[l]), 'sgu_b': row(sgu_ln_b[l]),
        'ws_cat': jnp.concatenate([sgu_w[l][g] for g in range(C_GROUPS)], axis=1).astype(BF16),
        'bs_tab': jnp.repeat(sgu_b[l].T, C_GROUP_DIM, axis=1),
        'mn_a': row(mn[:A_WIDTH]), 'mn_b': row(mn[A_WIDTH:A_WIDTH + B_WIDTH]), 'mn_c': row(mn[A_WIDTH + B_WIDTH:]),
        'w_out': w_out[l].astype(BF16),
        'ln1_g': row(ln1_g[l]), 'ln1_b': row(ln1_b[l]),
        'w_ff1': w_ff1[l].astype(BF16), 'b_ff1': row(b_ff1[l]), 'w_ff2': w_ff2[l].astype(BF16),
        'b_ff2': row(b_ff2[l]), 'ln2_g': row(ln2_g[l]), 'ln2_b': row(ln2_b[l]),
    }


def _rope_tables(s_len):
    inv_freq = ROPE_THETA ** (-jnp.arange(0, B_ROPE, 2, dtype=F32) / B_ROPE)
    ang = jnp.arange(s_len, dtype=F32)[:, None] * inv_freq[None, :]
    cos, sin = jnp.cos(ang), jnp.sin(ang)
    ones = jnp.ones((s_len, B_NOPE), F32)
    zeros = lambda n: jnp.zeros((s_len, n), F32)
    pad = HEAD_PAD - B_NOPE - B_ROPE
    c_tab = jnp.concatenate([cos, cos, zeros(pad)], axis=1)
    s_tab = jnp.concatenate([-sin, sin, zeros(pad)], axis=1)
    q_scale = (B_NOPE + B_ROPE) ** -0.5 * LOG2E
    return {'cq': q_scale * jnp.concatenate([ones, c_tab], axis=1),
            'sq': q_scale * jnp.concatenate([zeros(B_NOPE), s_tab], axis=1),
            'ck': jnp.concatenate([zeros(B_NOPE), c_tab], axis=1),
            'sk': jnp.concatenate([zeros(B_NOPE), s_tab], axis=1)}


def _dilated_bias():
    e = jnp.arange(BIAS_ROWS)[:, None]
    ql = jnp.arange(TQ_A)[None, :]
    dist = jnp.abs(e - 2 * HALO - ql)
    cnt = sum(((dist % d == 0) & (dist <= (w // (2 * d)) * d)).astype(F32) for w, d in DILATED_PATTERNS)
    base = jnp.where(cnt > 0, jnp.log2(jnp.maximum(cnt, 1.0)), NEG)
    base, dist = lax.optimization_barrier((base, dist.astype(F32)))
    slopes = jnp.asarray(2.0 ** (-8.0 * np.arange(1, A_HEADS + 1) / A_HEADS), dtype=F32)
    return base[None] - (slopes * LOG2E)[:, None, None] * dist[None]


def kernel(x, w_in, q_norm, w_q_up, kv_norm, w_kv_up, sgu_ln_g, sgu_ln_b, sgu_w, sgu_b, mix_norm, w_out,
           ln1_g, ln1_b, w_ff1, b_ff1, w_ff2, b_ff2, ln2_g, ln2_b):
    batch, s_len, _ = x.shape
    assert s_len >= WIN_A and s_len % TQ_A == 0
    t_len = batch * s_len
    tabs = _rope_tables(s_len)
    bias = _dilated_bias()
    x2 = x.reshape(t_len, D_MODEL)
    for l in range(DEPTH):
        pw = _pack_layer(l, w_in, q_norm, w_q_up, kv_norm, w_kv_up, sgu_ln_g, sgu_ln_b, sgu_w, sgu_b, mix_norm,
                         w_out, ln1_g, ln1_b, w_ff1, b_ff1, w_ff2, b_ff2, ln2_g, ln2_b)
        qaT, ka, vaT, qbT, kb, vbT, yc = _inproj(x2, pw, tabs, s_len)
        oaT = _dilated(qaT, ka, vaT, bias, batch, s_len)
        obT = _mla(qbT, kb, vbT, batch, s_len)
        x1 = _merge(oaT, obT, yc, x2, pw)
        x2 = _ffn(x1, pw)
    return x2.reshape(batch, s_len, D_MODEL)
```

```python
import functools
import math

import numpy as np
import jax
import jax.numpy as jnp
from jax import lax
from jax.experimental import pallas as pl
from jax.experimental.pallas import tpu as pltpu

BF16 = jnp.bfloat16
F32 = jnp.float32

D_MODEL = 1024
DEPTH = 2
HEAD_DIM = 64
A_HEADS = 6
A_WIDTH = A_HEADS * HEAD_DIM
DILATED_PATTERNS = ((128, 1), (512, 4), (2048, 16))
B_HEADS = 6
B_NOPE = 64
B_ROPE = 32
B_VDIM = 64
Q_RANK = 192
KV_RANK = 128
B_WIDTH = B_HEADS * B_VDIM
ROPE_THETA = 10000.0
C_GROUPS = 4
C_GROUP_DIM = 64
C_WIDTH = C_GROUPS * C_GROUP_DIM
CHUNK = 128
D_FF = 4 * D_MODEL
ALPHA = (2 * DEPTH) ** 0.25
EPS = 1e-5
NEG = -1e30
LOG2E = math.log2(math.e)

LANES = 128
MXU_N = 256
BF16_SUBLANES = 16
VT_ROWS = B_VDIM + BF16_SUBLANES
HEAD_PAD = LANES

P_QA, P_KA, P_VA = 0, A_WIDTH, 2 * A_WIDTH
P_CKV = 3 * A_WIDTH
P_CQ = P_CKV + KV_RANK
P_KPE = P_CQ + 256
P_U = P_KPE + LANES
P_V = P_U + C_WIDTH
P_PACKED = P_V + C_WIDTH

HALO = max((w // (2 * d)) * d for w, d in DILATED_PATTERNS)
TQ_A = 256
WIN_A = TQ_A + 2 * HALO
BIAS_ROWS = WIN_A + 2 * HALO

VMEM_LIMIT = 48 * 1024 * 1024
INPROJ_TM, INPROJ_SUB = 1024, 256
DIL_CHUNK, DIL_UNROLL = 8192, 4
MLA_TQ, MLA_TK, MLA_UNROLL = 2048, 256, 32
MERGE_TM, MERGE_SUB = 1024, 256
FFN_TM, FFN_SUB, FFN_CHUNK = 1024, 256, 1024


def _rms_scale(x):
    return x * lax.rsqrt(jnp.mean(x * x, axis=-1, keepdims=True) + EPS)


def _layer_norm(x, g, b):
    mu = jnp.mean(x, axis=-1, keepdims=True)
    xc = x - mu
    var = jnp.mean(xc * xc, axis=-1, keepdims=True)
    return xc * lax.rsqrt(var + EPS) * g + b


def _dot(a, b):
    return jnp.dot(a, b, preferred_element_type=F32)


def _tile(n, pref):
    t = min(pref, n)
    assert n % t == 0
    return t


def _inproj_kernel(x_ref, w_ref, qn_ref, wq_ref, kvn_ref, wk_ref, wv_ref, cq_ref, sq_ref, ck_ref, sk_ref,
                   lng_ref, lnb_ref, ws_ref, bs_ref, mnc_ref,
                   qaT_ref, ka_ref, vaT_ref, qbT_ref, kb_ref, vbT_ref, yc_ref):
    sub = INPROJ_SUB
    ones_tile = (lax.broadcasted_iota(jnp.int32, (BF16_SUBLANES, sub), 0) == 0).astype(BF16)
    zeros_half = jnp.zeros((HEAD_DIM, sub), BF16)
    group = lax.broadcasted_iota(jnp.int32, (CHUNK, C_WIDTH), 1) // C_GROUP_DIM

    def put_values_t(vT, out_ref, width, rows):
        for hd in range(A_HEADS):
            out_ref[hd * VT_ROWS:hd * VT_ROWS + width, rows] = vT[hd * width:(hd + 1) * width].astype(BF16)
            out_ref[hd * VT_ROWS + width:(hd + 1) * VT_ROWS, rows] = ones_tile

    for r in range(0, x_ref.shape[0], sub):
        rows = slice(r, r + sub)
        h = _dot(x_ref[rows, :].astype(BF16), w_ref[...])

        qaT = (h[:, P_QA:P_QA + A_WIDTH] * (HEAD_DIM ** -0.5 * LOG2E)).T.astype(BF16)
        for hd in range(A_HEADS):
            q_h = qaT[hd * HEAD_DIM:(hd + 1) * HEAD_DIM]
            lo, hi = (q_h, zeros_half) if hd % 2 == 0 else (zeros_half, q_h)
            qaT_ref[hd * HEAD_PAD:hd * HEAD_PAD + HEAD_DIM, rows] = lo
            qaT_ref[hd * HEAD_PAD + HEAD_DIM:(hd + 1) * HEAD_PAD, rows] = hi
        ka_ref[rows, :] = h[:, P_KA:P_KA + A_WIDTH].astype(BF16)
        put_values_t(h[:, P_VA:P_VA + A_WIDTH].T, vaT_ref, HEAD_DIM, rows)

        c_kv = (_rms_scale(h[:, P_CKV:P_CKV + KV_RANK]) * kvn_ref[...]).astype(BF16)
        kmat = _dot(c_kv, wk_ref[...])
        vmat = _dot(c_kv, wv_ref[...])
        g_k = h[:, P_KPE:P_KPE + LANES]
        kpe = g_k * ck_ref[rows, :] + pltpu.roll(g_k, 96, 1) * sk_ref[rows, :]
        c_q = (_rms_scale(h[:, P_CQ:P_CQ + Q_RANK]) * qn_ref[...]).astype(BF16)
        qmat = _dot(c_q, wq_ref[...])
        cq_t, sq_t = cq_ref[rows, :], sq_ref[rows, :]
        for hd in range(B_HEADS):
            sl = slice(hd * HEAD_PAD, (hd + 1) * HEAD_PAD)
            kb_ref[rows, sl] = (kmat[:, sl] + kpe).astype(BF16)
            g_q = qmat[:, sl]
            q_h = g_q * cq_t + pltpu.roll(g_q, 96, 1) * sq_t
            qbT_ref[sl, rows] = q_h.T.astype(BF16)
        put_values_t(vmat.T, vbT_ref, B_VDIM, rows)

        z = jax.nn.gelu(h[:, P_U:P_U + 2 * C_WIDTH])
        u = z[:, :C_WIDTH]
        v = _layer_norm(z[:, C_WIDTH:], lng_ref[...], lnb_ref[...])
        for c in range(sub // CHUNK):
            crows = slice(c * CHUNK, (c + 1) * CHUNK)
            vc = v[crows]
            stack = jnp.concatenate([jnp.where(group == g, vc, 0.0) for g in range(C_GROUPS)], axis=0).astype(BF16)
            mixed = _dot(ws_ref[...], stack) + bs_ref[...]
            yc = u[crows] * mixed
            yc_ref[r + c * CHUNK:r + (c + 1) * CHUNK, :] = (_rms_scale(yc) * mnc_ref[...]).astype(BF16)


def _inproj(x2, pw, tabs, s_len):
    t_len = x2.shape[0]
    tm = _tile(s_len, INPROJ_TM)
    nt = t_len // tm
    ns = s_len // tm
    row = lambda i: (i, 0)
    col = lambda i: (0, i)
    fixed = lambda i: (0, 0)
    pos = lambda i: (i % ns, 0)
    full = lambda a: pl.BlockSpec(a.shape, fixed)
    tab = pl.BlockSpec((tm, LANES), pos)
    in_specs = [pl.BlockSpec((tm, D_MODEL), row), full(pw['w_in']), full(pw['q_norm']), full(pw['w_q']),
                full(pw['kv_norm']), full(pw['w_k']), full(pw['w_v']), tab, tab, tab, tab,
                full(pw['sgu_g']), full(pw['sgu_b']), full(pw['ws_cat']), full(pw['bs_tab']), full(pw['mn_c'])]
    out_shape = [jax.ShapeDtypeStruct((A_HEADS * HEAD_PAD, t_len), BF16),
                 jax.ShapeDtypeStruct((t_len, A_WIDTH), BF16),
                 jax.ShapeDtypeStruct((A_HEADS * VT_ROWS, t_len), BF16),
                 jax.ShapeDtypeStruct((B_HEADS * HEAD_PAD, t_len), BF16),
                 jax.ShapeDtypeStruct((t_len, B_HEADS * HEAD_PAD), BF16),
                 jax.ShapeDtypeStruct((B_HEADS * VT_ROWS, t_len), BF16),
                 jax.ShapeDtypeStruct((t_len, C_WIDTH), BF16)]
    out_specs = [pl.BlockSpec((A_HEADS * HEAD_PAD, tm), col),
                 pl.BlockSpec((tm, A_WIDTH), row),
                 pl.BlockSpec((A_HEADS * VT_ROWS, tm), col),
                 pl.BlockSpec((B_HEADS * HEAD_PAD, tm), col),
                 pl.BlockSpec((tm, B_HEADS * HEAD_PAD), row),
                 pl.BlockSpec((B_HEADS * VT_ROWS, tm), col),
                 pl.BlockSpec((tm, C_WIDTH), row)]
    return pl.pallas_call(
        _inproj_kernel, grid=(nt,), in_specs=in_specs, out_specs=out_specs, out_shape=out_shape,
        compiler_params=pltpu.CompilerParams(dimension_semantics=("arbitrary",), vmem_limit_bytes=VMEM_LIMIT),
        name="inproj",
    )(x2, pw['w_in'], pw['q_norm'], pw['w_q'], pw['kv_norm'], pw['w_k'], pw['w_v'],
      tabs['cq'], tabs['sq'], tabs['ck'], tabs['sk'],
      pw['sgu_g'], pw['sgu_b'], pw['ws_cat'], pw['bs_tab'], pw['mn_c'])


def _dilated_kernel(qT_ref, k_ref, vT_ref, bias_ref, o_ref, s_ref, *, s_len):
    nblk = qT_ref.shape[1] // TQ_A
    first = pl.program_id(2) * nblk

    def window(i):
        t0 = (first + i) * TQ_A
        w0 = jnp.clip(t0 - HALO, 0, s_len - WIN_A)
        b0 = w0 - t0 + 2 * HALO
        return pl.multiple_of(w0, TQ_A), pl.multiple_of(b0, TQ_A), pl.multiple_of(i * TQ_A, TQ_A)

    def scores(i, slot, hh):
        w0, b0, q0 = window(i)
        qT = qT_ref[hh * HEAD_PAD:(hh + 1) * HEAD_PAD, pl.ds(q0, TQ_A)]
        s_h = _dot(k_ref[pl.ds(w0, WIN_A), :], qT) + bias_ref[hh, pl.ds(b0, WIN_A), :]
        s_ref[slot, :, hh * TQ_A:(hh + 1) * TQ_A] = s_h
        return jnp.max(s_h, axis=0, keepdims=True)

    def finish(i, slot, hh, m_h):
        w0, _, q0 = window(i)
        p = jnp.exp2(s_ref[slot, :, hh * TQ_A:(hh + 1) * TQ_A] - m_h).astype(BF16)
        acc = _dot(vT_ref[hh * VT_ROWS:(hh + 1) * VT_ROWS, pl.ds(w0, WIN_A)], p)
        o_ref[hh * HEAD_DIM:(hh + 1) * HEAD_DIM, pl.ds(q0, TQ_A)] = acc[:HEAD_DIM] / acc[HEAD_DIM:HEAD_DIM + 1]

    def body(i, m):
        for u in range(DIL_UNROLL):
            j = DIL_UNROLL * i + u
            m_next = []
            for hh in range(2):
                m_next.append(scores(jnp.minimum(j + 1, nblk - 1), (u + 1) % 2, hh))
                finish(j, u % 2, hh, m[hh])
            m = tuple(m_next)
        return m

    lax.fori_loop(0, nblk // DIL_UNROLL, body, (scores(0, 0, 0), scores(0, 0, 1)))


def _dilated(qaT, ka, vaT, bias, batch, s_len):
    chunk = _tile(s_len, DIL_CHUNK)
    assert (chunk // TQ_A) % DIL_UNROLL == 0 and DIL_UNROLL % 2 == 0
    nc = s_len // chunk
    t_len = batch * s_len
    once = pl.Buffered(1)
    return pl.pallas_call(
        functools.partial(_dilated_kernel, s_len=s_len),
        grid=(batch, A_HEADS // 2, nc),
        in_specs=[pl.BlockSpec((2 * HEAD_PAD, chunk), lambda b, h, c: (h, b * nc + c)),
                  pl.BlockSpec((s_len, LANES), lambda b, h, c: (b, h), pipeline_mode=once),
                  pl.BlockSpec((2 * VT_ROWS, s_len), lambda b, h, c: (h, b), pipeline_mode=once),
                  pl.BlockSpec((2, BIAS_ROWS, TQ_A), lambda b, h, c: (h, 0, 0), pipeline_mode=once)],
        out_specs=pl.BlockSpec((2 * HEAD_DIM, chunk), lambda b, h, c: (h, b * nc + c)),
        out_shape=jax.ShapeDtypeStruct((A_WIDTH, t_len), F32),
        scratch_shapes=[pltpu.VMEM((2, WIN_A, 2 * TQ_A), F32)],
        compiler_params=pltpu.CompilerParams(dimension_semantics=("arbitrary",) * 3, vmem_limit_bytes=VMEM_LIMIT),
        name="dilated",
    )(qaT, ka, vaT, bias)


def _mla_kernel(qT_ref, k_ref, vT_ref, o_ref, acc_ref, s_ref, *, tq, tk, unroll):
    s_len = k_ref.shape[0]
    nk = s_len // tk

    def query_tile(qi, _):
        q0 = pl.multiple_of(qi * tq, tq)
        qT = qT_ref[:, pl.ds(q0, tq)]
        acc_ref[...] = jnp.zeros_like(acc_ref)

        def scores(j, slot):
            off = pl.multiple_of(j * tk, tk)
            s = _dot(k_ref[pl.ds(off, tk), :], qT)
            s_ref[slot, :, :tq] = s
            return jnp.max(s, axis=0, keepdims=True)

        def step(j, slot, m_prev, m_tile):
            off = pl.multiple_of(j * tk, tk)
            off_n = pl.multiple_of(jnp.minimum(j + 1, nk - 1) * tk, tk)
            m_new = jnp.maximum(m_prev, m_tile)
            alpha = jnp.exp2(m_prev - m_new)
            k_next = k_ref[pl.ds(off_n, tk), :]
            vT = vT_ref[:, pl.ds(off, tk)]
            m_parts = []
            for n in range(tq // MXU_N):
                cols = slice(n * MXU_N, (n + 1) * MXU_N)
                s_n = _dot(k_next, qT[:, cols])
                s_ref[1 - slot, :, cols] = s_n
                m_parts.append(jnp.max(s_n, axis=0, keepdims=True))
                p_n = jnp.exp2(s_ref[slot, :, cols] - m_new[:, cols]).astype(BF16)
                acc_ref[:, cols] = acc_ref[:, cols] * alpha[:, cols] + _dot(vT, p_n)
            return m_new, jnp.concatenate(m_parts, axis=1)

        def body(i, carry):
            for u in range(unroll):
                carry = step(unroll * i + u, u % 2, *carry)
            return carry

        lax.fori_loop(0, nk // unroll, body, (jnp.full((1, tq), NEG, F32), scores(0, 0)))
        acc = acc_ref[...]
        o_ref[:, pl.ds(q0, tq)] = acc[:B_VDIM] / acc[B_VDIM:B_VDIM + 1]
        return 0

    lax.fori_loop(0, s_len // tq, query_tile, 0)


def _mla(qbT, kb, vbT, batch, s_len):
    tq, tk = _tile(s_len, MLA_TQ), _tile(s_len, MLA_TK)
    nk = s_len // tk
    unroll = min(MLA_UNROLL, nk)
    assert nk % unroll == 0 and unroll % 2 == 0
    t_len = batch * s_len
    return pl.pallas_call(
        functools.partial(_mla_kernel, tq=tq, tk=tk, unroll=unroll),
        grid=(batch, B_HEADS),
        in_specs=[pl.BlockSpec((HEAD_PAD, s_len), lambda b, h: (h, b)),
                  pl.BlockSpec((s_len, HEAD_PAD), lambda b, h: (b, h)),
                  pl.BlockSpec((VT_ROWS, s_len), lambda b, h: (h, b))],
        out_specs=pl.BlockSpec((B_VDIM, s_len), lambda b, h: (h, b)),
        out_shape=jax.ShapeDtypeStruct((B_WIDTH, t_len), F32),
        scratch_shapes=[pltpu.VMEM((VT_ROWS, tq), F32), pltpu.VMEM((2, tk, tq), F32)],
        compiler_params=pltpu.CompilerParams(dimension_semantics=("arbitrary",) * 2, vmem_limit_bytes=VMEM_LIMIT),
        name="mla",
    )(qbT, kb, vbT)


def _merge_kernel(oaT_ref, obT_ref, yc_ref, x_ref, wo_ref, mna_ref, mnb_ref, g_ref, b_ref, o_ref):
    for r in range(0, x_ref.shape[0], MERGE_SUB):
        rows = slice(r, r + MERGE_SUB)
        ya = (_rms_scale(oaT_ref[:, rows].T) * mna_ref[...]).astype(BF16)
        yb = (_rms_scale(obT_ref[:, rows].T) * mnb_ref[...]).astype(BF16)
        y = _dot(jnp.concatenate([ya, yb, yc_ref[rows, :]], axis=1), wo_ref[...])
        o_ref[rows, :] = _layer_norm(ALPHA * x_ref[rows, :] + y, g_ref[...], b_ref[...])


def _merge(oaT, obT, yc, x2, pw):
    t_len = x2.shape[0]
    tm = _tile(t_len, MERGE_TM)
    row = lambda i: (i, 0)
    col = lambda i: (0, i)
    full = lambda a: pl.BlockSpec(a.shape, lambda i: (0, 0))
    return pl.pallas_call(
        _merge_kernel, grid=(t_len // tm,),
        in_specs=[pl.BlockSpec((A_WIDTH, tm), col), pl.BlockSpec((B_WIDTH, tm), col),
                  pl.BlockSpec((tm, C_WIDTH), row), pl.BlockSpec((tm, D_MODEL), row),
                  full(pw['w_out']), full(pw['mn_a']), full(pw['mn_b']), full(pw['ln1_g']), full(pw['ln1_b'])],
        out_specs=pl.BlockSpec((tm, D_MODEL), row),
        out_shape=jax.ShapeDtypeStruct((t_len, D_MODEL), F32),
        compiler_params=pltpu.CompilerParams(dimension_semantics=("arbitrary",), vmem_limit_bytes=VMEM_LIMIT),
        name="merge",
    )(oaT, obT, yc, x2, pw['w_out'], pw['mn_a'], pw['mn_b'], pw['ln1_g'], pw['ln1_b'])


def _ffn_kernel(x_ref, w1_ref, b1_ref, w2_ref, b2_ref, g_ref, b_ref, o_ref):
    for r in range(0, x_ref.shape[0], FFN_SUB):
        rows = slice(r, r + FFN_SUB)
        x = x_ref[rows, :]
        xb = x.astype(BF16)
        f = b2_ref[...]
        for c in range(0, D_FF, FFN_CHUNK):
            hid = _dot(xb, w1_ref[:, c:c + FFN_CHUNK]) + b1_ref[:, c:c + FFN_CHUNK]
            hid = jnp.square(jnp.maximum(hid, 0.0)).astype(BF16)
            f = f + _dot(hid, w2_ref[c:c + FFN_CHUNK, :])
        o_ref[rows, :] = _layer_norm(ALPHA * x + f, g_ref[...], b_ref[...])


def _ffn(x2, pw):
    t_len = x2.shape[0]
    tm = _tile(t_len, FFN_TM)
    fixed = lambda i: (0, 0)
    once = pl.Buffered(1)
    return pl.pallas_call(
        _ffn_kernel, grid=(t_len // tm,),
        in_specs=[pl.BlockSpec((tm, D_MODEL), lambda i: (i, 0)),
                  pl.BlockSpec((D_MODEL, D_FF), fixed, pipeline_mode=once),
                  pl.BlockSpec((1, D_FF), fixed),
                  pl.BlockSpec((D_FF, D_MODEL), fixed, pipeline_mode=once),
                  pl.BlockSpec((1, D_MODEL), fixed), pl.BlockSpec((1, D_MODEL), fixed),
                  pl.BlockSpec((1, D_MODEL), fixed)],
        out_specs=pl.BlockSpec((tm, D_MODEL), lambda i: (i, 0)),
        out_shape=jax.ShapeDtypeStruct((t_len, D_MODEL), F32),
        compiler_params=pltpu.CompilerParams(dimension_semantics=("arbitrary",), vmem_limit_bytes=VMEM_LIMIT),
        name="ffn",
    )(x2, pw['w_ff1'], pw['b_ff1'], pw['w_ff2'], pw['b_ff2'], pw['ln2_g'], pw['ln2_b'])


def _swap_halves(w):
    half = w.shape[-1] // 2
    return jnp.concatenate([w[..., half:], w[..., :half]], axis=-1)


def _pack_layer(l, w_in, q_norm, w_q_up, kv_norm, w_kv_up, sgu_ln_g, sgu_ln_b, sgu_w, sgu_b, mix_norm,
                w_out, ln1_g, ln1_b, w_ff1, b_ff1, w_ff2, b_ff2, ln2_g, ln2_b):
    w = w_in[l]
    zc = lambda n: jnp.zeros((D_MODEL, n), F32)
    off_cq = 3 * A_WIDTH
    off_ckv = off_cq + Q_RANK
    off_kpe = off_ckv + KV_RANK
    off_u = off_kpe + B_ROPE
    kpe = w[:, off_kpe:off_u]
    w_packed = jnp.concatenate([w[:, :off_cq], w[:, off_ckv:off_kpe], w[:, off_cq:off_ckv], zc(64),
                                zc(64), kpe, _swap_halves(kpe), w[:, off_u:]], axis=1)
    wq = w_q_up[l].reshape(Q_RANK, B_HEADS, B_NOPE + B_ROPE)
    wq = jnp.concatenate([wq, _swap_halves(wq[..., B_NOPE:])], axis=-1).reshape(Q_RANK, B_HEADS * HEAD_PAD)
    wkv = w_kv_up[l].reshape(KV_RANK, B_HEADS, B_NOPE + B_VDIM)
    wk = jnp.concatenate([wkv[..., :B_NOPE], jnp.zeros((KV_RANK, B_HEADS, HEAD_PAD - B_NOPE), F32)], axis=-1)
    row = lambda a: a.reshape(1, -1)
    mn = mix_norm[l]
    return {
        'w_in': w_packed.astype(BF16),
        'q_norm': row(q_norm[l]), 'w_q': wq.astype(BF16),
        'kv_norm': row(kv_norm[l]), 'w_k': wk.reshape(KV_RANK, B_HEADS * HEAD_PAD).astype(BF16),
        'w_v': wkv[..., B_NOPE:].reshape(KV_RANK, B_WIDTH).astype(BF16),
        'sgu_g': row(sgu_ln_g[l]), 'sgu_b': row(sgu_ln_b[l]),
        'ws_cat': jnp.concatenate([sgu_w[l][g] for g in range(C_GROUPS)], axis=1).astype(BF16),
        'bs_tab': jnp.repeat(sgu_b[l].T, C_GROUP_DIM, axis=1),
        'mn_a': row(mn[:A_WIDTH]), 'mn_b': row(mn[A_WIDTH:A_WIDTH + B_WIDTH]), 'mn_c': row(mn[A_WIDTH + B_WIDTH:]),
        'w_out': w_out[l].astype(BF16),
        'ln1_g': row(ln1_g[l]), 'ln1_b': row(ln1_b[l]),
        'w_ff1': w_ff1[l].astype(BF16), 'b_ff1': row(b_ff1[l]), 'w_ff2': w_ff2[l].astype(BF16),
        'b_ff2': row(b_ff2[l]), 'ln2_g': row(ln2_g[l]), 'ln2_b': row(ln2_b[l]),
    }


def _rope_tables(s_len):
    inv_freq = ROPE_THETA ** (-jnp.arange(0, B_ROPE, 2, dtype=F32) / B_ROPE)
    ang = jnp.arange(s_len, dtype=F32)[:, None] * inv_freq[None, :]
    cos, sin = jnp.cos(ang), jnp.sin(ang)
    ones = jnp.ones((s_len, B_NOPE), F32)
    zeros = lambda n: jnp.zeros((s_len, n), F32)
    pad = HEAD_PAD - B_NOPE - B_ROPE
    c_tab = jnp.concatenate([cos, cos, zeros(pad)], axis=1)
    s_tab = jnp.concatenate([-sin, sin, zeros(pad)], axis=1)
    q_scale = (B_NOPE + B_ROPE) ** -0.5 * LOG2E
    return {'cq': q_scale * jnp.concatenate([ones, c_tab], axis=1),
            'sq': q_scale * jnp.concatenate([zeros(B_NOPE), s_tab], axis=1),
            'ck': jnp.concatenate([zeros(B_NOPE), c_tab], axis=1),
            'sk': jnp.concatenate([zeros(B_NOPE), s_tab], axis=1)}


def _dilated_bias():
    e = jnp.arange(BIAS_ROWS)[:, None]
    ql = jnp.arange(TQ_A)[None, :]
    dist = jnp.abs(e - 2 * HALO - ql)
    cnt = sum(((dist % d == 0) & (dist <= (w // (2 * d)) * d)).astype(F32) for w, d in DILATED_PATTERNS)
    base = jnp.where(cnt > 0, jnp.log2(jnp.maximum(cnt, 1.0)), NEG)
    base, dist = lax.optimization_barrier((base, dist.astype(F32)))
    slopes = jnp.asarray(2.0 ** (-8.0 * np.arange(1, A_HEADS + 1) / A_HEADS), dtype=F32)
    return base[None] - (slopes * LOG2E)[:, None, None] * dist[None]


def kernel(x, w_in, q_norm, w_q_up, kv_norm, w_kv_up, sgu_ln_g, sgu_ln_b, sgu_w, sgu_b, mix_norm, w_out,
           ln1_g, ln1_b, w_ff1, b_ff1, w_ff2, b_ff2, ln2_g, ln2_b):
    batch, s_len, _ = x.shape
    assert s_len >= WIN_A and s_len % TQ_A == 0
    t_len = batch * s_len
    tabs = _rope_tables(s_len)
    bias = _dilated_bias()
    x2 = x.reshape(t_len, D_MODEL)
    for l in range(DEPTH):
        pw = _pack_layer(l, w_in, q_norm, w_q_up, kv_norm, w_kv_up, sgu_ln_g, sgu_ln_b, sgu_w, sgu_b, mix_norm,
                         w_out, ln1_g, ln1_b, w_ff1, b_ff1, w_ff2, b_ff2, ln2_g, ln2_b)
        qaT, ka, vaT, qbT, kb, vbT, yc = _inproj(x2, pw, tabs, s_len)
        oaT = _dilated(qaT, ka, vaT, bias, batch, s_len)
        obT = _mla(qbT, kb, vbT, batch, s_len)
        x1 = _merge(oaT, obT, yc, x2, pw)
        x2 = _ffn(x1, pw)
    return x2.reshape(batch, s_len, D_MODEL)
```
